```python
import math
import jax, jax.numpy as jnp
from jax import lax
import numpy as np

D_MODEL = 4096
BATCH = 2
SEQ = 8192
DEPTH = 2

H_A = 8
DH_A = 128
R_Q = 768
R_KV = 256
H_I = 16
D_I = 64
TOPK_MAX = 256
Q_BLOCK = 128
N_BUCKETS = 32
MAX_DIST = 128
H_B = 24
N_B = 64
W_B = H_B * N_B
LORA_W = 128
LORA_A = 128
LORA_G = 480
H_C = 12
DK_C = 128
DV_C = 128
W_C = H_C * DK_C
CHUNK = 64
D_MIX = H_A * DH_A + W_B + W_C
D_FF = 4 * D_MODEL
EPS = 1e-6
GN_EPS = 64e-5
A_COLS = R_Q + R_KV + D_I + H_I
B_COLS = 3 * W_B + LORA_W + LORA_A + LORA_G
C_COLS = 4 * W_C
N_IN = A_COLS + B_COLS + C_COLS

kernel_name = "hybrid_dsa_rwkv7_hgrn2_trunk"


def rms_norm(x, g, eps=EPS):
    xf = x.astype(jnp.float32)
    y = xf * lax.rsqrt(jnp.mean(xf * xf, axis=-1, keepdims=True) + eps)
    return (y * g.astype(jnp.float32)).astype(x.dtype)


def token_shift(p, mu):
    prev = jnp.pad(p, ((0, 0), (1, 0), (0, 0)))[:, :-1]
    return p + mu * (prev - p)


def rel_bucket(n):
    max_exact = N_BUCKETS // 2
    nf = jnp.maximum(n, 1).astype(jnp.float32)
    large = max_exact + (jnp.log(nf / max_exact) / math.log(MAX_DIST / max_exact)
                         * (N_BUCKETS - max_exact)).astype(jnp.int32)
    large = jnp.minimum(large, N_BUCKETS - 1)
    return jnp.where(n < max_exact, n, large)


def dsa_attention(cq, ckv, k_idx, w_idx, q_norm_g, kv_norm_g, w_uq, w_uk, w_uv, w_qidx, rel_bias):
    B, L, _ = cq.shape
    topk = min(TOPK_MAX, L // 4)
    nb = L // Q_BLOCK
    cq = rms_norm(cq, q_norm_g)
    ckv = rms_norm(ckv, kv_norm_g)
    q = jnp.einsum('blr,rhd->blhd', cq, w_uq)
    q_lat = jnp.einsum('blhd,hrd->blhr', q, w_uk)
    q_idx = jnp.einsum('blr,rhd->blhd', cq, w_qidx)
    w_idx = w_idx * (H_I ** -0.5 * D_I ** -0.5)
    pos = jnp.arange(L, dtype=jnp.int32)

    def to_blocks(a):
        return jnp.moveaxis(a.reshape((B, nb, Q_BLOCK) + a.shape[2:]), 1, 0)

    def block(args):
        qi, wi, ql, t = args
        s = jax.nn.relu(jnp.einsum('bqhd,bsd->bqhs', qi, k_idx))
        score = jnp.einsum('bqhs,bqh->bqs', s, wi).astype(jnp.float32)
        causal = pos[None, :] <= t[:, None]
        score = jnp.where(causal[None], score, -jnp.inf)
        _, sel = lax.top_k(score, topk)
        kv_sel = jax.vmap(lambda kv, i: kv[i])(ckv, sel)
        logits = jnp.einsum('bqhr,bqkr->bqhk', ql, kv_sel).astype(jnp.float32) * (DH_A ** -0.5)
        dist = t[None, :, None] - sel
        bias = rel_bias[rel_bucket(jnp.maximum(dist, 0))]
        logits = logits + jnp.moveaxis(bias, -1, 2).astype(jnp.float32)
        logits = jnp.where((dist >= 0)[:, :, None, :], logits, -jnp.inf)
        prob = jax.nn.softmax(logits, axis=-1).astype(kv_sel.dtype)
        return jnp.einsum('bqhk,bqkr->bqhr', prob, kv_sel)

    o_lat = lax.map(block, (to_blocks(q_idx), to_blocks(w_idx), to_blocks(q_lat),
                            pos.reshape(nb, Q_BLOCK)))
    o_lat = jnp.moveaxis(o_lat, 0, 1).reshape(B, L, H_A, R_KV)
    out = jnp.einsum('blhr,hrd->blhd', o_lat, w_uv)
    return out.reshape(B, L, H_A * DH_A)


def rwkv7_time_mix(p, mu, w0, w_up, a0, a_up, g_up, k_k, k_a, r_k, lnx_w, lnx_b):
    B, L, _ = p.shape
    p = token_shift(p, mu)
    r, k, v, wd, ad, gd = jnp.split(
        p, [W_B, 2 * W_B, 3 * W_B, 3 * W_B + LORA_W, 3 * W_B + LORA_W + LORA_A], axis=-1)
    w = -jax.nn.softplus(-(w0 + jnp.tanh(wd) @ w_up)) - 0.5
    decay = jnp.exp(-jnp.exp(w.astype(jnp.float32)))
    a = jax.nn.sigmoid(a0 + ad @ a_up)
    g = jax.nn.sigmoid(gd) @ g_up
    hs = lambda t: t.reshape(B, L, H_B, N_B)
    kk = hs(k * k_k).astype(jnp.float32)
    kk = kk * lax.rsqrt(jnp.maximum(jnp.sum(kk * kk, -1, keepdims=True), 1e-24))
    k = k * (1 + (a - 1) * k_a)
    r_h, k_h, v_h, a_h, d_h = hs(r), hs(k), hs(v), hs(a), hs(decay)

    def step(S, inp):
        rt, dt, kt, vt, kkt, at = inp
        sa = jnp.einsum('bhij,bhj->bhi', S, -kkt)
        S = (S * dt[:, :, None, :] + sa[..., None] * (kkt * at)[:, :, None, :]
             + vt[..., None] * kt[:, :, None, :])
        return S, jnp.einsum('bhij,bhj->bhi', S, rt)

    tm = lambda t: jnp.moveaxis(t.astype(jnp.float32), 1, 0)
    S0 = jnp.zeros((B, H_B, N_B, N_B), jnp.float32)
    _, y = lax.scan(step, S0, (tm(r_h), tm(d_h), tm(k_h), tm(v_h), tm(kk), tm(a_h)))
    y = jnp.moveaxis(y, 0, 1)
    mean = jnp.mean(y, -1, keepdims=True)
    var = jnp.mean(jnp.square(y - mean), -1, keepdims=True)
    y = ((y - mean) * lax.rsqrt(var + GN_EPS)).reshape(B, L, W_B) * lnx_w + lnx_b
    bonus = jnp.sum((r_h * k_h * r_k).astype(jnp.float32), -1, keepdims=True) * v_h
    y = y + bonus.reshape(B, L, W_B)
    return (y * g).astype(p.dtype)


def hgrn2(p, lb, onorm_g):
    B, L, _ = p.shape
    q, f, i, g = jnp.split(p, 4, axis=-1)
    q = jax.nn.silu(q)
    ff = f.astype(jnp.float32)
    log_f = jnp.logaddexp(jnp.log(lb), jnp.log1p(-lb) + jax.nn.log_sigmoid(ff))
    k = (1 - lb) * jax.nn.sigmoid(-ff)
    nc = L // CHUNK

    def chunks(t, d):
        t = t.astype(jnp.float32).reshape(B, nc, CHUNK, H_C, d)
        return jnp.transpose(t, (1, 0, 3, 2, 4))

    causal = jnp.tril(jnp.ones((CHUNK, CHUNK), bool))

    def chunk_step(S, inp):
        qc, kc, vc, lfc = inp
        b = jnp.cumsum(lfc, axis=2)
        diff = b[:, :, :, None, :] - b[:, :, None, :, :]
        dec = jnp.exp(jnp.where(causal[:, :, None], diff, -jnp.inf))
        A = jnp.einsum('bhtd,bhsd,bhtsd->bhts', qc, kc, dec)
        o = (jnp.einsum('bhts,bhsv->bhtv', A, vc)
             + jnp.einsum('bhtd,bhdv->bhtv', qc * jnp.exp(b), S))
        b_end = b[:, :, -1:, :]
        S = (jnp.exp(b_end[:, :, 0, :])[..., None] * S
             + jnp.einsum('bhsd,bhsv->bhdv', kc * jnp.exp(b_end - b), vc))
        return S, o

    S0 = jnp.zeros((B, H_C, DK_C, DV_C), jnp.float32)
    _, o = lax.scan(chunk_step, S0, (chunks(q, DK_C), chunks(k, DK_C), chunks(i, DV_C), chunks(log_f, DK_C)))
    o = jnp.transpose(o, (1, 0, 3, 2, 4)).reshape(B, L, H_C, DV_C)
    o = o * lax.rsqrt(jnp.mean(o * o, -1, keepdims=True) + EPS)
    o = o.reshape(B, L, W_C) * onorm_g * jax.nn.silu(g.astype(jnp.float32))
    return o.astype(p.dtype)


def setup_inputs(seed: int = 0) -> dict:
    key = jax.random.key(seed)
    ks = iter(jax.random.split(key, 40))
    nrm = lambda shape, scale: jax.random.normal(next(ks), shape, jnp.float32) * scale
    gain = lambda shape: 1.0 + nrm(shape, 0.02)
    L = DEPTH
    return {
        "x": nrm((BATCH, SEQ, D_MODEL), 1.0),
        "c": nrm((BATCH, D_MODEL), 1.0),
        "rel_bias": nrm((N_BUCKETS, H_A), 0.5),
        "hgrn_lb": nrm((L, W_C), 1.0),
        "ada_w": nrm((L, D_MODEL, 6 * D_MODEL), 0.2 * D_MODEL ** -0.5),
        "ada_b": nrm((L, 6 * D_MODEL), 0.02),
        "norm_g": gain((L, 4, D_MODEL)),
        "w_in": nrm((L, D_MODEL, N_IN), D_MODEL ** -0.5),
        "w_out": nrm((L, D_MIX, D_MODEL), D_MIX ** -0.5),
        "mla_q_norm": gain((L, R_Q)),
        "mla_kv_norm": gain((L, R_KV)),
        "w_uq": nrm((L, R_Q, H_A, DH_A), R_Q ** -0.5),
        "w_uk": nrm((L, H_A, R_KV, DH_A), R_KV ** -0.5),
        "w_uv": nrm((L, H_A, R_KV, DH_A), R_KV ** -0.5),
        "w_qidx": nrm((L, R_Q, H_I, D_I), R_Q ** -0.5),
        "rwkv_mu": jax.random.uniform(next(ks), (L, B_COLS), jnp.float32),
        "rwkv_w0": jax.random.uniform(next(ks), (L, W_B), jnp.float32, -6.0, -1.0),
        "rwkv_w_up": nrm((L, LORA_W, W_B), 0.5 * LORA_W ** -0.5),
        "rwkv_a0": nrm((L, W_B), 0.5),
        "rwkv_a_up": nrm((L, LORA_A, W_B), 0.5 * LORA_A ** -0.5),
        "rwkv_g_up": nrm((L, LORA_G, W_B), LORA_G ** -0.5),
        "rwkv_k_k": 0.85 + nrm((L, W_B), 0.1),
        "rwkv_k_a": 1.0 + nrm((L, W_B), 0.1),
        "rwkv_r_k": nrm((L, H_B, N_B), 0.1),
        "rwkv_lnx_w": gain((L, W_B)),
        "rwkv_lnx_b": nrm((L, W_B), 0.02),
        "hgrn_onorm": gain((L, W_C)),
        "w_ff1": nrm((L, D_MODEL, D_FF), D_MODEL ** -0.5),
        "w_ff2": nrm((L, D_FF, D_MODEL), D_FF ** -0.5),
    }


def reference(x, c, rel_bias, hgrn_lb, ada_w, ada_b, norm_g, w_in, w_out, mla_q_norm, mla_kv_norm,
              w_uq, w_uk, w_uv, w_qidx, rwkv_mu, rwkv_w0, rwkv_w_up, rwkv_a0, rwkv_a_up, rwkv_g_up,
              rwkv_k_k, rwkv_k_a, rwkv_r_k, rwkv_lnx_w, rwkv_lnx_b, hgrn_onorm, w_ff1, w_ff2):
    lb_all = jnp.cumsum(jax.nn.softmax(hgrn_lb.astype(jnp.float32), axis=0), axis=0)
    lb_all = lb_all - lb_all[0]
    c_act = jax.nn.silu(c)
    for l in range(DEPTH):
        mod = c_act @ ada_w[l] + ada_b[l]
        sh_m, sc_m, g_m, sh_f, sc_f, g_f = [m[:, None, :] for m in jnp.split(mod, 6, axis=-1)]
        h = rms_norm(x, norm_g[l, 0]) * (1 + sc_m) + sh_m
        p = h @ w_in[l]
        pa, pb, pc = jnp.split(p, [A_COLS, A_COLS + B_COLS], axis=-1)
        cq, ckv, k_idx, w_idx = jnp.split(pa, [R_Q, R_Q + R_KV, R_Q + R_KV + D_I], axis=-1)
        y_a = dsa_attention(cq, ckv, k_idx, w_idx, mla_q_norm[l], mla_kv_norm[l], w_uq[l], w_uk[l],
                            w_uv[l], w_qidx[l], rel_bias)
        y_b = rwkv7_time_mix(pb, rwkv_mu[l], rwkv_w0[l], rwkv_w_up[l], rwkv_a0[l], rwkv_a_up[l],
                             rwkv_g_up[l], rwkv_k_k[l], rwkv_k_a[l], rwkv_r_k[l], rwkv_lnx_w[l],
                             rwkv_lnx_b[l])
        y_c = hgrn2(pc, lb_all[l], hgrn_onorm[l])
        y = jnp.concatenate([y_a, y_b, y_c], axis=-1) @ w_out[l]
        x = x + g_m * rms_norm(y, norm_g[l, 1])
        h = rms_norm(x, norm_g[l, 2]) * (1 + sc_f) + sh_f
        y = jnp.square(jax.nn.relu(h @ w_ff1[l])) @ w_ff2[l]
        x = x + g_f * rms_norm(y, norm_g[l, 3])
    return x
```

```python
import functools
import math

import jax
import jax.numpy as jnp
from jax import lax
from jax.experimental import pallas as pl
from jax.experimental.pallas import tpu as pltpu

H_A, DH_A, R_Q, R_KV, H_I, D_I = 8, 128, 768, 256, 16, 64
TOPK_MAX, Q_BLOCK, N_BUCKETS, MAX_DIST = 256, 128, 32, 128
H_B, N_B, LORA_W, LORA_A, LORA_G = 24, 64, 128, 128, 480
W_B = H_B * N_B
H_C, DK_C, DV_C = 12, 128, 128
W_C = H_C * DK_C
EPS = 1e-6
GN_EPS = 64e-5

LANES = 128
KEY_CHUNK = 256
SUB = 16
LORA_G_PAD = 512
A_PAD = 1280
B_PAD = 3 * W_B + LORA_W + LORA_A + LORA_G_PAD
VMEM_LIMIT = 48 * 1024 * 1024

F32 = jnp.float32
BF16 = jnp.bfloat16
NT_DIMS = (((1,), (1,)), ((), ()))
TN_DIMS = (((0,), (0,)), ((), ()))
INT_MIN = -2 ** 31


def _cparams(*sem):
    return pltpu.CompilerParams(dimension_semantics=sem, vmem_limit_bytes=VMEM_LIMIT)


def _sigmoid(x):
    return 1.0 / (1.0 + jnp.exp(-x))


def _rms(x, eps=EPS):
    return x * lax.rsqrt(jnp.mean(x * x, axis=-1, keepdims=True) + eps)


def _mm_kernel(a_ref, b_ref, o_ref, acc_ref, *, nk, relu2):
    k = pl.program_id(2)

    @pl.when(k == 0)
    def _():
        acc_ref[...] = jnp.zeros_like(acc_ref)

    acc_ref[...] += jnp.dot(a_ref[...], b_ref[...], preferred_element_type=F32)

    @pl.when(k == nk - 1)
    def _():
        r = acc_ref[...]
        if relu2:
            r = jnp.square(jnp.maximum(r, 0.0))
        o_ref[...] = r.astype(o_ref.dtype)


def _matmul(a, b, out_dtype, tm, tn, tk, relu2=False, name="matmul"):
    m, kdim = a.shape
    _, n = b.shape
    tm, tn, tk = min(tm, m), min(tn, n), min(tk, kdim)
    assert m % tm == 0 and n % tn == 0 and kdim % tk == 0, (a.shape, b.shape, tm, tn, tk)
    nk = kdim // tk
    return pl.pallas_call(
        functools.partial(_mm_kernel, nk=nk, relu2=relu2),
        grid=(m // tm, n // tn, nk),
        in_specs=[pl.BlockSpec((tm, tk), lambda i, j, k: (i, k)),
                  pl.BlockSpec((tk, tn), lambda i, j, k: (k, j))],
        out_specs=pl.BlockSpec((tm, tn), lambda i, j, k: (i, j)),
        out_shape=jax.ShapeDtypeStruct((m, n), out_dtype),
        scratch_shapes=[pltpu.VMEM((tm, tn), F32)],
        compiler_params=_cparams("parallel", "parallel", "arbitrary"),
        name=name,
    )(a, b)


def _ada_kernel(c_ref, w_ref, b_ref, o_ref):
    k = pl.program_id(2)

    @pl.when(k == 0)
    def _():
        o_ref[0] = jnp.broadcast_to(b_ref[0], o_ref.shape[1:])

    c = c_ref[...]
    ca = (c * _sigmoid(c)).astype(BF16)
    o_ref[0] += jnp.dot(ca, w_ref[0].astype(BF16), preferred_element_type=F32)


def _ada_mod(c, ada_w, ada_b):
    nl, d, n = ada_w.shape
    bsz = c.shape[0]
    rows = 8
    cp = jnp.zeros((rows, d), F32).at[:bsz].set(c)
    tn, tk = min(2048, n), min(1024, d)
    out = pl.pallas_call(
        _ada_kernel,
        grid=(nl, n // tn, d // tk),
        in_specs=[pl.BlockSpec((rows, tk), lambda l, j, k: (0, k)),
                  pl.BlockSpec((1, tk, tn), lambda l, j, k: (l, k, j)),
                  pl.BlockSpec((1, 1, tn), lambda l, j, k: (l, 0, j))],
        out_specs=pl.BlockSpec((1, rows, tn), lambda l, j, k: (l, 0, j)),
        out_shape=jax.ShapeDtypeStruct((nl, rows, n), F32),
        compiler_params=_cparams("parallel", "parallel", "arbitrary"),
        name="ada_mod",
    )(cp, ada_w, ada_b.reshape(nl, 1, n))
    return out[:, :bsz]


def _norm_mod_kernel(x_ref, g_ref, sc_ref, sh_ref, h_ref):
    y = _rms(x_ref[0]) * g_ref[...]
    h_ref[0] = (y * (1.0 + sc_ref[0]) + sh_ref[0]).astype(h_ref.dtype)


def _norm_mod(x, g, sc, sh, tm=256):
    bsz, l, d = x.shape
    tm = min(tm, l)
    row = pl.BlockSpec((1, tm, d), lambda b, i: (b, i, 0))
    vec = pl.BlockSpec((1, d), lambda b, i: (0, 0))
    bvec = pl.BlockSpec((1, 1, d), lambda b, i: (b, 0, 0))
    return pl.pallas_call(
        _norm_mod_kernel,
        grid=(bsz, l // tm),
        in_specs=[row, vec, bvec, bvec],
        out_specs=row,
        out_shape=jax.ShapeDtypeStruct((bsz, l, d), BF16),
        compiler_params=_cparams("parallel", "parallel"),
        name="norm_mod",
    )(x, g.reshape(1, d), sc, sh)


def _resid_kernel(x_ref, y_ref, gate_ref, g1_ref, g2_ref, sc_ref, sh_ref, xo_ref, h_ref):
    xn = x_ref[0] + gate_ref[0] * (_rms(y_ref[0]) * g1_ref[...])
    xo_ref[0] = xn
    hn = _rms(xn) * g2_ref[...]
    h_ref[0] = (hn * (1.0 + sc_ref[0]) + sh_ref[0]).astype(h_ref.dtype)


def _resid_last_kernel(x_ref, y_ref, gate_ref, g1_ref, xo_ref):
    xo_ref[0] = x_ref[0] + gate_ref[0] * (_rms(y_ref[0]) * g1_ref[...])


def _resid_norm(x, y, gate, g1, nxt=None, tm=128):
    bsz, l, d = x.shape
    tm = min(tm, l)
    row = pl.BlockSpec((1, tm, d), lambda b, i: (b, i, 0))
    vec = pl.BlockSpec((1, d), lambda b, i: (0, 0))
    bvec = pl.BlockSpec((1, 1, d), lambda b, i: (b, 0, 0))
    if nxt is None:
        return pl.pallas_call(
            _resid_last_kernel,
            grid=(bsz, l // tm),
            in_specs=[row, row, bvec, vec],
            out_specs=row,
            out_shape=jax.ShapeDtypeStruct((bsz, l, d), F32),
            compiler_params=_cparams("parallel", "parallel"),
            name="resid_last",
        )(x, y, gate, g1.reshape(1, d)), None
    g2, sc, sh = nxt
    return pl.pallas_call(
        _resid_kernel,
        grid=(bsz, l // tm),
        in_specs=[row, row, bvec, vec, vec, bvec, bvec],
        out_specs=[row, row],
        out_shape=[jax.ShapeDtypeStruct((bsz, l, d), F32), jax.ShapeDtypeStruct((bsz, l, d), BF16)],
        compiler_params=_cparams("parallel", "parallel"),
        name="resid_norm",
    )(x, y, gate, g1.reshape(1, d), g2.reshape(1, d), sc, sh)


def _bias_kernel(rb_ref, o_ref):
    i = lax.broadcasted_iota(jnp.int32, (Q_BLOCK, KEY_CHUNK), 0)
    j = lax.broadcasted_iota(jnp.int32, (Q_BLOCK, KEY_CHUNK), 1)
    max_exact = N_BUCKETS // 2
    for v, off in enumerate((Q_BLOCK, 0, 2 * Q_BLOCK, None)):
        if off is None:
            dist = jnp.full((Q_BLOCK, KEY_CHUNK), 2 * MAX_DIST, jnp.int32)
        else:
            dist = jnp.maximum(i + off - j, 0)
        nf = jnp.maximum(dist, 1).astype(F32)
        large = max_exact + (jnp.log(nf / max_exact) / math.log(MAX_DIST / max_exact)
                             * (N_BUCKETS - max_exact)).astype(jnp.int32)
        large = jnp.minimum(large, N_BUCKETS - 1)
        bucket = jnp.where(dist < max_exact, dist, large)
        for h in range(H_A):
            def body(b, acc, h=h, bucket=bucket):
                return jnp.where(bucket == b, rb_ref[b * H_A + h], acc)
            o_ref[v, h] = lax.fori_loop(0, N_BUCKETS, body, jnp.zeros((Q_BLOCK, KEY_CHUNK), F32))


def _bias_tiles(rel_bias):
    return pl.pallas_call(
        _bias_kernel,
        in_specs=[pl.BlockSpec(memory_space=pltpu.SMEM)],
        out_specs=pl.BlockSpec(memory_space=pltpu.VMEM),
        out_shape=jax.ShapeDtypeStruct((4, H_A, Q_BLOCK, KEY_CHUNK), F32),
        name="dsa_bias_tiles",
    )(rel_bias.reshape(-1))


def _dsa_prep_kernel(pa_ref, gq_ref, gkv_ref, wuq_ref, wuk_ref, wqi_ref,
                     qlat_ref, qidx_ref, widx_ref, kidx_ref, ckv_ref):
    pa = pa_ref[0]
    cq = pa[:, :R_Q]
    ckv = pa[:, R_Q:R_Q + R_KV]
    kid = pa[:, R_Q + R_KV:R_Q + R_KV + D_I]
    wid = pa[:, R_Q + R_KV + LANES:R_Q + R_KV + LANES + H_I]
    cqn = (_rms(cq) * gq_ref[...]).astype(BF16)
    ckv_ref[0] = (_rms(ckv) * gkv_ref[...]).astype(BF16)
    kidx_ref[0] = kid.astype(BF16)
    widx_ref[0] = wid * (H_I ** -0.5 * D_I ** -0.5)
    q = jnp.dot(cqn, wuq_ref[...], preferred_element_type=F32)
    for h in range(H_A):
        qh = q[:, h * DH_A:(h + 1) * DH_A].astype(BF16)
        ql = lax.dot_general(qh, wuk_ref[h], NT_DIMS, preferred_element_type=F32)
        qlat_ref[0, 0, h * Q_BLOCK:(h + 1) * Q_BLOCK, :] = (ql * DH_A ** -0.5).astype(BF16)
    for h in range(H_I):
        qi = jnp.dot(cqn, wqi_ref[h], preferred_element_type=F32)
        qidx_ref[0, 0, h * Q_BLOCK:(h + 1) * Q_BLOCK, :] = qi.astype(BF16)


def _dsa_prep(pa, gq, gkv, wuq, wuk, wqi):
    bsz, l, _ = pa.shape
    nb = l // Q_BLOCK
    full = lambda shp: pl.BlockSpec(shp, lambda b, i: (0,) * len(shp))
    return pl.pallas_call(
        _dsa_prep_kernel,
        grid=(bsz, nb),
        in_specs=[pl.BlockSpec((1, Q_BLOCK, A_PAD), lambda b, i: (b, i, 0)),
                  full((1, R_Q)), full((1, R_KV)), full((R_Q, H_A * DH_A)),
                  full((H_A, R_KV, DH_A)), full((H_I, R_Q, D_I))],
        out_specs=[pl.BlockSpec((1, 1, H_A * Q_BLOCK, R_KV), lambda b, i: (b, i, 0, 0)),
                   pl.BlockSpec((1, 1, H_I * Q_BLOCK, D_I), lambda b, i: (b, i, 0, 0)),
                   pl.BlockSpec((1, Q_BLOCK, H_I), lambda b, i: (b, i, 0)),
                   pl.BlockSpec((1, Q_BLOCK, D_I), lambda b, i: (b, i, 0)),
                   pl.BlockSpec((1, Q_BLOCK, R_KV), lambda b, i: (b, i, 0))],
        out_shape=[jax.ShapeDtypeStruct((bsz, nb, H_A * Q_BLOCK, R_KV), BF16),
                   jax.ShapeDtypeStruct((bsz, nb, H_I * Q_BLOCK, D_I), BF16),
                   jax.ShapeDtypeStruct((bsz, l, H_I), F32),
                   jax.ShapeDtypeStruct((bsz, l, D_I), BF16),
                   jax.ShapeDtypeStruct((bsz, l, R_KV), BF16)],
        compiler_params=_cparams("parallel", "parallel"),
        name="dsa_prep",
    )(pa, gq.reshape(1, R_Q), gkv.reshape(1, R_KV), wuq, wuk, wqi)


def _dsa_attn_kernel(qlat_ref, qidx_ref, widx_ref, kidx_ref, ckv_ref, wuv_ref, bias_ref, o_ref,
                     key_ref, m_ref, l_ref, a_ref, acc_ref, p_ref, *, topk):
    qi = pl.program_id(1)
    nch = (qi * Q_BLOCK) // KEY_CHUNK + 1
    row = lax.broadcasted_iota(jnp.int32, (Q_BLOCK, KEY_CHUNK), 0)
    lane = lax.broadcasted_iota(jnp.int32, (Q_BLOCK, KEY_CHUNK), 1)
    t_abs = qi * Q_BLOCK + row
    w = widx_ref[0]
    wcols = [w[:, h:h + 1] for h in range(H_I)]

    def score_body(c, carry):
        start = pl.multiple_of(c * KEY_CHUNK, KEY_CHUNK)
        kc = kidx_ref[0, pl.ds(start, KEY_CHUNK), :]
        acc = jnp.zeros((Q_BLOCK, KEY_CHUNK), F32)
        for h in range(H_I):
            s = lax.dot_general(qidx_ref[0, 0, h * Q_BLOCK:(h + 1) * Q_BLOCK, :], kc, NT_DIMS,
                                preferred_element_type=F32)
            acc = acc + wcols[h] * jnp.maximum(s, 0.0)
        acc = jnp.where(c * KEY_CHUNK + lane <= t_abs, acc, -jnp.inf)
        bits = lax.bitcast_convert_type(acc, jnp.int32)
        key_ref[c] = bits ^ ((bits >> 31) & 0x7FFFFFFF)
        return carry

    lax.fori_loop(0, nch, score_body, 0)

    def count_ge(cand):
        def body(c, cnt):
            return cnt + jnp.where(key_ref[c] >= cand, 1.0, 0.0)
        cnt = lax.fori_loop(0, nch, body, jnp.zeros((Q_BLOCK, KEY_CHUNK), F32))
        return jnp.sum(cnt, axis=1, keepdims=True)

    kf = float(topk)
    thr = jnp.where(count_ge(jnp.zeros((Q_BLOCK, 1), jnp.int32)) >= kf, 0, INT_MIN).astype(jnp.int32)

    def bit_body(i, thr):
        cand = thr + jnp.left_shift(jnp.int32(1), 30 - i)
        return jnp.where(count_ge(cand) >= kf, cand, thr)

    thr = lax.fori_loop(0, 31, bit_body, thr)

    m_ref[...] = jnp.full(m_ref.shape, -1e30, F32)
    l_ref[...] = jnp.zeros(l_ref.shape, F32)
    acc_ref[...] = jnp.zeros(acc_ref.shape, F32)
    odd = (qi % 2) == 1

    def attn_body(c, carry):
        start = pl.multiple_of(c * KEY_CHUNK, KEY_CHUNK)
        kv = ckv_ref[0, pl.ds(start, KEY_CHUNK), :]
        logits = lax.dot_general(qlat_ref[0, 0], kv, NT_DIMS, preferred_element_type=F32)
        sel = (key_ref[c] >= thr) & (c * KEY_CHUNK + lane <= t_abs)
        last = c == nch - 1
        idx = jnp.where(last, jnp.where(odd, 0, 1),
                        jnp.where((c == nch - 2) & jnp.logical_not(odd), 2, 3))
        for h in range(H_A):
            rs = slice(h * Q_BLOCK, (h + 1) * Q_BLOCK)
            lg = jnp.where(sel, logits[rs] + bias_ref[idx, h], -1e30)
            m_old = m_ref[rs]
            m_new = jnp.maximum(m_old, jnp.max(lg, axis=1, keepdims=True))
            p = jnp.where(sel, jnp.exp(lg - m_new), 0.0)
            alpha = jnp.exp(m_old - m_new)
            l_ref[rs] = alpha * l_ref[rs] + jnp.sum(p, axis=1, keepdims=True)
            m_ref[rs] = m_new
            a_ref[rs] = alpha
            p_ref[rs] = p.astype(BF16)
        acc_ref[...] = a_ref[...] * acc_ref[...] + jnp.dot(p_ref[...], kv, preferred_element_type=F32)
        return carry

    lax.fori_loop(0, nch, attn_body, 0)

    for h in range(H_A):
        rs = slice(h * Q_BLOCK, (h + 1) * Q_BLOCK)
        o_lat = (acc_ref[rs] / l_ref[rs]).astype(BF16)
        out = jnp.dot(o_lat, wuv_ref[h], preferred_element_type=F32)
        o_ref[0, :, h * DH_A:(h + 1) * DH_A] = out.astype(o_ref.dtype)


def _dsa_attn(qlat, qidx, widx, kidx, ckv, wuv, bias):
    bsz, l, _ = ckv.shape
    nb = l // Q_BLOCK
    topk = min(TOPK_MAX, l // 4)
    nchunks = (l + KEY_CHUNK - 1) // KEY_CHUNK
    hq = H_A * Q_BLOCK
    return pl.pallas_call(
        functools.partial(_dsa_attn_kernel, topk=topk),
        grid=(bsz, nb),
        in_specs=[pl.BlockSpec((1, 1, hq, R_KV), lambda b, i: (b, i, 0, 0)),
                  pl.BlockSpec((1, 1, H_I * Q_BLOCK, D_I), lambda b, i: (b, i, 0, 0)),
                  pl.BlockSpec((1, Q_BLOCK, H_I), lambda b, i: (b, i, 0)),
                  pl.BlockSpec((1, l, D_I), lambda b, i: (b, 0, 0)),
                  pl.BlockSpec((1, l, R_KV), lambda b, i: (b, 0, 0)),
                  pl.BlockSpec((H_A, R_KV, DH_A), lambda b, i: (0, 0, 0)),
                  pl.BlockSpec((4, H_A, Q_BLOCK, KEY_CHUNK), lambda b, i: (0, 0, 0, 0))],
        out_specs=pl.BlockSpec((1, Q_BLOCK, H_A * DH_A), lambda b, i: (b, i, 0)),
        out_shape=jax.ShapeDtypeStruct((bsz, l, H_A * DH_A), BF16),
        scratch_shapes=[pltpu.VMEM((nchunks, Q_BLOCK, KEY_CHUNK), jnp.int32),
                        pltpu.VMEM((hq, 1), F32), pltpu.VMEM((hq, 1), F32), pltpu.VMEM((hq, 1), F32),
                        pltpu.VMEM((hq, R_KV), F32), pltpu.VMEM((hq, KEY_CHUNK), BF16)],
        compiler_params=_cparams("parallel", "arbitrary"),
        name="dsa_attn",
    )(qlat, qidx, widx, kidx, ckv, wuv, bias)


def _seg_sum(x, bd):
    hi = x.astype(BF16)
    lo = (x - hi.astype(F32)).astype(BF16)
    outs = []
    for p in range(x.shape[1] // LANES):
        sl = slice(p * LANES, (p + 1) * LANES)
        outs.append(jnp.dot(hi[:, sl], bd, preferred_element_type=F32)
                    + jnp.dot(lo[:, sl], bd, preferred_element_type=F32))
    return outs[0] if len(outs) == 1 else jnp.concatenate(outs, axis=1)


def _rwkv_pre_kernel(pb_ref, mu_ref, w0_ref, a0_ref, kk_ref, ka_ref, rk_ref, wup_ref, aup_ref, gup_ref,
                     bd_ref, nkk_ref, d_ref, kka_ref, k2_ref, r_ref, v_ref, g_ref, bonus_ref, carry_ref):
    @pl.when(pl.program_id(1) == 0)
    def _():
        carry_ref[...] = jnp.zeros_like(carry_ref)

    x = pb_ref[0]
    tm = x.shape[0]
    rows = lax.broadcasted_iota(jnp.int32, x.shape, 0)
    prev = jnp.where(rows == 0, carry_ref[...], pltpu.roll(x, 1, 0))
    carry_ref[...] = x[tm - 1:tm]
    ps = x + mu_ref[...] * (prev - x)
    r = ps[:, :W_B]
    k = ps[:, W_B:2 * W_B]
    v = ps[:, 2 * W_B:3 * W_B]
    o = 3 * W_B
    wd = ps[:, o:o + LORA_W]
    ad = ps[:, o + LORA_W:o + LORA_W + LORA_A]
    gd = ps[:, o + LORA_W + LORA_A:]
    hp = lax.Precision.HIGHEST
    z = w0_ref[...] + jnp.dot(jnp.tanh(wd), wup_ref[...], precision=hp, preferred_element_type=F32)
    u = -z
    w = -(jnp.maximum(u, 0.0) + jnp.log(1.0 + jnp.exp(-jnp.abs(u)))) - 0.5
    d_ref[0] = jnp.exp(-jnp.exp(w))
    a = _sigmoid(a0_ref[...] + jnp.dot(ad, aup_ref[...], precision=hp, preferred_element_type=F32))
    g_ref[0] = jnp.dot(_sigmoid(gd).astype(BF16), gup_ref[...], preferred_element_type=F32)
    bd = bd_ref[...]
    kk = k * kk_ref[...]
    kk = kk * lax.rsqrt(jnp.maximum(_seg_sum(kk * kk, bd), 1e-24))
    k2 = k * (1.0 + (a - 1.0) * ka_ref[...])
    nkk_ref[0] = -kk
    kka_ref[0] = kk * a
    k2_ref[0] = k2
    r_ref[0] = r
    v_ref[0] = v
    bonus_ref[0] = _seg_sum(r * k2 * rk_ref[...], bd) * v


def _rwkv_pre(pb, mu, w0, a0, k_k, k_a, r_k, w_up, a_up, g_up, bd, tm=128):
    bsz, l, _ = pb.shape
    tm = min(tm, l)
    full = lambda shp: pl.BlockSpec(shp, lambda b, i: (0,) * len(shp))
    row = pl.BlockSpec((1, tm, W_B), lambda b, i: (b, i, 0))
    vec = full((1, W_B))
    return pl.pallas_call(
        _rwkv_pre_kernel,
        grid=(bsz, l // tm),
        in_specs=[pl.BlockSpec((1, tm, B_PAD), lambda b, i: (b, i, 0)), full((1, B_PAD)),
                  vec, vec, vec, vec, vec,
                  full((LORA_W, W_B)), full((LORA_A, W_B)), full((LORA_G_PAD, W_B)), full((LANES, LANES))],
        out_specs=[row] * 8,
        out_shape=[jax.ShapeDtypeStruct((bsz, l, W_B), F32)] * 8,
        scratch_shapes=[pltpu.VMEM((1, B_PAD), F32)],
        compiler_params=_cparams("parallel", "arbitrary"),
        name="rwkv_pre",
    )(pb, mu, w0.reshape(1, W_B), a0.reshape(1, W_B), k_k.reshape(1, W_B), k_a.reshape(1, W_B),
      r_k.reshape(1, W_B), w_up, a_up, g_up, bd)


def _rwkv_scan_kernel(nkk_ref, d_ref, kka_ref, k_ref, r_ref, vt_ref, bd_ref, yt_ref, s_ref, *, bsz):
    @pl.when(pl.program_id(0) == 0)
    def _():
        s_ref[...] = jnp.zeros_like(s_ref)

    yt_ref[...] = jnp.zeros_like(yt_ref)
    bd = bd_ref[...]
    lane_t = lax.broadcasted_iota(jnp.int32, (N_B, LANES), 1) % N_B

    def seg(x):
        hi = x.astype(BF16)
        lo = (x - hi.astype(F32)).astype(BF16)
        return (jnp.dot(hi, bd, preferred_element_type=F32) + jnp.dot(lo, bd, preferred_element_type=F32))

    def step(t, carry):
        onehot = lane_t == t
        for b in range(bsz):
            for p in range(H_B // 2):
                s = s_ref[b, p]
                sa = seg(s * nkk_ref[b, t, p:p + 1, :])
                vcol = seg(jnp.where(onehot, vt_ref[b, 0, p], 0.0))
                sn = (s * d_ref[b, t, p:p + 1, :] + sa * kka_ref[b, t, p:p + 1, :]
                      + vcol * k_ref[b, t, p:p + 1, :])
                s_ref[b, p] = sn
                y = seg(sn * r_ref[b, t, p:p + 1, :])
                yt_ref[b, 0, p] = jnp.where(onehot, y, yt_ref[b, 0, p])
        return carry

    lax.fori_loop(0, N_B, step, 0)


def _rwkv_scan(nkk, d, kka, k2, r, vt, bd):
    bsz, l, _ = nkk.shape
    nt = l // N_B
    npair = H_B // 2
    nkk, d, kka, k2, r = [a.reshape(bsz, l, npair, LANES) for a in (nkk, d, kka, k2, r)]
    row = pl.BlockSpec((bsz, N_B, npair, LANES), lambda i: (0, i, 0, 0))
    tile = pl.BlockSpec((bsz, 1, npair, N_B, LANES), lambda i: (0, i, 0, 0, 0))
    return pl.pallas_call(
        functools.partial(_rwkv_scan_kernel, bsz=bsz),
        grid=(nt,),
        in_specs=[row] * 5 + [tile, pl.BlockSpec((LANES, LANES), lambda i: (0, 0))],
        out_specs=tile,
        out_shape=jax.ShapeDtypeStruct((bsz, nt, npair, N_B, LANES), F32),
        scratch_shapes=[pltpu.VMEM((bsz, npair, N_B, LANES), F32)],
        compiler_params=_cparams("arbitrary"),
        name="rwkv_scan",
    )(nkk, d, kka, k2, r, vt, bd)


def _rwkv_post_kernel(y_ref, bonus_ref, g_ref, lw_ref, lb_ref, bd_ref, o_ref):
    bd = bd_ref[...]
    y = y_ref[0]
    mean = _seg_sum(y, bd) * (1.0 / N_B)
    c = y - mean
    var = _seg_sum(c * c, bd) * (1.0 / N_B)
    yn = c * lax.rsqrt(var + GN_EPS) * lw_ref[...] + lb_ref[...]
    o_ref[0] = ((yn + bonus_ref[0]) * g_ref[0]).astype(o_ref.dtype)


def _rwkv_post(y, bonus, g, lnx_w, lnx_b, bd, tm=256):
    bsz, l, _ = y.shape
    tm = min(tm, l)
    row = pl.BlockSpec((1, tm, W_B), lambda b, i: (b, i, 0))
    vec = pl.BlockSpec((1, W_B), lambda b, i: (0, 0))
    return pl.pallas_call(
        _rwkv_post_kernel,
        grid=(bsz, l // tm),
        in_specs=[row, row, row, vec, vec, pl.BlockSpec((LANES, LANES), lambda b, i: (0, 0))],
        out_specs=row,
        out_shape=jax.ShapeDtypeStruct((bsz, l, W_B), BF16),
        compiler_params=_cparams("parallel", "parallel"),
        name="rwkv_post",
    )(y, bonus, g, lnx_w.reshape(1, W_B), lnx_b.reshape(1, W_B), bd)


def _rwkv_mixer(pb, mu, w0, w_up, a0, a_up, g_up, k_k, k_a, r_k, lnx_w, lnx_b):
    bsz, l, _ = pb.shape
    nt, npair = l // N_B, H_B // 2
    eye2 = jnp.kron(jnp.eye(2, dtype=F32), jnp.ones((N_B, N_B), F32)).astype(BF16)
    nkk, d, kka, k2, r, v, g, bonus = _rwkv_pre(pb, mu, w0, a0, k_k, k_a, r_k, w_up, a_up, g_up, eye2)
    vt = v.reshape(bsz, nt, N_B, npair, 2, N_B).transpose(0, 1, 3, 5, 4, 2).reshape(bsz, nt, npair, N_B, LANES)
    yt = _rwkv_scan(nkk, d, kka, k2, r, vt, eye2)
    y = yt.reshape(bsz, nt, npair, N_B, 2, N_B).transpose(0, 1, 5, 2, 4, 3).reshape(bsz, l, W_B)
    return _rwkv_post(y, bonus, g, lnx_w, lnx_b, eye2)


def _hgrn_kernel(q_ref, f_ref, i_ref, g_ref, lb_ref, on_ref, tril_ref, o_ref, st_ref, *, layer, hb, tt):
    @pl.when(pl.program_id(2) == 0)
    def _():
        st_ref[...] = jnp.zeros_like(st_ref)

    x = lb_ref[...]
    e = jnp.exp(x - jnp.max(x, axis=0, keepdims=True))
    sm = e / jnp.sum(e, axis=0, keepdims=True)
    cs = sm[0:1]
    for i in range(1, layer + 1):
        cs = cs + sm[i:i + 1]
    lb_all = cs - sm[0:1]
    rowi = lax.broadcasted_iota(jnp.int32, (SUB, 1), 0)
    for hh in range(hb):
        sl = slice(hh * LANES, (hh + 1) * LANES)
        lb = lb_all[:, sl]
        qraw = q_ref[0][:, sl]
        ff = f_ref[0][:, sl]
        v = i_ref[0][:, sl]
        gg = g_ref[0][:, sl]
        q = qraw * _sigmoid(qraw)
        logsig = jnp.minimum(ff, 0.0) - jnp.log(1.0 + jnp.exp(-jnp.abs(ff)))
        la = jnp.log(lb)
        lbv = jnp.log(1.0 - lb) + logsig
        mx = jnp.maximum(la, lbv)
        logf = mx + jnp.log(jnp.exp(la - mx) + jnp.exp(lbv - mx))
        k = (1.0 - lb) * _sigmoid(-ff)
        bcum = jnp.dot(tril_ref[...], logf, precision=lax.Precision.HIGHEST, preferred_element_type=F32)
        st = st_ref[hh]
        outs = []
        for n in range(tt // SUB):
            rs = slice(n * SUB, (n + 1) * SUB)
            bq, qn, kn, vn = bcum[rs], q[rs], k[rs], v[rs]
            bend = bq[SUB - 1:SUB]
            qd = (qn * jnp.exp(bq)).astype(BF16)
            o = lax.dot_general(qd, st.astype(BF16), NT_DIMS, preferred_element_type=F32)
            for s in range(SUB):
                wgt = kn[s:s + 1] * jnp.exp(jnp.minimum(bq - bq[s:s + 1], 0.0))
                a_s = jnp.sum(qn * wgt, axis=1, keepdims=True)
                o = o + jnp.where(rowi >= s, a_s, 0.0) * vn[s:s + 1]
            kd = (kn * jnp.exp(bend - bq)).astype(BF16)
            st = st * jnp.exp(bend) + lax.dot_general(vn.astype(BF16), kd, TN_DIMS,
                                                      preferred_element_type=F32)
            outs.append(o)
        st_ref[hh] = st
        o = jnp.concatenate(outs, axis=0)
        o = _rms(o) * on_ref[:, sl] * (gg * _sigmoid(gg))
        o_ref[0, :, sl] = o.astype(o_ref.dtype)


def _hgrn_mixer(pc, hgrn_lb, onorm, layer, hb=4, tt=128):
    bsz, l, _ = pc.shape
    depth = hgrn_lb.shape[0]
    tt = min(tt, l)
    ng = H_C // hb
    wblk = hb * LANES
    tril = jnp.kron(jnp.eye(tt // SUB, dtype=F32), jnp.tril(jnp.ones((SUB, SUB), F32)))
    col = lambda j: pl.BlockSpec((1, tt, wblk), lambda b, h, i, j=j: (b, i, j * ng + h))
    return pl.pallas_call(
        functools.partial(_hgrn_kernel, layer=layer, hb=hb, tt=tt),
        grid=(bsz, ng, l // tt),
        in_specs=[col(0), col(1), col(2), col(3),
                  pl.BlockSpec((depth, wblk), lambda b, h, i: (0, h)),
                  pl.BlockSpec((1, wblk), lambda b, h, i: (0, h)),
                  pl.BlockSpec((tt, tt), lambda b, h, i: (0, 0))],
        out_specs=pl.BlockSpec((1, tt, wblk), lambda b, h, i: (b, i, h)),
        out_shape=jax.ShapeDtypeStruct((bsz, l, W_C), BF16),
        scratch_shapes=[pltpu.VMEM((hb, DV_C, DK_C), F32)],
        compiler_params=_cparams("parallel", "parallel", "arbitrary"),
        name="hgrn2",
    )(pc, pc, pc, pc, hgrn_lb, onorm.reshape(1, W_C), tril)


def _pack_w_in(w):
    d = w.shape[0]
    a_cols = R_Q + R_KV + D_I + H_I
    b_cols = 3 * W_B + LORA_W + LORA_A + LORA_G
    wa, wb, wc = w[:, :a_cols], w[:, a_cols:a_cols + b_cols], w[:, a_cols + b_cols:]
    z = lambda n: jnp.zeros((d, n), w.dtype)
    o = R_Q + R_KV
    wa = jnp.concatenate([wa[:, :o + D_I], z(LANES - D_I), wa[:, o + D_I:], z(LANES - H_I)], axis=1)
    wb = jnp.concatenate([wb, z(LORA_G_PAD - LORA_G)], axis=1)
    return wa.astype(BF16), wb.astype(BF16), wc.astype(BF16)


def kernel(x, c, rel_bias, hgrn_lb, ada_w, ada_b, norm_g, w_in, w_out, mla_q_norm, mla_kv_norm, w_uq, w_uk, w_uv, w_qidx, rwkv_mu, rwkv_w0, rwkv_w_up, rwkv_a0, rwkv_a_up, rwkv_g_up, rwkv_k_k, rwkv_k_a, rwkv_r_k, rwkv_lnx_w, rwkv_lnx_b, hgrn_onorm, w_ff1, w_ff2):
    bsz, l, d = x.shape
    depth = ada_w.shape[0]
    m = bsz * l
    mod = _ada_mod(c, ada_w, ada_b)
    mods = [[mod[i, :, None, j * d:(j + 1) * d] for j in range(6)] for i in range(depth)]
    bias = _bias_tiles(rel_bias)
    h = _norm_mod(x, norm_g[0, 0], mods[0][1], mods[0][0])
    for i in range(depth):
        sh_m, sc_m, g_m, sh_f, sc_f, g_f = mods[i]
        wa, wb, wc = _pack_w_in(w_in[i])
        h2d = h.reshape(m, d)
        pa = _matmul(h2d, wa, F32, 1024, A_PAD, 2048, name="in_proj_a").reshape(bsz, l, -1)
        pb = _matmul(h2d, wb, F32, 1024, 768, 2048, name="in_proj_b").reshape(bsz, l, -1)
        pc = _matmul(h2d, wc, F32, 1024, 1024, 2048, name="in_proj_c").reshape(bsz, l, -1)
        wuq = w_uq[i].reshape(R_Q, H_A * DH_A).astype(BF16)
        wqi = jnp.transpose(w_qidx[i], (1, 0, 2)).astype(BF16)
        qlat, qidx, widx, kidx, ckv = _dsa_prep(pa, mla_q_norm[i], mla_kv_norm[i], wuq,
                                                w_uk[i].astype(BF16), wqi)
        y_a = _dsa_attn(qlat, qidx, widx, kidx, ckv, w_uv[i].astype(BF16), bias)
        mu = jnp.concatenate([rwkv_mu[i], jnp.zeros((LORA_G_PAD - LORA_G,), F32)]).reshape(1, B_PAD)
        gup = jnp.concatenate([rwkv_g_up[i], jnp.zeros((LORA_G_PAD - LORA_G, W_B), F32)]).astype(BF16)
        y_b = _rwkv_mixer(pb, mu, rwkv_w0[i], rwkv_w_up[i], rwkv_a0[i], rwkv_a_up[i], gup,
                          rwkv_k_k[i], rwkv_k_a[i], rwkv_r_k[i], rwkv_lnx_w[i], rwkv_lnx_b[i])
        y_c = _hgrn_mixer(pc, hgrn_lb, hgrn_onorm[i], i)
        ycat = jnp.concatenate([y_a, y_b, y_c], axis=-1).reshape(m, -1)
        y = _matmul(ycat, w_out[i].astype(BF16), F32, 1024, 1024, 2048, name="out_proj").reshape(bsz, l, d)
        x, h = _resid_norm(x, y, g_m, norm_g[i, 1], (norm_g[i, 2], sc_f, sh_f))
        u = _matmul(h.reshape(m, d), w_ff1[i].astype(BF16), BF16, 1024, 1024, 2048, relu2=True, name="ffn_up")
        y = _matmul(u, w_ff2[i].astype(BF16), F32, 1024, 1024, 2048, name="ffn_down").reshape(bsz, l, d)
        nxt = None if i == depth - 1 else (norm_g[i + 1, 0], mods[i + 1][1], mods[i + 1][0])
        x, h = _resid_norm(x, y, g_f, norm_g[i, 3], nxt)
    return x
```

```python
import functools
import math

import jax
import jax.numpy as jnp
from jax import lax
from jax.experimental import pallas as pl
from jax.experimental.pallas import tpu as pltpu

H_A, DH_A, R_Q, R_KV, H_I, D_I = 8, 128, 768, 256, 16, 64
TOPK_MAX, Q_BLOCK, N_BUCKETS, MAX_DIST = 256, 128, 32, 128
H_B, N_B, LORA_W, LORA_A, LORA_G = 24, 64, 128, 128, 480
W_B = H_B * N_B
H_C, DK_C, DV_C = 12, 128, 128
W_C = H_C * DK_C
EPS = 1e-6
GN_EPS = 64e-5

LANES = 128
KEY_CHUNK = 256
SUB = 16
RW_CHUNK = 64
RW_PASSES = 3
LORA_G_PAD = 512
A_PAD = 1280
B_PAD = 3 * W_B + LORA_W + LORA_A + LORA_G_PAD
VMEM_LIMIT = 48 * 1024 * 1024

F32 = jnp.float32
BF16 = jnp.bfloat16
NT_DIMS = (((1,), (1,)), ((), ()))
TN_DIMS = (((0,), (0,)), ((), ()))
INT_MIN = -2 ** 31


def _cparams(*sem):
    return pltpu.CompilerParams(dimension_semantics=sem, vmem_limit_bytes=VMEM_LIMIT)


def _sigmoid(x):
    return 1.0 / (1.0 + jnp.exp(-x))


def _rms(x, eps=EPS):
    return x * lax.rsqrt(jnp.mean(x * x, axis=-1, keepdims=True) + eps)


def _mm_kernel(a_ref, b_ref, o_ref, acc_ref, *, nk, relu2):
    k = pl.program_id(2)

    @pl.when(k == 0)
    def _():
        acc_ref[...] = jnp.zeros_like(acc_ref)

    acc_ref[...] += jnp.dot(a_ref[...], b_ref[...], preferred_element_type=F32)

    @pl.when(k == nk - 1)
    def _():
        r = acc_ref[...]
        if relu2:
            r = jnp.square(jnp.maximum(r, 0.0))
        o_ref[...] = r.astype(o_ref.dtype)


def _matmul(a, b, out_dtype, tm, tn, tk, relu2=False, name="matmul"):
    m, kdim = a.shape
    _, n = b.shape
    tm, tn, tk = min(tm, m), min(tn, n), min(tk, kdim)
    assert m % tm == 0 and n % tn == 0 and kdim % tk == 0, (a.shape, b.shape, tm, tn, tk)
    nk = kdim // tk
    return pl.pallas_call(
        functools.partial(_mm_kernel, nk=nk, relu2=relu2),
        grid=(m // tm, n // tn, nk),
        in_specs=[pl.BlockSpec((tm, tk), lambda i, j, k: (i, k)),
                  pl.BlockSpec((tk, tn), lambda i, j, k: (k, j))],
        out_specs=pl.BlockSpec((tm, tn), lambda i, j, k: (i, j)),
        out_shape=jax.ShapeDtypeStruct((m, n), out_dtype),
        scratch_shapes=[pltpu.VMEM((tm, tn), F32)],
        compiler_params=_cparams("parallel", "parallel", "arbitrary"),
        name=name,
    )(a, b)


def _ada_kernel(c_ref, w_ref, b_ref, o_ref):
    k = pl.program_id(2)

    @pl.when(k == 0)
    def _():
        o_ref[0] = jnp.broadcast_to(b_ref[0], o_ref.shape[1:])

    c = c_ref[...]
    ca = (c * _sigmoid(c)).astype(BF16)
    o_ref[0] += jnp.dot(ca, w_ref[0].astype(BF16), preferred_element_type=F32)


def _ada_mod(c, ada_w, ada_b):
    nl, d, n = ada_w.shape
    bsz = c.shape[0]
    rows = 8
    cp = jnp.zeros((rows, d), F32).at[:bsz].set(c)
    tn, tk = min(2048, n), min(1024, d)
    out = pl.pallas_call(
        _ada_kernel,
        grid=(nl, n // tn, d // tk),
        in_specs=[pl.BlockSpec((rows, tk), lambda l, j, k: (0, k)),
                  pl.BlockSpec((1, tk, tn), lambda l, j, k: (l, k, j)),
                  pl.BlockSpec((1, 1, tn), lambda l, j, k: (l, 0, j))],
        out_specs=pl.BlockSpec((1, rows, tn), lambda l, j, k: (l, 0, j)),
        out_shape=jax.ShapeDtypeStruct((nl, rows, n), F32),
        compiler_params=_cparams("parallel", "parallel", "arbitrary"),
        name="ada_mod",
    )(cp, ada_w, ada_b.reshape(nl, 1, n))
    return out[:, :bsz]


def _norm_mod_kernel(x_ref, g_ref, sc_ref, sh_ref, h_ref):
    y = _rms(x_ref[0]) * g_ref[...]
    h_ref[0] = (y * (1.0 + sc_ref[0]) + sh_ref[0]).astype(h_ref.dtype)


def _norm_mod(x, g, sc, sh, tm=256):
    bsz, l, d = x.shape
    tm = min(tm, l)
    row = pl.BlockSpec((1, tm, d), lambda b, i: (b, i, 0))
    vec = pl.BlockSpec((1, d), lambda b, i: (0, 0))
    bvec = pl.BlockSpec((1, 1, d), lambda b, i: (b, 0, 0))
    return pl.pallas_call(
        _norm_mod_kernel,
        grid=(bsz, l // tm),
        in_specs=[row, vec, bvec, bvec],
        out_specs=row,
        out_shape=jax.ShapeDtypeStruct((bsz, l, d), BF16),
        compiler_params=_cparams("parallel", "parallel"),
        name="norm_mod",
    )(x, g.reshape(1, d), sc, sh)


def _resid_kernel(x_ref, y_ref, gate_ref, g1_ref, g2_ref, sc_ref, sh_ref, xo_ref, h_ref):
    xn = x_ref[0] + gate_ref[0] * (_rms(y_ref[0]) * g1_ref[...])
    xo_ref[0] = xn
    hn = _rms(xn) * g2_ref[...]
    h_ref[0] = (hn * (1.0 + sc_ref[0]) + sh_ref[0]).astype(h_ref.dtype)


def _resid_last_kernel(x_ref, y_ref, gate_ref, g1_ref, xo_ref):
    xo_ref[0] = x_ref[0] + gate_ref[0] * (_rms(y_ref[0]) * g1_ref[...])


def _resid_norm(x, y, gate, g1, nxt=None, tm=128):
    bsz, l, d = x.shape
    tm = min(tm, l)
    row = pl.BlockSpec((1, tm, d), lambda b, i: (b, i, 0))
    vec = pl.BlockSpec((1, d), lambda b, i: (0, 0))
    bvec = pl.BlockSpec((1, 1, d), lambda b, i: (b, 0, 0))
    if nxt is None:
        return pl.pallas_call(
            _resid_last_kernel,
            grid=(bsz, l // tm),
            in_specs=[row, row, bvec, vec],
            out_specs=row,
            out_shape=jax.ShapeDtypeStruct((bsz, l, d), F32),
            compiler_params=_cparams("parallel", "parallel"),
            name="resid_last",
        )(x, y, gate, g1.reshape(1, d)), None
    g2, sc, sh = nxt
    return pl.pallas_call(
        _resid_kernel,
        grid=(bsz, l // tm),
        in_specs=[row, row, bvec, vec, vec, bvec, bvec],
        out_specs=[row, row],
        out_shape=[jax.ShapeDtypeStruct((bsz, l, d), F32), jax.ShapeDtypeStruct((bsz, l, d), BF16)],
        compiler_params=_cparams("parallel", "parallel"),
        name="resid_norm",
    )(x, y, gate, g1.reshape(1, d), g2.reshape(1, d), sc, sh)


def _bias_kernel(rb_ref, o_ref):
    i = lax.broadcasted_iota(jnp.int32, (Q_BLOCK, KEY_CHUNK), 0)
    j = lax.broadcasted_iota(jnp.int32, (Q_BLOCK, KEY_CHUNK), 1)
    max_exact = N_BUCKETS // 2
    for v, off in enumerate((Q_BLOCK, 0, 2 * Q_BLOCK, None)):
        if off is None:
            dist = jnp.full((Q_BLOCK, KEY_CHUNK), 2 * MAX_DIST, jnp.int32)
        else:
            dist = jnp.maximum(i + off - j, 0)
        nf = jnp.maximum(dist, 1).astype(F32)
        large = max_exact + (jnp.log(nf / max_exact) / math.log(MAX_DIST / max_exact)
                             * (N_BUCKETS - max_exact)).astype(jnp.int32)
        large = jnp.minimum(large, N_BUCKETS - 1)
        bucket = jnp.where(dist < max_exact, dist, large)
        for h in range(H_A):
            def body(b, acc, h=h, bucket=bucket):
                return jnp.where(bucket == b, rb_ref[b * H_A + h], acc)
            o_ref[v, h] = lax.fori_loop(0, N_BUCKETS, body, jnp.zeros((Q_BLOCK, KEY_CHUNK), F32))


def _bias_tiles(rel_bias):
    return pl.pallas_call(
        _bias_kernel,
        in_specs=[pl.BlockSpec(memory_space=pltpu.SMEM)],
        out_specs=pl.BlockSpec(memory_space=pltpu.VMEM),
        out_shape=jax.ShapeDtypeStruct((4, H_A, Q_BLOCK, KEY_CHUNK), F32),
        name="dsa_bias_tiles",
    )(rel_bias.reshape(-1))


def _dsa_prep_kernel(pa_ref, gq_ref, gkv_ref, wuq_ref, wuk_ref, wqi_ref,
                     qlat_ref, qidx_ref, widx_ref, kidx_ref, ckv_ref):
    pa = pa_ref[0]
    cq = pa[:, :R_Q]
    ckv = pa[:, R_Q:R_Q + R_KV]
    kid = pa[:, R_Q + R_KV:R_Q + R_KV + D_I]
    wid = pa[:, R_Q + R_KV + LANES:R_Q + R_KV + LANES + H_I]
    cqn = (_rms(cq) * gq_ref[...]).astype(BF16)
    ckv_ref[0] = (_rms(ckv) * gkv_ref[...]).astype(BF16)
    kidx_ref[0] = kid.astype(BF16)
    widx_ref[0] = wid * (H_I ** -0.5 * D_I ** -0.5)
    q = jnp.dot(cqn, wuq_ref[...], preferred_element_type=F32)
    for h in range(H_A):
        qh = q[:, h * DH_A:(h + 1) * DH_A].astype(BF16)
        ql = lax.dot_general(qh, wuk_ref[h], NT_DIMS, preferred_element_type=F32)
        qlat_ref[0, 0, h * Q_BLOCK:(h + 1) * Q_BLOCK, :] = (ql * DH_A ** -0.5).astype(BF16)
    for h in range(H_I):
        qi = jnp.dot(cqn, wqi_ref[h], preferred_element_type=F32)
        qidx_ref[0, 0, h * Q_BLOCK:(h + 1) * Q_BLOCK, :] = qi.astype(BF16)


def _dsa_prep(pa, gq, gkv, wuq, wuk, wqi):
    bsz, l, _ = pa.shape
    nb = l // Q_BLOCK
    full = lambda shp: pl.BlockSpec(shp, lambda b, i: (0,) * len(shp))
    return pl.pallas_call(
        _dsa_prep_kernel,
        grid=(bsz, nb),
        in_specs=[pl.BlockSpec((1, Q_BLOCK, A_PAD), lambda b, i: (b, i, 0)),
                  full((1, R_Q)), full((1, R_KV)), full((R_Q, H_A * DH_A)),
                  full((H_A, R_KV, DH_A)), full((H_I, R_Q, D_I))],
        out_specs=[pl.BlockSpec((1, 1, H_A * Q_BLOCK, R_KV), lambda b, i: (b, i, 0, 0)),
                   pl.BlockSpec((1, 1, H_I * Q_BLOCK, D_I), lambda b, i: (b, i, 0, 0)),
                   pl.BlockSpec((1, Q_BLOCK, H_I), lambda b, i: (b, i, 0)),
                   pl.BlockSpec((1, Q_BLOCK, D_I), lambda b, i: (b, i, 0)),
                   pl.BlockSpec((1, Q_BLOCK, R_KV), lambda b, i: (b, i, 0))],
        out_shape=[jax.ShapeDtypeStruct((bsz, nb, H_A * Q_BLOCK, R_KV), BF16),
                   jax.ShapeDtypeStruct((bsz, nb, H_I * Q_BLOCK, D_I), BF16),
                   jax.ShapeDtypeStruct((bsz, l, H_I), F32),
                   jax.ShapeDtypeStruct((bsz, l, D_I), BF16),
                   jax.ShapeDtypeStruct((bsz, l, R_KV), BF16)],
        compiler_params=_cparams("parallel", "parallel"),
        name="dsa_prep",
    )(pa, gq.reshape(1, R_Q), gkv.reshape(1, R_KV), wuq, wuk, wqi)


def _dsa_attn_kernel(qlat_ref, qidx_ref, widx_ref, kidx_ref, ckv_ref, wuv_ref, bias_ref, o_ref,
                     key_ref, m_ref, l_ref, mx_ref, ls_ref, acc_ref, p_ref, *, topk):
    qi = pl.program_id(1)
    nch = (qi * Q_BLOCK) // KEY_CHUNK + 1
    row = lax.broadcasted_iota(jnp.int32, (Q_BLOCK, KEY_CHUNK), 0)
    lane = lax.broadcasted_iota(jnp.int32, (Q_BLOCK, KEY_CHUNK), 1)
    t_abs = qi * Q_BLOCK + row
    w = widx_ref[0]
    wcols = [w[:, h:h + 1] for h in range(H_I)]

    def score_body(c, carry):
        start = pl.multiple_of(c * KEY_CHUNK, KEY_CHUNK)
        kc = kidx_ref[0, pl.ds(start, KEY_CHUNK), :]
        acc = jnp.zeros((Q_BLOCK, KEY_CHUNK), F32)
        for h in range(H_I):
            s = lax.dot_general(qidx_ref[0, 0, h * Q_BLOCK:(h + 1) * Q_BLOCK, :], kc, NT_DIMS,
                                preferred_element_type=F32)
            acc = acc + wcols[h] * jnp.maximum(s, 0.0)
        acc = jnp.where(c * KEY_CHUNK + lane <= t_abs, acc, -jnp.inf)
        bits = lax.bitcast_convert_type(acc, jnp.int32)
        key_ref[c] = bits ^ ((bits >> 31) & 0x7FFFFFFF)
        return carry

    lax.fori_loop(0, nch, score_body, 0)

    def count_ge(cand):
        def body(c, cnt):
            return cnt + jnp.where(key_ref[c] >= cand, 1.0, 0.0)
        cnt = lax.fori_loop(0, nch, body, jnp.zeros((Q_BLOCK, KEY_CHUNK), F32))
        return jnp.sum(cnt, axis=1, keepdims=True)

    kf = float(topk)
    cnt0 = count_ge(jnp.zeros((Q_BLOCK, 1), jnp.int32))
    thr = jnp.where(cnt0 >= kf, 0, INT_MIN).astype(jnp.int32)
    cnt_thr = jnp.where(cnt0 >= kf, cnt0, (nch * KEY_CHUNK).astype(F32))
    few = (qi * Q_BLOCK + lax.broadcasted_iota(jnp.int32, (Q_BLOCK, 1), 0)) < topk

    def unsettled(cnt_thr):
        return jnp.max(jnp.where(few | (cnt_thr == kf), 0.0, 1.0)) > 0.5

    def bit_cond(st):
        return (st[0] < 31) & st[3]

    def bit_body(st):
        i, thr, cnt_thr, _ = st
        cand = thr + jnp.left_shift(jnp.int32(1), 30 - i)
        cnt = count_ge(cand)
        take = cnt >= kf
        cnt_thr = jnp.where(take, cnt, cnt_thr)
        return i + 1, jnp.where(take, cand, thr), cnt_thr, unsettled(cnt_thr)

    thr = lax.while_loop(bit_cond, bit_body, (jnp.int32(0), thr, cnt_thr, unsettled(cnt_thr)))[1]

    odd = (qi % 2) == 1

    def chunk_logits(c):
        start = pl.multiple_of(c * KEY_CHUNK, KEY_CHUNK)
        kv = ckv_ref[0, pl.ds(start, KEY_CHUNK), :]
        logits = lax.dot_general(qlat_ref[0, 0], kv, NT_DIMS, preferred_element_type=F32)
        sel = (key_ref[c] >= thr) & (c * KEY_CHUNK + lane <= t_abs)
        idx = jnp.where(c == nch - 1, jnp.where(odd, 0, 1),
                        jnp.where((c == nch - 2) & jnp.logical_not(odd), 2, 3))
        return kv, logits, sel, idx

    mx_ref[...] = jnp.full(mx_ref.shape, -1e30, F32)

    def max_body(c, carry):
        _, logits, sel, idx = chunk_logits(c)
        for h in range(H_A):
            rs = slice(h * Q_BLOCK, (h + 1) * Q_BLOCK)
            mx_ref[rs] = jnp.maximum(mx_ref[rs], jnp.where(sel, logits[rs] + bias_ref[idx, h], -1e30))
        return carry

    lax.fori_loop(0, nch, max_body, 0)
    m_ref[...] = jnp.max(mx_ref[...], axis=1, keepdims=True)
    ls_ref[...] = jnp.zeros(ls_ref.shape, F32)
    acc_ref[...] = jnp.zeros(acc_ref.shape, F32)

    def attn_body(c, carry):
        kv, logits, sel, idx = chunk_logits(c)
        for h in range(H_A):
            rs = slice(h * Q_BLOCK, (h + 1) * Q_BLOCK)
            p = jnp.where(sel, jnp.exp(logits[rs] + bias_ref[idx, h] - m_ref[rs]), 0.0)
            ls_ref[rs] += p
            p_ref[rs] = p.astype(BF16)
        acc_ref[...] += jnp.dot(p_ref[...], kv, preferred_element_type=F32)
        return carry

    lax.fori_loop(0, nch, attn_body, 0)
    l_ref[...] = jnp.sum(ls_ref[...], axis=1, keepdims=True)

    for h in range(H_A):
        rs = slice(h * Q_BLOCK, (h + 1) * Q_BLOCK)
        o_lat = (acc_ref[rs] / l_ref[rs]).astype(BF16)
        out = jnp.dot(o_lat, wuv_ref[h], preferred_element_type=F32)
        o_ref[0, :, h * DH_A:(h + 1) * DH_A] = out.astype(o_ref.dtype)


def _dsa_attn(qlat, qidx, widx, kidx, ckv, wuv, bias):
    bsz, l, _ = ckv.shape
    nb = l // Q_BLOCK
    topk = min(TOPK_MAX, l // 4)
    nchunks = (l + KEY_CHUNK - 1) // KEY_CHUNK
    hq = H_A * Q_BLOCK
    return pl.pallas_call(
        functools.partial(_dsa_attn_kernel, topk=topk),
        grid=(bsz, nb),
        in_specs=[pl.BlockSpec((1, 1, hq, R_KV), lambda b, i: (b, i, 0, 0)),
                  pl.BlockSpec((1, 1, H_I * Q_BLOCK, D_I), lambda b, i: (b, i, 0, 0)),
                  pl.BlockSpec((1, Q_BLOCK, H_I), lambda b, i: (b, i, 0)),
                  pl.BlockSpec((1, l, D_I), lambda b, i: (b, 0, 0)),
                  pl.BlockSpec((1, l, R_KV), lambda b, i: (b, 0, 0)),
                  pl.BlockSpec((H_A, R_KV, DH_A), lambda b, i: (0, 0, 0)),
                  pl.BlockSpec((4, H_A, Q_BLOCK, KEY_CHUNK), lambda b, i: (0, 0, 0, 0))],
        out_specs=pl.BlockSpec((1, Q_BLOCK, H_A * DH_A), lambda b, i: (b, i, 0)),
        out_shape=jax.ShapeDtypeStruct((bsz, l, H_A * DH_A), BF16),
        scratch_shapes=[pltpu.VMEM((nchunks, Q_BLOCK, KEY_CHUNK), jnp.int32),
                        pltpu.VMEM((hq, 1), F32), pltpu.VMEM((hq, 1), F32),
                        pltpu.VMEM((hq, KEY_CHUNK), F32), pltpu.VMEM((hq, KEY_CHUNK), F32),
                        pltpu.VMEM((hq, R_KV), F32), pltpu.VMEM((hq, KEY_CHUNK), BF16)],
        compiler_params=_cparams("parallel", "arbitrary"),
        name="dsa_attn",
    )(qlat, qidx, widx, kidx, ckv, wuv, bias)


def _seg_sum(x, bd):
    hi = x.astype(BF16)
    lo = (x - hi.astype(F32)).astype(BF16)
    outs = []
    for p in range(x.shape[1] // LANES):
        sl = slice(p * LANES, (p + 1) * LANES)
        outs.append(jnp.dot(hi[:, sl], bd, preferred_element_type=F32)
                    + jnp.dot(lo[:, sl], bd, preferred_element_type=F32))
    return outs[0] if len(outs) == 1 else jnp.concatenate(outs, axis=1)


def _rwkv_pre_kernel(pb_ref, mu_ref, w0_ref, a0_ref, kk_ref, ka_ref, rk_ref, wup_ref, aup_ref, gup_ref,
                     bd_ref, tril_ref, ab_ref, bb_ref, kb_ref, rb_ref, v_ref, p_ref, g_ref, bonus_ref,
                     carry_ref):
    @pl.when(pl.program_id(1) == 0)
    def _():
        carry_ref[...] = jnp.zeros_like(carry_ref)

    x = pb_ref[0]
    tm = x.shape[0]
    rows = lax.broadcasted_iota(jnp.int32, x.shape, 0)
    prev = jnp.where(rows == 0, carry_ref[...], pltpu.roll(x, 1, 0))
    carry_ref[...] = x[tm - 1:tm]
    ps = x + mu_ref[...] * (prev - x)
    r = ps[:, :W_B]
    k = ps[:, W_B:2 * W_B]
    v = ps[:, 2 * W_B:3 * W_B]
    o = 3 * W_B
    wd = ps[:, o:o + LORA_W]
    ad = ps[:, o + LORA_W:o + LORA_W + LORA_A]
    gd = ps[:, o + LORA_W + LORA_A:]
    hp = lax.Precision.HIGHEST
    z = w0_ref[...] + jnp.dot(jnp.tanh(wd), wup_ref[...], precision=hp, preferred_element_type=F32)
    u = -z
    w = -(jnp.maximum(u, 0.0) + jnp.log(1.0 + jnp.exp(-jnp.abs(u)))) - 0.5
    logd = -jnp.exp(w)
    logp = jnp.dot(tril_ref[...], logd, precision=hp, preferred_element_type=F32)
    pinv = jnp.exp(-logp)
    a = _sigmoid(a0_ref[...] + jnp.dot(ad, aup_ref[...], precision=hp, preferred_element_type=F32))
    g_ref[0] = jnp.dot(_sigmoid(gd).astype(BF16), gup_ref[...], preferred_element_type=F32)
    bd = bd_ref[...]
    kk = k * kk_ref[...]
    kk = kk * lax.rsqrt(jnp.maximum(_seg_sum(kk * kk, bd), 1e-24))
    k2 = k * (1.0 + (a - 1.0) * ka_ref[...])
    p = jnp.exp(logp)
    ab_ref[0] = -kk * jnp.exp(logp - logd)
    bb_ref[0] = kk * a * pinv
    kb_ref[0] = k2 * pinv
    rb_ref[0] = r * p
    v_ref[0] = v
    p_ref[0] = p
    bonus_ref[0] = _seg_sum(r * k2 * rk_ref[...], bd) * v


def _rwkv_pre(pb, mu, w0, a0, k_k, k_a, r_k, w_up, a_up, g_up, bd, tm=128):
    bsz, l, _ = pb.shape
    tm = min(tm, l)
    assert tm % RW_CHUNK == 0
    tril = jnp.kron(jnp.eye(tm // RW_CHUNK, dtype=F32), jnp.tril(jnp.ones((RW_CHUNK, RW_CHUNK), F32)))
    full = lambda shp: pl.BlockSpec(shp, lambda b, i: (0,) * len(shp))
    row = pl.BlockSpec((1, tm, W_B), lambda b, i: (b, i, 0))
    vec = full((1, W_B))
    return pl.pallas_call(
        _rwkv_pre_kernel,
        grid=(bsz, l // tm),
        in_specs=[pl.BlockSpec((1, tm, B_PAD), lambda b, i: (b, i, 0)), full((1, B_PAD)),
                  vec, vec, vec, vec, vec,
                  full((LORA_W, W_B)), full((LORA_A, W_B)), full((LORA_G_PAD, W_B)), full((LANES, LANES)),
                  full((tm, tm))],
        out_specs=[row] * 8,
        out_shape=[jax.ShapeDtypeStruct((bsz, l, W_B), F32)] * 8,
        scratch_shapes=[pltpu.VMEM((1, B_PAD), F32)],
        compiler_params=_cparams("parallel", "arbitrary"),
        name="rwkv_pre",
    )(pb, mu, w0.reshape(1, W_B), a0.reshape(1, W_B), k_k.reshape(1, W_B), k_a.reshape(1, W_B),
      r_k.reshape(1, W_B), w_up, a_up, g_up, bd, tril)


def _mm(a, b, dims, passes):
    dg = lambda x, y: lax.dot_general(x, y, dims, preferred_element_type=F32)
    ah, bh = a.astype(BF16), b.astype(BF16)
    if passes == 1:
        return dg(ah, bh)
    al = (a - ah.astype(F32)).astype(BF16)
    bl = (b - bh.astype(F32)).astype(BF16)
    return dg(ah, bh) + (dg(ah, bl) + dg(al, bh))


def _rwkv_chunk_kernel(ab_ref, bb_ref, kb_ref, rb_ref, vt_ref, pe_ref, yt_ref, s_ref, *, hb):
    @pl.when(pl.program_id(2) == 0)
    def _():
        s_ref[...] = jnp.zeros_like(s_ref)

    c = RW_CHUNK
    ri = lax.broadcasted_iota(jnp.int32, (c, c), 0)
    ci = lax.broadcasted_iota(jnp.int32, (c, c), 1)
    strict, incl = ci < ri, ci <= ri
    eye = jnp.where(ri == ci, 1.0, 0.0)
    nt = (((2,), (2,)), ((0,), (0,)))
    nn = (((2,), (1,)), ((0,), (0,)))
    ab, bb, kb, rb = ab_ref[0], bb_ref[0], kb_ref[0], rb_ref[0]
    vt = vt_ref[0, :, 0]
    s0 = s_ref[...]
    ar = jnp.concatenate([ab, rb], axis=1)
    gb = _mm(ar, bb, nt, RW_PASSES)
    gk = _mm(ar, kb, nt, RW_PASSES)
    m_ab = jnp.where(strict, gb[:, :c], 0.0)
    n_rb = jnp.where(incl, gb[:, c:], 0.0)
    m_ak = jnp.where(strict, gk[:, :c], 0.0)
    n_rk = jnp.where(incl, gk[:, c:], 0.0)
    tinv = eye + m_ab
    pw = m_ab
    for _ in range(5):
        pw = _mm(pw, pw, nn, RW_PASSES)
        tinv = tinv + _mm(tinv, pw, nn, RW_PASSES)
    za = _mm(s0, ab, nt, RW_PASSES)
    zr = _mm(s0, rb, nt, RW_PASSES)
    ut = _mm(za + _mm(vt, m_ak, nt, RW_PASSES), tinv, nt, RW_PASSES)
    yt_ref[0, :, 0] = zr + _mm(ut, n_rb, nt, RW_PASSES) + _mm(vt, n_rk, nt, RW_PASSES)
    s_ref[...] = (s0 + _mm(ut, bb, nn, RW_PASSES) + _mm(vt, kb, nn, RW_PASSES)) * pe_ref[0, :, 0]


def _rwkv_chunks(ab, bb, kb, rb, vt, pe, hb=12):
    bsz, nh, l, n = ab.shape
    nt = l // RW_CHUNK
    row = pl.BlockSpec((1, hb, RW_CHUNK, n), lambda b, h, i: (b, h, i, 0))
    tile = pl.BlockSpec((1, hb, 1, n, RW_CHUNK), lambda b, h, i: (b, h, i, 0, 0))
    return pl.pallas_call(
        functools.partial(_rwkv_chunk_kernel, hb=hb),
        grid=(bsz, nh // hb, nt),
        in_specs=[row] * 4 + [tile, pl.BlockSpec((1, hb, 1, 1, n), lambda b, h, i: (b, h, i, 0, 0))],
        out_specs=tile,
        out_shape=jax.ShapeDtypeStruct((bsz, nh, nt, n, RW_CHUNK), F32),
        scratch_shapes=[pltpu.VMEM((hb, n, n), F32)],
        compiler_params=_cparams("parallel", "parallel", "arbitrary"),
        name="rwkv_chunks",
    )(ab, bb, kb, rb, vt, pe)


def _rwkv_post_kernel(y_ref, bonus_ref, g_ref, lw_ref, lb_ref, bd_ref, o_ref):
    bd = bd_ref[...]
    y = y_ref[0]
    mean = _seg_sum(y, bd) * (1.0 / N_B)
    c = y - mean
    var = _seg_sum(c * c, bd) * (1.0 / N_B)
    yn = c * lax.rsqrt(var + GN_EPS) * lw_ref[...] + lb_ref[...]
    o_ref[0] = ((yn + bonus_ref[0]) * g_ref[0]).astype(o_ref.dtype)


def _rwkv_post(y, bonus, g, lnx_w, lnx_b, bd, tm=256):
    bsz, l, _ = y.shape
    tm = min(tm, l)
    row = pl.BlockSpec((1, tm, W_B), lambda b, i: (b, i, 0))
    vec = pl.BlockSpec((1, W_B), lambda b, i: (0, 0))
    return pl.pallas_call(
        _rwkv_post_kernel,
        grid=(bsz, l // tm),
        in_specs=[row, row, row, vec, vec, pl.BlockSpec((LANES, LANES), lambda b, i: (0, 0))],
        out_specs=row,
        out_shape=jax.ShapeDtypeStruct((bsz, l, W_B), BF16),
        compiler_params=_cparams("parallel", "parallel"),
        name="rwkv_post",
    )(y, bonus, g, lnx_w.reshape(1, W_B), lnx_b.reshape(1, W_B), bd)


def _rwkv_mixer(pb, mu, w0, w_up, a0, a_up, g_up, k_k, k_a, r_k, lnx_w, lnx_b):
    bsz, l, _ = pb.shape
    nt = l // RW_CHUNK
    eye2 = jnp.kron(jnp.eye(2, dtype=F32), jnp.ones((N_B, N_B), F32)).astype(BF16)
    ab, bb, kb, rb, v, p, g, bonus = _rwkv_pre(pb, mu, w0, a0, k_k, k_a, r_k, w_up, a_up, g_up, eye2)
    heads = lambda a: a.reshape(bsz, l, H_B, N_B).transpose(0, 2, 1, 3)
    vt = v.reshape(bsz, nt, RW_CHUNK, H_B, N_B).transpose(0, 3, 1, 4, 2)
    pe = p[:, RW_CHUNK - 1::RW_CHUNK].reshape(bsz, nt, H_B, 1, N_B).transpose(0, 2, 1, 3, 4)
    yt = _rwkv_chunks(heads(ab), heads(bb), heads(kb), heads(rb), vt, pe)
    y = yt.transpose(0, 2, 4, 1, 3).reshape(bsz, l, W_B)
    return _rwkv_post(y, bonus, g, lnx_w, lnx_b, eye2)


def _hgrn_kernel(q_ref, f_ref, i_ref, g_ref, lb_ref, on_ref, tril_ref, o_ref, st_ref, *, layer, hb, tt):
    @pl.when(pl.program_id(2) == 0)
    def _():
        st_ref[...] = jnp.zeros_like(st_ref)

    x = lb_ref[...]
    e = jnp.exp(x - jnp.max(x, axis=0, keepdims=True))
    sm = e / jnp.sum(e, axis=0, keepdims=True)
    cs = sm[0:1]
    for i in range(1, layer + 1):
        cs = cs + sm[i:i + 1]
    lb_all = cs - sm[0:1]
    rowi = lax.broadcasted_iota(jnp.int32, (SUB, 1), 0)
    for hh in range(hb):
        sl = slice(hh * LANES, (hh + 1) * LANES)
        lb = lb_all[:, sl]
        qraw = q_ref[0][:, sl]
        ff = f_ref[0][:, sl]
        v = i_ref[0][:, sl]
        gg = g_ref[0][:, sl]
        q = qraw * _sigmoid(qraw)
        logsig = jnp.minimum(ff, 0.0) - jnp.log(1.0 + jnp.exp(-jnp.abs(ff)))
        la = jnp.log(lb)
        lbv = jnp.log(1.0 - lb) + logsig
        mx = jnp.maximum(la, lbv)
        logf = mx + jnp.log(jnp.exp(la - mx) + jnp.exp(lbv - mx))
        k = (1.0 - lb) * _sigmoid(-ff)
        bcum = jnp.dot(tril_ref[...], logf, precision=lax.Precision.HIGHEST, preferred_element_type=F32)
        st = st_ref[hh]
        outs = []
        for n in range(tt // SUB):
            rs = slice(n * SUB, (n + 1) * SUB)
            bq, qn, kn, vn = bcum[rs], q[rs], k[rs], v[rs]
            bend = bq[SUB - 1:SUB]
            qd = (qn * jnp.exp(bq)).astype(BF16)
            o = lax.dot_general(qd, st.astype(BF16), NT_DIMS, preferred_element_type=F32)
            for s in range(SUB):
                wgt = kn[s:s + 1] * jnp.exp(jnp.minimum(bq - bq[s:s + 1], 0.0))
                a_s = jnp.sum(qn * wgt, axis=1, keepdims=True)
                o = o + jnp.where(rowi >= s, a_s, 0.0) * vn[s:s + 1]
            kd = (kn * jnp.exp(bend - bq)).astype(BF16)
            st = st * jnp.exp(bend) + lax.dot_general(vn.astype(BF16), kd, TN_DIMS,
                                                      preferred_element_type=F32)
            outs.append(o)
        st_ref[hh] = st
        o = jnp.concatenate(outs, axis=0)
        o = _rms(o) * on_ref[:, sl] * (gg * _sigmoid(gg))
        o_ref[0, :, sl] = o.astype(o_ref.dtype)


def _hgrn_mixer(pc, hgrn_lb, onorm, layer, hb=4, tt=128):
    bsz, l, _ = pc.shape
    depth = hgrn_lb.shape[0]
    tt = min(tt, l)
    ng = H_C // hb
    wblk = hb * LANES
    tril = jnp.kron(jnp.eye(tt // SUB, dtype=F32), jnp.tril(jnp.ones((SUB, SUB), F32)))
    col = lambda j: pl.BlockSpec((1, tt, wblk), lambda b, h, i, j=j: (b, i, j * ng + h))
    return pl.pallas_call(
        functools.partial(_hgrn_kernel, layer=layer, hb=hb, tt=tt),
        grid=(bsz, ng, l // tt),
        in_specs=[col(0), col(1), col(2), col(3),
                  pl.BlockSpec((depth, wblk), lambda b, h, i: (0, h)),
                  pl.BlockSpec((1, wblk), lambda b, h, i: (0, h)),
                  pl.BlockSpec((tt, tt), lambda b, h, i: (0, 0))],
        out_specs=pl.BlockSpec((1, tt, wblk), lambda b, h, i: (b, i, h)),
        out_shape=jax.ShapeDtypeStruct((bsz, l, W_C), BF16),
        scratch_shapes=[pltpu.VMEM((hb, DV_C, DK_C), F32)],
        compiler_params=_cparams("parallel", "parallel", "arbitrary"),
        name="hgrn2",
    )(pc, pc, pc, pc, hgrn_lb, onorm.reshape(1, W_C), tril)


def _pack_w_in(w):
    d = w.shape[0]
    a_cols = R_Q + R_KV + D_I + H_I
    b_cols = 3 * W_B + LORA_W + LORA_A + LORA_G
    wa, wb, wc = w[:, :a_cols], w[:, a_cols:a_cols + b_cols], w[:, a_cols + b_cols:]
    z = lambda n: jnp.zeros((d, n), w.dtype)
    o = R_Q + R_KV
    wa = jnp.concatenate([wa[:, :o + D_I], z(LANES - D_I), wa[:, o + D_I:], z(LANES - H_I)], axis=1)
    wb = jnp.concatenate([wb, z(LORA_G_PAD - LORA_G)], axis=1)
    return wa.astype(BF16), wb.astype(BF16), wc.astype(BF16)


def kernel(x, c, rel_bias, hgrn_lb, ada_w, ada_b, norm_g, w_in, w_out, mla_q_norm, mla_kv_norm, w_uq, w_uk, w_uv, w_qidx, rwkv_mu, rwkv_w0, rwkv_w_up, rwkv_a0, rwkv_a_up, rwkv_g_up, rwkv_k_k, rwkv_k_a, rwkv_r_k, rwkv_lnx_w, rwkv_lnx_b, hgrn_onorm, w_ff1, w_ff2):
    bsz, l, d = x.shape
    depth = ada_w.shape[0]
    m = bsz * l
    mod = _ada_mod(c, ada_w, ada_b)
    mods = [[mod[i, :, None, j * d:(j + 1) * d] for j in range(6)] for i in range(depth)]
    bias = _bias_tiles(rel_bias)
    h = _norm_mod(x, norm_g[0, 0], mods[0][1], mods[0][0])
    for i in range(depth):
        sh_m, sc_m, g_m, sh_f, sc_f, g_f = mods[i]
        wa, wb, wc = _pack_w_in(w_in[i])
        h2d = h.reshape(m, d)
        pa = _matmul(h2d, wa, F32, 1024, A_PAD, 2048, name="in_proj_a").reshape(bsz, l, -1)
        pb = _matmul(h2d, wb, F32, 1024, 768, 2048, name="in_proj_b").reshape(bsz, l, -1)
        pc = _matmul(h2d, wc, F32, 1024, 1024, 2048, name="in_proj_c").reshape(bsz, l, -1)
        wuq = w_uq[i].reshape(R_Q, H_A * DH_A).astype(BF16)
        wqi = jnp.transpose(w_qidx[i], (1, 0, 2)).astype(BF16)
        qlat, qidx, widx, kidx, ckv = _dsa_prep(pa, mla_q_norm[i], mla_kv_norm[i], wuq,
                                                w_uk[i].astype(BF16), wqi)
        y_a = _dsa_attn(qlat, qidx, widx, kidx, ckv, w_uv[i].astype(BF16), bias)
        mu = jnp.concatenate([rwkv_mu[i], jnp.zeros((LORA_G_PAD - LORA_G,), F32)]).reshape(1, B_PAD)
        gup = jnp.concatenate([rwkv_g_up[i], jnp.zeros((LORA_G_PAD - LORA_G, W_B), F32)]).astype(BF16)
        y_b = _rwkv_mixer(pb, mu, rwkv_w0[i], rwkv_w_up[i], rwkv_a0[i], rwkv_a_up[i], gup,
                          rwkv_k_k[i], rwkv_k_a[i], rwkv_r_k[i], rwkv_lnx_w[i], rwkv_lnx_b[i])
        y_c = _hgrn_mixer(pc, hgrn_lb, hgrn_onorm[i], i)
        ycat = jnp.concatenate([y_a, y_b, y_c], axis=-1).reshape(m, -1)
        y = _matmul(ycat, w_out[i].astype(BF16), F32, 1024, 1024, 2048, name="out_proj").reshape(bsz, l, d)
        x, h = _resid_norm(x, y, g_m, norm_g[i, 1], (norm_g[i, 2], sc_f, sh_f))
        u = _matmul(h.reshape(m, d), w_ff1[i].astype(BF16), BF16, 1024, 1024, 2048, relu2=True, name="ffn_up")
        y = _matmul(u, w_ff2[i].astype(BF16), F32, 1024, 1024, 2048, name="ffn_down").reshape(bsz, l, d)
        nxt = None if i == depth - 1 else (norm_g[i + 1, 0], mods[i + 1][1], mods[i + 1][0])
        x, h = _resid_norm(x, y, g_f, norm_g[i, 3], nxt)
    return x
```

```python
import functools
import math

import jax
import jax.numpy as jnp
from jax import lax
from jax.experimental import pallas as pl
from jax.experimental.pallas import tpu as pltpu

H_A, DH_A, R_Q, R_KV, H_I, D_I = 8, 128, 768, 256, 16, 64
TOPK_MAX, Q_BLOCK, N_BUCKETS, MAX_DIST = 256, 128, 32, 128
H_B, N_B, LORA_W, LORA_A, LORA_G = 24, 64, 128, 128, 480
W_B = H_B * N_B
H_C, DK_C, DV_C = 12, 128, 128
W_C = H_C * DK_C
EPS = 1e-6
GN_EPS = 64e-5

LANES = 128
KEY_CHUNK = 256
SUB = 16
RW_CHUNK = 64
RW_PASSES = 1
LORA_G_PAD = 512
A_PAD = 1280
B_PAD = 3 * W_B + LORA_W + LORA_A + LORA_G_PAD
VMEM_LIMIT = 56 * 1024 * 1024

F32 = jnp.float32
BF16 = jnp.bfloat16
NT_DIMS = (((1,), (1,)), ((), ()))
TN_DIMS = (((0,), (0,)), ((), ()))
INT_MIN = -2 ** 31


def _cparams(*sem):
    return pltpu.CompilerParams(dimension_semantics=sem, vmem_limit_bytes=VMEM_LIMIT)


def _sigmoid(x):
    return 1.0 / (1.0 + jnp.exp(-x))


def _rms(x, eps=EPS):
    return x * lax.rsqrt(jnp.mean(x * x, axis=-1, keepdims=True) + eps)


def _mm_kernel(a_ref, b_ref, o_ref, acc_ref, *, nk, relu2):
    k = pl.program_id(2)

    @pl.when(k == 0)
    def _():
        acc_ref[...] = jnp.zeros_like(acc_ref)

    acc_ref[...] += jnp.dot(a_ref[...], b_ref[...], preferred_element_type=F32)

    @pl.when(k == nk - 1)
    def _():
        r = acc_ref[...]
        if relu2:
            r = jnp.square(jnp.maximum(r, 0.0))
        o_ref[...] = r.astype(o_ref.dtype)


def _mm_full_k_kernel(a_ref, b_ref, o_ref, *, relu2):
    r = jnp.dot(a_ref[...], b_ref[...], preferred_element_type=F32)
    if relu2:
        r = jnp.square(jnp.maximum(r, 0.0))
    o_ref[...] = r.astype(o_ref.dtype)


def _matmul(a, b, out_dtype, tm, tn, tk, relu2=False, name="matmul"):
    m, kdim = a.shape
    _, n = b.shape
    tm, tn, tk = min(tm, m), min(tn, n), min(tk, kdim)
    assert m % tm == 0 and n % tn == 0 and kdim % tk == 0, (a.shape, b.shape, tm, tn, tk)
    nk = kdim // tk
    if nk == 1:
        return pl.pallas_call(
            functools.partial(_mm_full_k_kernel, relu2=relu2),
            grid=(m // tm, n // tn),
            in_specs=[pl.BlockSpec((tm, kdim), lambda i, j: (i, 0)),
                      pl.BlockSpec((kdim, tn), lambda i, j: (0, j))],
            out_specs=pl.BlockSpec((tm, tn), lambda i, j: (i, j)),
            out_shape=jax.ShapeDtypeStruct((m, n), out_dtype),
            compiler_params=_cparams("parallel", "parallel"),
            name=name,
        )(a, b)
    return pl.pallas_call(
        functools.partial(_mm_kernel, nk=nk, relu2=relu2),
        grid=(m // tm, n // tn, nk),
        in_specs=[pl.BlockSpec((tm, tk), lambda i, j, k: (i, k)),
                  pl.BlockSpec((tk, tn), lambda i, j, k: (k, j))],
        out_specs=pl.BlockSpec((tm, tn), lambda i, j, k: (i, j)),
        out_shape=jax.ShapeDtypeStruct((m, n), out_dtype),
        scratch_shapes=[pltpu.VMEM((tm, tn), F32)],
        compiler_params=_cparams("parallel", "parallel", "arbitrary"),
        name=name,
    )(a, b)


def _ada_kernel(c_ref, w_ref, b_ref, o_ref):
    k = pl.program_id(2)

    @pl.when(k == 0)
    def _():
        o_ref[0] = jnp.broadcast_to(b_ref[0], o_ref.shape[1:])

    c = c_ref[...]
    ca = (c * _sigmoid(c)).astype(BF16)
    o_ref[0] += jnp.dot(ca, w_ref[0].astype(BF16), preferred_element_type=F32)


def _ada_mod(c, ada_w, ada_b):
    nl, d, n = ada_w.shape
    bsz = c.shape[0]
    rows = 8
    cp = jnp.zeros((rows, d), F32).at[:bsz].set(c)
    tn, tk = min(2048, n), min(1024, d)
    out = pl.pallas_call(
        _ada_kernel,
        grid=(nl, n // tn, d // tk),
        in_specs=[pl.BlockSpec((rows, tk), lambda l, j, k: (0, k)),
                  pl.BlockSpec((1, tk, tn), lambda l, j, k: (l, k, j)),
                  pl.BlockSpec((1, 1, tn), lambda l, j, k: (l, 0, j))],
        out_specs=pl.BlockSpec((1, rows, tn), lambda l, j, k: (l, 0, j)),
        out_shape=jax.ShapeDtypeStruct((nl, rows, n), F32),
        compiler_params=_cparams("parallel", "parallel", "arbitrary"),
        name="ada_mod",
    )(cp, ada_w, ada_b.reshape(nl, 1, n))
    return out[:, :bsz]


def _norm_mod_kernel(x_ref, g_ref, sc_ref, sh_ref, h_ref):
    y = _rms(x_ref[0]) * g_ref[...]
    h_ref[0] = (y * (1.0 + sc_ref[0]) + sh_ref[0]).astype(h_ref.dtype)


def _norm_mod(x, g, sc, sh, tm=256):
    bsz, l, d = x.shape
    tm = min(tm, l)
    row = pl.BlockSpec((1, tm, d), lambda b, i: (b, i, 0))
    vec = pl.BlockSpec((1, d), lambda b, i: (0, 0))
    bvec = pl.BlockSpec((1, 1, d), lambda b, i: (b, 0, 0))
    return pl.pallas_call(
        _norm_mod_kernel,
        grid=(bsz, l // tm),
        in_specs=[row, vec, bvec, bvec],
        out_specs=row,
        out_shape=jax.ShapeDtypeStruct((bsz, l, d), BF16),
        compiler_params=_cparams("parallel", "parallel"),
        name="norm_mod",
    )(x, g.reshape(1, d), sc, sh)


def _resid_kernel(x_ref, y_ref, gate_ref, g1_ref, g2_ref, sc_ref, sh_ref, xo_ref, h_ref):
    xn = x_ref[0] + gate_ref[0] * (_rms(y_ref[0]) * g1_ref[...])
    xo_ref[0] = xn
    hn = _rms(xn) * g2_ref[...]
    h_ref[0] = (hn * (1.0 + sc_ref[0]) + sh_ref[0]).astype(h_ref.dtype)


def _resid_last_kernel(x_ref, y_ref, gate_ref, g1_ref, xo_ref):
    xo_ref[0] = x_ref[0] + gate_ref[0] * (_rms(y_ref[0]) * g1_ref[...])


def _resid_norm(x, y, gate, g1, nxt=None, tm=128):
    bsz, l, d = x.shape
    tm = min(tm, l)
    row = pl.BlockSpec((1, tm, d), lambda b, i: (b, i, 0))
    vec = pl.BlockSpec((1, d), lambda b, i: (0, 0))
    bvec = pl.BlockSpec((1, 1, d), lambda b, i: (b, 0, 0))
    if nxt is None:
        return pl.pallas_call(
            _resid_last_kernel,
            grid=(bsz, l // tm),
            in_specs=[row, row, bvec, vec],
            out_specs=row,
            out_shape=jax.ShapeDtypeStruct((bsz, l, d), F32),
            compiler_params=_cparams("parallel", "parallel"),
            name="resid_last",
        )(x, y, gate, g1.reshape(1, d)), None
    g2, sc, sh = nxt
    return pl.pallas_call(
        _resid_kernel,
        grid=(bsz, l // tm),
        in_specs=[row, row, bvec, vec, vec, bvec, bvec],
        out_specs=[row, row],
        out_shape=[jax.ShapeDtypeStruct((bsz, l, d), F32), jax.ShapeDtypeStruct((bsz, l, d), BF16)],
        compiler_params=_cparams("parallel", "parallel"),
        name="resid_norm",
    )(x, y, gate, g1.reshape(1, d), g2.reshape(1, d), sc, sh)


def _bias_kernel(rb_ref, o_ref):
    i = lax.broadcasted_iota(jnp.int32, (Q_BLOCK, KEY_CHUNK), 0)
    j = lax.broadcasted_iota(jnp.int32, (Q_BLOCK, KEY_CHUNK), 1)
    max_exact = N_BUCKETS // 2
    for v, off in enumerate((Q_BLOCK, 0, 2 * Q_BLOCK, None)):
        if off is None:
            dist = jnp.full((Q_BLOCK, KEY_CHUNK), 2 * MAX_DIST, jnp.int32)
        else:
            dist = jnp.maximum(i + off - j, 0)
        nf = jnp.maximum(dist, 1).astype(F32)
        large = max_exact + (jnp.log(nf / max_exact) / math.log(MAX_DIST / max_exact)
                             * (N_BUCKETS - max_exact)).astype(jnp.int32)
        large = jnp.minimum(large, N_BUCKETS - 1)
        bucket = jnp.where(dist < max_exact, dist, large)
        for h in range(H_A):
            def body(b, acc, h=h, bucket=bucket):
                return jnp.where(bucket == b, rb_ref[b * H_A + h], acc)
            o_ref[v, h] = lax.fori_loop(0, N_BUCKETS, body, jnp.zeros((Q_BLOCK, KEY_CHUNK), F32))


def _bias_tiles(rel_bias):
    return pl.pallas_call(
        _bias_kernel,
        in_specs=[pl.BlockSpec(memory_space=pltpu.SMEM)],
        out_specs=pl.BlockSpec(memory_space=pltpu.VMEM),
        out_shape=jax.ShapeDtypeStruct((4, H_A, Q_BLOCK, KEY_CHUNK), F32),
        name="dsa_bias_tiles",
    )(rel_bias.reshape(-1))


def _dsa_prep_kernel(pa_ref, gq_ref, gkv_ref, wuq_ref, wuk_ref, wqi_ref,
                     qlat_ref, qidx_ref, widx_ref, kidx_ref, ckv_ref):
    pa = pa_ref[0]
    cq = pa[:, :R_Q]
    ckv = pa[:, R_Q:R_Q + R_KV]
    kid = pa[:, R_Q + R_KV:R_Q + R_KV + D_I]
    wid = pa[:, R_Q + R_KV + LANES:R_Q + R_KV + LANES + H_I]
    cqn = (_rms(cq) * gq_ref[...]).astype(BF16)
    ckv_ref[0] = (_rms(ckv) * gkv_ref[...]).astype(BF16)
    kidx_ref[0] = kid.astype(BF16)
    widx_ref[0] = wid * (H_I ** -0.5 * D_I ** -0.5)
    q = jnp.dot(cqn, wuq_ref[...], preferred_element_type=F32)
    for h in range(H_A):
        qh = q[:, h * DH_A:(h + 1) * DH_A].astype(BF16)
        ql = lax.dot_general(qh, wuk_ref[h], NT_DIMS, preferred_element_type=F32)
        qlat_ref[0, 0, h * Q_BLOCK:(h + 1) * Q_BLOCK, :] = (ql * DH_A ** -0.5).astype(BF16)
    for h in range(H_I):
        qi = jnp.dot(cqn, wqi_ref[h], preferred_element_type=F32)
        qidx_ref[0, 0, h * Q_BLOCK:(h + 1) * Q_BLOCK, :] = qi.astype(BF16)


def _dsa_prep(pa, gq, gkv, wuq, wuk, wqi):
    bsz, l, _ = pa.shape
    nb = l // Q_BLOCK
    full = lambda shp: pl.BlockSpec(shp, lambda b, i: (0,) * len(shp))
    return pl.pallas_call(
        _dsa_prep_kernel,
        grid=(bsz, nb),
        in_specs=[pl.BlockSpec((1, Q_BLOCK, A_PAD), lambda b, i: (b, i, 0)),
                  full((1, R_Q)), full((1, R_KV)), full((R_Q, H_A * DH_A)),
                  full((H_A, R_KV, DH_A)), full((H_I, R_Q, D_I))],
        out_specs=[pl.BlockSpec((1, 1, H_A * Q_BLOCK, R_KV), lambda b, i: (b, i, 0, 0)),
                   pl.BlockSpec((1, 1, H_I * Q_BLOCK, D_I), lambda b, i: (b, i, 0, 0)),
                   pl.BlockSpec((1, Q_BLOCK, H_I), lambda b, i: (b, i, 0)),
                   pl.BlockSpec((1, Q_BLOCK, D_I), lambda b, i: (b, i, 0)),
                   pl.BlockSpec((1, Q_BLOCK, R_KV), lambda b, i: (b, i, 0))],
        out_shape=[jax.ShapeDtypeStruct((bsz, nb, H_A * Q_BLOCK, R_KV), BF16),
                   jax.ShapeDtypeStruct((bsz, nb, H_I * Q_BLOCK, D_I), BF16),
                   jax.ShapeDtypeStruct((bsz, l, H_I), F32),
                   jax.ShapeDtypeStruct((bsz, l, D_I), BF16),
                   jax.ShapeDtypeStruct((bsz, l, R_KV), BF16)],
        compiler_params=_cparams("parallel", "parallel"),
        name="dsa_prep",
    )(pa, gq.reshape(1, R_Q), gkv.reshape(1, R_KV), wuq, wuk, wqi)


def _dsa_attn_kernel(qlat_ref, qidx_ref, widx_ref, kidx_ref, ckv_ref, wuv_ref, bias_ref, o_ref,
                     key_ref, m_ref, l_ref, mx_ref, ls_ref, acc_ref, p_ref, *, topk):
    qi = pl.program_id(1)
    nch = (qi * Q_BLOCK) // KEY_CHUNK + 1
    row = lax.broadcasted_iota(jnp.int32, (Q_BLOCK, KEY_CHUNK), 0)
    lane = lax.broadcasted_iota(jnp.int32, (Q_BLOCK, KEY_CHUNK), 1)
    t_abs = qi * Q_BLOCK + row
    w = widx_ref[0]
    wcols = [w[:, h:h + 1] for h in range(H_I)]

    def score_body(c, carry):
        start = pl.multiple_of(c * KEY_CHUNK, KEY_CHUNK)
        kc = kidx_ref[0, pl.ds(start, KEY_CHUNK), :]
        s_all = lax.dot_general(qidx_ref[0, 0], kc, NT_DIMS, preferred_element_type=F32)
        acc = jnp.zeros((Q_BLOCK, KEY_CHUNK), F32)
        for h in range(H_I):
            acc = acc + wcols[h] * jnp.maximum(s_all[h * Q_BLOCK:(h + 1) * Q_BLOCK], 0.0)
        acc = jnp.where(c * KEY_CHUNK + lane <= t_abs, acc, -jnp.inf)
        bits = lax.bitcast_convert_type(acc, jnp.int32)
        key_ref[c] = bits ^ ((bits >> 31) & 0x7FFFFFFF)
        return carry

    lax.fori_loop(0, nch, score_body, 0)

    @pl.when(nch % 2 == 1)
    def _():
        key_ref[nch] = jnp.full((Q_BLOCK, KEY_CHUNK), INT_MIN, jnp.int32)

    def count_ge(cand):
        def body(c, cnt):
            return (cnt + jnp.where(key_ref[2 * c] >= cand, 1.0, 0.0)
                    + jnp.where(key_ref[2 * c + 1] >= cand, 1.0, 0.0))
        cnt = lax.fori_loop(0, (nch + 1) // 2, body, jnp.zeros((Q_BLOCK, KEY_CHUNK), F32))
        return jnp.sum(cnt, axis=1, keepdims=True)

    kf = float(topk)
    cnt0 = count_ge(jnp.zeros((Q_BLOCK, 1), jnp.int32))
    thr = jnp.where(cnt0 >= kf, 0, INT_MIN).astype(jnp.int32)
    cnt_thr = jnp.where(cnt0 >= kf, cnt0, ((nch + 1) // 2 * (2 * KEY_CHUNK)).astype(F32))
    few = (qi * Q_BLOCK + lax.broadcasted_iota(jnp.int32, (Q_BLOCK, 1), 0)) < topk

    def unsettled(cnt_thr):
        return jnp.max(jnp.where(few | (cnt_thr == kf), 0.0, 1.0)) > 0.5

    def bit_cond(st):
        return (st[0] < 31) & st[3]

    def bit_body(st):
        i, thr, cnt_thr, _ = st
        cand = thr + jnp.left_shift(jnp.int32(1), 30 - i)
        cnt = count_ge(cand)
        take = cnt >= kf
        cnt_thr = jnp.where(take, cnt, cnt_thr)
        return i + 1, jnp.where(take, cand, thr), cnt_thr, unsettled(cnt_thr)

    thr = lax.while_loop(bit_cond, bit_body, (jnp.int32(0), thr, cnt_thr, unsettled(cnt_thr)))[1]

    odd = (qi % 2) == 1

    def chunk_logits(c):
        start = pl.multiple_of(c * KEY_CHUNK, KEY_CHUNK)
        kv = ckv_ref[0, pl.ds(start, KEY_CHUNK), :]
        logits = lax.dot_general(qlat_ref[0, 0], kv, NT_DIMS, preferred_element_type=F32)
        sel = (key_ref[c] >= thr) & (c * KEY_CHUNK + lane <= t_abs)
        idx = jnp.where(c == nch - 1, jnp.where(odd, 0, 1),
                        jnp.where((c == nch - 2) & jnp.logical_not(odd), 2, 3))
        return kv, logits, sel, idx

    mx_ref[...] = jnp.full(mx_ref.shape, -1e30, F32)

    def max_body(c, carry):
        _, logits, sel, idx = chunk_logits(c)
        for h in range(H_A):
            rs = slice(h * Q_BLOCK, (h + 1) * Q_BLOCK)
            mx_ref[rs] = jnp.maximum(mx_ref[rs], jnp.where(sel, logits[rs] + bias_ref[idx, h], -1e30))
        return carry

    lax.fori_loop(0, nch, max_body, 0)
    m_ref[...] = jnp.max(mx_ref[...], axis=1, keepdims=True)
    ls_ref[...] = jnp.zeros(ls_ref.shape, F32)
    acc_ref[...] = jnp.zeros(acc_ref.shape, F32)

    def attn_body(c, carry):
        kv, logits, sel, idx = chunk_logits(c)
        for h in range(H_A):
            rs = slice(h * Q_BLOCK, (h + 1) * Q_BLOCK)
            p = jnp.where(sel, jnp.exp(logits[rs] + bias_ref[idx, h] - m_ref[rs]), 0.0)
            ls_ref[rs] += p
            p_ref[rs] = p.astype(BF16)
        acc_ref[...] += jnp.dot(p_ref[...], kv, preferred_element_type=F32)
        return carry

    lax.fori_loop(0, nch, attn_body, 0)
    l_ref[...] = jnp.sum(ls_ref[...], axis=1, keepdims=True)

    for h in range(H_A):
        rs = slice(h * Q_BLOCK, (h + 1) * Q_BLOCK)
        o_lat = (acc_ref[rs] / l_ref[rs]).astype(BF16)
        out = jnp.dot(o_lat, wuv_ref[h], preferred_element_type=F32)
        o_ref[0, :, h * DH_A:(h + 1) * DH_A] = out.astype(o_ref.dtype)


def _dsa_attn(qlat, qidx, widx, kidx, ckv, wuv, bias):
    bsz, l, _ = ckv.shape
    nb = l // Q_BLOCK
    topk = min(TOPK_MAX, l // 4)
    nchunks = 2 * ((l + 2 * KEY_CHUNK - 1) // (2 * KEY_CHUNK))
    hq = H_A * Q_BLOCK
    return pl.pallas_call(
        functools.partial(_dsa_attn_kernel, topk=topk),
        grid=(bsz, nb),
        in_specs=[pl.BlockSpec((1, 1, hq, R_KV), lambda b, i: (b, i, 0, 0)),
                  pl.BlockSpec((1, 1, H_I * Q_BLOCK, D_I), lambda b, i: (b, i, 0, 0)),
                  pl.BlockSpec((1, Q_BLOCK, H_I), lambda b, i: (b, i, 0)),
                  pl.BlockSpec((1, l, D_I), lambda b, i: (b, 0, 0)),
                  pl.BlockSpec((1, l, R_KV), lambda b, i: (b, 0, 0)),
                  pl.BlockSpec((H_A, R_KV, DH_A), lambda b, i: (0, 0, 0)),
                  pl.BlockSpec((4, H_A, Q_BLOCK, KEY_CHUNK), lambda b, i: (0, 0, 0, 0))],
        out_specs=pl.BlockSpec((1, Q_BLOCK, H_A * DH_A), lambda b, i: (b, i, 0)),
        out_shape=jax.ShapeDtypeStruct((bsz, l, H_A * DH_A), BF16),
        scratch_shapes=[pltpu.VMEM((nchunks, Q_BLOCK, KEY_CHUNK), jnp.int32),
                        pltpu.VMEM((hq, 1), F32), pltpu.VMEM((hq, 1), F32),
                        pltpu.VMEM((hq, KEY_CHUNK), F32), pltpu.VMEM((hq, KEY_CHUNK), F32),
                        pltpu.VMEM((hq, R_KV), F32), pltpu.VMEM((hq, KEY_CHUNK), BF16)],
        compiler_params=_cparams("parallel", "arbitrary"),
        name="dsa_attn",
    )(qlat, qidx, widx, kidx, ckv, wuv, bias)


def _seg_sum(x, bd):
    hi = x.astype(BF16)
    lo = (x - hi.astype(F32)).astype(BF16)
    outs = []
    for p in range(x.shape[1] // LANES):
        sl = slice(p * LANES, (p + 1) * LANES)
        outs.append(jnp.dot(hi[:, sl], bd, preferred_element_type=F32)
                    + jnp.dot(lo[:, sl], bd, preferred_element_type=F32))
    return outs[0] if len(outs) == 1 else jnp.concatenate(outs, axis=1)


def _rwkv_pre_kernel(pb_ref, mu_ref, w0_ref, a0_ref, kk_ref, ka_ref, rk_ref, wup_ref, aup_ref, gup_ref,
                     bd_ref, tril_ref, ab_ref, bb_ref, kb_ref, rb_ref, v_ref, p_ref, g_ref, bonus_ref,
                     carry_ref):
    @pl.when(pl.program_id(1) == 0)
    def _():
        carry_ref[...] = jnp.zeros_like(carry_ref)

    x = pb_ref[0]
    tm = x.shape[0]
    rows = lax.broadcasted_iota(jnp.int32, x.shape, 0)
    prev = jnp.where(rows == 0, carry_ref[...], pltpu.roll(x, 1, 0))
    carry_ref[...] = x[tm - 1:tm]
    ps = x + mu_ref[...] * (prev - x)
    r = ps[:, :W_B]
    k = ps[:, W_B:2 * W_B]
    v = ps[:, 2 * W_B:3 * W_B]
    o = 3 * W_B
    wd = ps[:, o:o + LORA_W]
    ad = ps[:, o + LORA_W:o + LORA_W + LORA_A]
    gd = ps[:, o + LORA_W + LORA_A:]
    hp = lax.Precision.HIGHEST
    z = w0_ref[...] + jnp.dot(jnp.tanh(wd), wup_ref[...], precision=hp, preferred_element_type=F32)
    u = -z
    w = -(jnp.maximum(u, 0.0) + jnp.log(1.0 + jnp.exp(-jnp.abs(u)))) - 0.5
    logd = -jnp.exp(w)
    logp = jnp.dot(tril_ref[...], logd, precision=hp, preferred_element_type=F32)
    pinv = jnp.exp(-logp)
    a = _sigmoid(a0_ref[...] + jnp.dot(ad, aup_ref[...], precision=hp, preferred_element_type=F32))
    g_ref[0] = jnp.dot(_sigmoid(gd).astype(BF16), gup_ref[...], preferred_element_type=F32)
    bd = bd_ref[...]
    kk = k * kk_ref[...]
    kk = kk * lax.rsqrt(jnp.maximum(_seg_sum(kk * kk, bd), 1e-24))
    k2 = k * (1.0 + (a - 1.0) * ka_ref[...])
    p = jnp.exp(logp)
    ab_ref[0] = -kk * jnp.exp(logp - logd)
    bb_ref[0] = kk * a * pinv
    kb_ref[0] = k2 * pinv
    rb_ref[0] = r * p
    v_ref[0] = v
    p_ref[0] = p
    bonus_ref[0] = _seg_sum(r * k2 * rk_ref[...], bd) * v


def _rwkv_pre(pb, mu, w0, a0, k_k, k_a, r_k, w_up, a_up, g_up, bd, tm=128):
    bsz, l, _ = pb.shape
    tm = min(tm, l)
    assert tm % RW_CHUNK == 0
    tril = jnp.kron(jnp.eye(tm // RW_CHUNK, dtype=F32), jnp.tril(jnp.ones((RW_CHUNK, RW_CHUNK), F32)))
    full = lambda shp: pl.BlockSpec(shp, lambda b, i: (0,) * len(shp))
    row = pl.BlockSpec((1, tm, W_B), lambda b, i: (b, i, 0))
    vec = full((1, W_B))
    return pl.pallas_call(
        _rwkv_pre_kernel,
        grid=(bsz, l // tm),
        in_specs=[pl.BlockSpec((1, tm, B_PAD), lambda b, i: (b, i, 0)), full((1, B_PAD)),
                  vec, vec, vec, vec, vec,
                  full((LORA_W, W_B)), full((LORA_A, W_B)), full((LORA_G_PAD, W_B)), full((LANES, LANES)),
                  full((tm, tm))],
        out_specs=[row] * 8,
        out_shape=[jax.ShapeDtypeStruct((bsz, l, W_B), F32)] * 8,
        scratch_shapes=[pltpu.VMEM((1, B_PAD), F32)],
        compiler_params=_cparams("parallel", "arbitrary"),
        name="rwkv_pre",
    )(pb, mu, w0.reshape(1, W_B), a0.reshape(1, W_B), k_k.reshape(1, W_B), k_a.reshape(1, W_B),
      r_k.reshape(1, W_B), w_up, a_up, g_up, bd, tril)


def _mm(a, b, dims, passes):
    dg = lambda x, y: lax.dot_general(x, y, dims, preferred_element_type=F32)
    ah, bh = a.astype(BF16), b.astype(BF16)
    if passes == 1:
        return dg(ah, bh)
    al = (a - ah.astype(F32)).astype(BF16)
    bl = (b - bh.astype(F32)).astype(BF16)
    return dg(ah, bh) + (dg(ah, bl) + dg(al, bh))


def _rwkv_chunk_kernel(ab_ref, bb_ref, kb_ref, rb_ref, vt_ref, pe_ref, yt_ref, s_ref, *, hb):
    @pl.when(pl.program_id(2) == 0)
    def _():
        s_ref[...] = jnp.zeros_like(s_ref)

    c = RW_CHUNK
    ri = lax.broadcasted_iota(jnp.int32, (c, c), 0)
    ci = lax.broadcasted_iota(jnp.int32, (c, c), 1)
    strict, incl = ci < ri, ci <= ri
    eye = jnp.where(ri == ci, 1.0, 0.0)
    nt = (((2,), (2,)), ((0,), (0,)))
    nn = (((2,), (1,)), ((0,), (0,)))
    ab, bb, kb, rb = ab_ref[0], bb_ref[0], kb_ref[0], rb_ref[0]
    vt = vt_ref[0, :, 0]
    s0 = s_ref[...]
    ar = jnp.concatenate([ab, rb], axis=1)
    gb = _mm(ar, bb, nt, RW_PASSES)
    gk = _mm(ar, kb, nt, RW_PASSES)
    m_ab = jnp.where(strict, gb[:, :c], 0.0)
    n_rb = jnp.where(incl, gb[:, c:], 0.0)
    m_ak = jnp.where(strict, gk[:, :c], 0.0)
    n_rk = jnp.where(incl, gk[:, c:], 0.0)
    tinv = eye + m_ab
    pw = m_ab
    for _ in range(5):
        pw = _mm(pw, pw, nn, RW_PASSES)
        tinv = tinv + _mm(tinv, pw, nn, RW_PASSES)
    za = _mm(s0, ab, nt, RW_PASSES)
    zr = _mm(s0, rb, nt, RW_PASSES)
    ut = _mm(za + _mm(vt, m_ak, nt, RW_PASSES), tinv, nt, RW_PASSES)
    yt_ref[0, :, 0] = zr + _mm(ut, n_rb, nt, RW_PASSES) + _mm(vt, n_rk, nt, RW_PASSES)
    s_ref[...] = (s0 + _mm(ut, bb, nn, RW_PASSES) + _mm(vt, kb, nn, RW_PASSES)) * pe_ref[0, :, 0]


def _rwkv_chunks(ab, bb, kb, rb, vt, pe, hb=12):
    bsz, nh, l, n = ab.shape
    nt = l // RW_CHUNK
    row = pl.BlockSpec((1, hb, RW_CHUNK, n), lambda b, h, i: (b, h, i, 0))
    tile = pl.BlockSpec((1, hb, 1, n, RW_CHUNK), lambda b, h, i: (b, h, i, 0, 0))
    return pl.pallas_call(
        functools.partial(_rwkv_chunk_kernel, hb=hb),
        grid=(bsz, nh // hb, nt),
        in_specs=[row] * 4 + [tile, pl.BlockSpec((1, hb, 1, 1, n), lambda b, h, i: (b, h, i, 0, 0))],
        out_specs=tile,
        out_shape=jax.ShapeDtypeStruct((bsz, nh, nt, n, RW_CHUNK), F32),
        scratch_shapes=[pltpu.VMEM((hb, n, n), F32)],
        compiler_params=_cparams("parallel", "parallel", "arbitrary"),
        name="rwkv_chunks",
    )(ab, bb, kb, rb, vt, pe)


def _rwkv_post_kernel(y_ref, bonus_ref, g_ref, lw_ref, lb_ref, bd_ref, o_ref):
    bd = bd_ref[...]
    y = y_ref[0]
    mean = _seg_sum(y, bd) * (1.0 / N_B)
    c = y - mean
    var = _seg_sum(c * c, bd) * (1.0 / N_B)
    yn = c * lax.rsqrt(var + GN_EPS) * lw_ref[...] + lb_ref[...]
    o_ref[0] = ((yn + bonus_ref[0]) * g_ref[0]).astype(o_ref.dtype)


def _rwkv_post(y, bonus, g, lnx_w, lnx_b, bd, tm=256):
    bsz, l, _ = y.shape
    tm = min(tm, l)
    row = pl.BlockSpec((1, tm, W_B), lambda b, i: (b, i, 0))
    vec = pl.BlockSpec((1, W_B), lambda b, i: (0, 0))
    return pl.pallas_call(
        _rwkv_post_kernel,
        grid=(bsz, l // tm),
        in_specs=[row, row, row, vec, vec, pl.BlockSpec((LANES, LANES), lambda b, i: (0, 0))],
        out_specs=row,
        out_shape=jax.ShapeDtypeStruct((bsz, l, W_B), BF16),
        compiler_params=_cparams("parallel", "parallel"),
        name="rwkv_post",
    )(y, bonus, g, lnx_w.reshape(1, W_B), lnx_b.reshape(1, W_B), bd)


def _rwkv_mixer(pb, mu, w0, w_up, a0, a_up, g_up, k_k, k_a, r_k, lnx_w, lnx_b):
    bsz, l, _ = pb.shape
    nt = l // RW_CHUNK
    eye2 = jnp.kron(jnp.eye(2, dtype=F32), jnp.ones((N_B, N_B), F32)).astype(BF16)
    ab, bb, kb, rb, v, p, g, bonus = _rwkv_pre(pb, mu, w0, a0, k_k, k_a, r_k, w_up, a_up, g_up, eye2)
    heads = lambda a: a.reshape(bsz, l, H_B, N_B).transpose(0, 2, 1, 3)
    vt = v.reshape(bsz, nt, RW_CHUNK, H_B, N_B).transpose(0, 3, 1, 4, 2)
    pe = p[:, RW_CHUNK - 1::RW_CHUNK].reshape(bsz, nt, H_B, 1, N_B).transpose(0, 2, 1, 3, 4)
    yt = _rwkv_chunks(heads(ab), heads(bb), heads(kb), heads(rb), vt, pe)
    y = yt.transpose(0, 2, 4, 1, 3).reshape(bsz, l, W_B)
    return _rwkv_post(y, bonus, g, lnx_w, lnx_b, eye2)


def _hgrn_kernel(q_ref, f_ref, i_ref, g_ref, lb_ref, on_ref, tril_ref, o_ref, st_ref, *, layer, hb, tt):
    @pl.when(pl.program_id(2) == 0)
    def _():
        st_ref[...] = jnp.zeros_like(st_ref)

    x = lb_ref[...]
    e = jnp.exp(x - jnp.max(x, axis=0, keepdims=True))
    sm = e / jnp.sum(e, axis=0, keepdims=True)
    cs = sm[0:1]
    for i in range(1, layer + 1):
        cs = cs + sm[i:i + 1]
    lb_all = cs - sm[0:1]
    rowi = lax.broadcasted_iota(jnp.int32, (SUB, 1), 0)
    for hh in range(hb):
        sl = slice(hh * LANES, (hh + 1) * LANES)
        lb = lb_all[:, sl]
        qraw = q_ref[0][:, sl]
        ff = f_ref[0][:, sl]
        v = i_ref[0][:, sl]
        gg = g_ref[0][:, sl]
        q = qraw * _sigmoid(qraw)
        logsig = jnp.minimum(ff, 0.0) - jnp.log(1.0 + jnp.exp(-jnp.abs(ff)))
        la = jnp.log(lb)
        lbv = jnp.log(1.0 - lb) + logsig
        mx = jnp.maximum(la, lbv)
        logf = mx + jnp.log(jnp.exp(la - mx) + jnp.exp(lbv - mx))
        k = (1.0 - lb) * _sigmoid(-ff)
        bcum = jnp.dot(tril_ref[...], logf, precision=lax.Precision.HIGHEST, preferred_element_type=F32)
        st = st_ref[hh]
        outs = []
        for n in range(tt // SUB):
            rs = slice(n * SUB, (n + 1) * SUB)
            bq, qn, kn, vn = bcum[rs], q[rs], k[rs], v[rs]
            bend = bq[SUB - 1:SUB]
            qd = (qn * jnp.exp(bq)).astype(BF16)
            o = lax.dot_general(qd, st.astype(BF16), NT_DIMS, preferred_element_type=F32)
            for s in range(SUB):
                wgt = kn[s:s + 1] * jnp.exp(jnp.minimum(bq - bq[s:s + 1], 0.0))
                a_s = jnp.sum(qn * wgt, axis=1, keepdims=True)
                o = o + jnp.where(rowi >= s, a_s, 0.0) * vn[s:s + 1]
            kd = (kn * jnp.exp(bend - bq)).astype(BF16)
            st = st * jnp.exp(bend) + lax.dot_general(vn.astype(BF16), kd, TN_DIMS,
                                                      preferred_element_type=F32)
            outs.append(o)
        st_ref[hh] = st
        o = jnp.concatenate(outs, axis=0)
        o = _rms(o) * on_ref[:, sl] * (gg * _sigmoid(gg))
        o_ref[0, :, sl] = o.astype(o_ref.dtype)


def _hgrn_mixer(pc, hgrn_lb, onorm, layer, hb=4, tt=128):
    bsz, l, _ = pc.shape
    depth = hgrn_lb.shape[0]
    tt = min(tt, l)
    ng = H_C // hb
    wblk = hb * LANES
    tril = jnp.kron(jnp.eye(tt // SUB, dtype=F32), jnp.tril(jnp.ones((SUB, SUB), F32)))
    col = lambda j: pl.BlockSpec((1, tt, wblk), lambda b, h, i, j=j: (b, i, j * ng + h))
    return pl.pallas_call(
        functools.partial(_hgrn_kernel, layer=layer, hb=hb, tt=tt),
        grid=(bsz, ng, l // tt),
        in_specs=[col(0), col(1), col(2), col(3),
                  pl.BlockSpec((depth, wblk), lambda b, h, i: (0, h)),
                  pl.BlockSpec((1, wblk), lambda b, h, i: (0, h)),
                  pl.BlockSpec((tt, tt), lambda b, h, i: (0, 0))],
        out_specs=pl.BlockSpec((1, tt, wblk), lambda b, h, i: (b, i, h)),
        out_shape=jax.ShapeDtypeStruct((bsz, l, W_C), BF16),
        scratch_shapes=[pltpu.VMEM((hb, DV_C, DK_C), F32)],
        compiler_params=_cparams("parallel", "parallel", "arbitrary"),
        name="hgrn2",
    )(pc, pc, pc, pc, hgrn_lb, onorm.reshape(1, W_C), tril)


def _pack_w_in(w):
    d = w.shape[0]
    a_cols = R_Q + R_KV + D_I + H_I
    b_cols = 3 * W_B + LORA_W + LORA_A + LORA_G
    wa, wb, wc = w[:, :a_cols], w[:, a_cols:a_cols + b_cols], w[:, a_cols + b_cols:]
    z = lambda n: jnp.zeros((d, n), w.dtype)
    o = R_Q + R_KV
    wa = jnp.concatenate([wa[:, :o + D_I], z(LANES - D_I), wa[:, o + D_I:], z(LANES - H_I)], axis=1)
    wb = jnp.concatenate([wb, z(LORA_G_PAD - LORA_G)], axis=1)
    return wa.astype(BF16), wb.astype(BF16), wc.astype(BF16)


def kernel(x, c, rel_bias, hgrn_lb, ada_w, ada_b, norm_g, w_in, w_out, mla_q_norm, mla_kv_norm, w_uq, w_uk, w_uv, w_qidx, rwkv_mu, rwkv_w0, rwkv_w_up, rwkv_a0, rwkv_a_up, rwkv_g_up, rwkv_k_k, rwkv_k_a, rwkv_r_k, rwkv_lnx_w, rwkv_lnx_b, hgrn_onorm, w_ff1, w_ff2):
    bsz, l, d = x.shape
    depth = ada_w.shape[0]
    m = bsz * l
    mod = _ada_mod(c, ada_w, ada_b)
    mods = [[mod[i, :, None, j * d:(j + 1) * d] for j in range(6)] for i in range(depth)]
    bias = _bias_tiles(rel_bias)
    h = _norm_mod(x, norm_g[0, 0], mods[0][1], mods[0][0])
    for i in range(depth):
        sh_m, sc_m, g_m, sh_f, sc_f, g_f = mods[i]
        wa, wb, wc = _pack_w_in(w_in[i])
        h2d = h.reshape(m, d)
        pa = _matmul(h2d, wa, F32, 512, A_PAD, 4096, name="in_proj_a").reshape(bsz, l, -1)
        pb = _matmul(h2d, wb, F32, 1024, 768, 4096, name="in_proj_b").reshape(bsz, l, -1)
        pc = _matmul(h2d, wc, F32, 1024, 1024, 4096, name="in_proj_c").reshape(bsz, l, -1)
        wuq = w_uq[i].reshape(R_Q, H_A * DH_A).astype(BF16)
        wqi = jnp.transpose(w_qidx[i], (1, 0, 2)).astype(BF16)
        qlat, qidx, widx, kidx, ckv = _dsa_prep(pa, mla_q_norm[i], mla_kv_norm[i], wuq,
                                                w_uk[i].astype(BF16), wqi)
        y_a = _dsa_attn(qlat, qidx, widx, kidx, ckv, w_uv[i].astype(BF16), bias)
        mu = jnp.concatenate([rwkv_mu[i], jnp.zeros((LORA_G_PAD - LORA_G,), F32)]).reshape(1, B_PAD)
        gup = jnp.concatenate([rwkv_g_up[i], jnp.zeros((LORA_G_PAD - LORA_G, W_B), F32)]).astype(BF16)
        y_b = _rwkv_mixer(pb, mu, rwkv_w0[i], rwkv_w_up[i], rwkv_a0[i], rwkv_a_up[i], gup,
                          rwkv_k_k[i], rwkv_k_a[i], rwkv_r_k[i], rwkv_lnx_w[i], rwkv_lnx_b[i])
        y_c = _hgrn_mixer(pc, hgrn_lb, hgrn_onorm[i], i)
        ycat = jnp.concatenate([y_a, y_b, y_c], axis=-1).reshape(m, -1)
        y = _matmul(ycat, w_out[i].astype(BF16), F32, 1024, 1024, 4096, name="out_proj").reshape(bsz, l, d)
        x, h = _resid_norm(x, y, g_m, norm_g[i, 1], (norm_g[i, 2], sc_f, sh_f))
        u = _matmul(h.reshape(m, d), w_ff1[i].astype(BF16), BF16, 1024, 1024, 4096, relu2=True, name="ffn_up")
        y = _matmul(u, w_ff2[i].astype(BF16), F32, 1024, 1024, 4096, name="ffn_down").reshape(bsz, l, d)
        nxt = None if i == depth - 1 else (norm_g[i + 1, 0], mods[i + 1][1], mods[i + 1][0])
        x, h = _resid_norm(x, y, g_f, norm_g[i, 3], nxt)
    return x
```

```python
import functools
import math

import jax
import jax.numpy as jnp
from jax import lax
from jax.experimental import pallas as pl
from jax.experimental.pallas import tpu as pltpu

H_A, DH_A, R_Q, R_KV, H_I, D_I = 8, 128, 768, 256, 16, 64
TOPK_MAX, Q_BLOCK, N_BUCKETS, MAX_DIST = 256, 128, 32, 128
H_B, N_B, LORA_W, LORA_A, LORA_G = 24, 64, 128, 128, 480
W_B = H_B * N_B
H_C, DK_C, DV_C = 12, 128, 128
W_C = H_C * DK_C
EPS = 1e-6
GN_EPS = 64e-5

LANES = 128
KEY_CHUNK = 256
SUB = 8
RW_CHUNK = 64
RW_PASSES = 1
LORA_G_PAD = 512
A_PAD = 1280
B_PAD = 3 * W_B + LORA_W + LORA_A + LORA_G_PAD
VMEM_LIMIT = 56 * 1024 * 1024

F32 = jnp.float32
BF16 = jnp.bfloat16
NT_DIMS = (((1,), (1,)), ((), ()))
TN_DIMS = (((0,), (0,)), ((), ()))
INT_MIN = -2 ** 31


def _cparams(*sem):
    return pltpu.CompilerParams(dimension_semantics=sem, vmem_limit_bytes=VMEM_LIMIT)


def _sigmoid(x):
    return 1.0 / (1.0 + jnp.exp(-x))


def _rms(x, eps=EPS):
    return x * lax.rsqrt(jnp.mean(x * x, axis=-1, keepdims=True) + eps)


def _mm_kernel(a_ref, b_ref, o_ref, acc_ref, *, nk, relu2):
    k = pl.program_id(2)

    @pl.when(k == 0)
    def _():
        acc_ref[...] = jnp.zeros_like(acc_ref)

    acc_ref[...] += jnp.dot(a_ref[...], b_ref[...], preferred_element_type=F32)

    @pl.when(k == nk - 1)
    def _():
        r = acc_ref[...]
        if relu2:
            r = jnp.square(jnp.maximum(r, 0.0))
        o_ref[...] = r.astype(o_ref.dtype)


def _mm_full_k_kernel(a_ref, b_ref, o_ref, *, relu2):
    r = jnp.dot(a_ref[...], b_ref[...], preferred_element_type=F32)
    if relu2:
        r = jnp.square(jnp.maximum(r, 0.0))
    o_ref[...] = r.astype(o_ref.dtype)


def _matmul(a, b, out_dtype, tm, tn, tk, relu2=False, name="matmul"):
    m, kdim = a.shape
    _, n = b.shape
    tm, tn, tk = min(tm, m), min(tn, n), min(tk, kdim)
    assert m % tm == 0 and n % tn == 0 and kdim % tk == 0, (a.shape, b.shape, tm, tn, tk)
    nk = kdim // tk
    if nk == 1:
        return pl.pallas_call(
            functools.partial(_mm_full_k_kernel, relu2=relu2),
            grid=(m // tm, n // tn),
            in_specs=[pl.BlockSpec((tm, kdim), lambda i, j: (i, 0)),
                      pl.BlockSpec((kdim, tn), lambda i, j: (0, j))],
            out_specs=pl.BlockSpec((tm, tn), lambda i, j: (i, j)),
            out_shape=jax.ShapeDtypeStruct((m, n), out_dtype),
            compiler_params=_cparams("parallel", "parallel"),
            name=name,
        )(a, b)
    return pl.pallas_call(
        functools.partial(_mm_kernel, nk=nk, relu2=relu2),
        grid=(m // tm, n // tn, nk),
        in_specs=[pl.BlockSpec((tm, tk), lambda i, j, k: (i, k)),
                  pl.BlockSpec((tk, tn), lambda i, j, k: (k, j))],
        out_specs=pl.BlockSpec((tm, tn), lambda i, j, k: (i, j)),
        out_shape=jax.ShapeDtypeStruct((m, n), out_dtype),
        scratch_shapes=[pltpu.VMEM((tm, tn), F32)],
        compiler_params=_cparams("parallel", "parallel", "arbitrary"),
        name=name,
    )(a, b)


def _ada_kernel(c_ref, w_ref, b_ref, o_ref):
    k = pl.program_id(2)

    @pl.when(k == 0)
    def _():
        o_ref[0] = jnp.broadcast_to(b_ref[0], o_ref.shape[1:])

    c = c_ref[...]
    ca = (c * _sigmoid(c)).astype(BF16)
    o_ref[0] += jnp.dot(ca, w_ref[0].astype(BF16), preferred_element_type=F32)


def _ada_mod(c, ada_w, ada_b):
    nl, d, n = ada_w.shape
    bsz = c.shape[0]
    rows = 8
    cp = jnp.zeros((rows, d), F32).at[:bsz].set(c)
    tn, tk = min(2048, n), min(1024, d)
    out = pl.pallas_call(
        _ada_kernel,
        grid=(nl, n // tn, d // tk),
        in_specs=[pl.BlockSpec((rows, tk), lambda l, j, k: (0, k)),
                  pl.BlockSpec((1, tk, tn), lambda l, j, k: (l, k, j)),
                  pl.BlockSpec((1, 1, tn), lambda l, j, k: (l, 0, j))],
        out_specs=pl.BlockSpec((1, rows, tn), lambda l, j, k: (l, 0, j)),
        out_shape=jax.ShapeDtypeStruct((nl, rows, n), F32),
        compiler_params=_cparams("parallel", "parallel", "arbitrary"),
        name="ada_mod",
    )(cp, ada_w, ada_b.reshape(nl, 1, n))
    return out[:, :bsz]


def _norm_mod_kernel(x_ref, g_ref, sc_ref, sh_ref, h_ref):
    y = _rms(x_ref[0]) * g_ref[...]
    h_ref[0] = (y * (1.0 + sc_ref[0]) + sh_ref[0]).astype(h_ref.dtype)


def _norm_mod(x, g, sc, sh, tm=256):
    bsz, l, d = x.shape
    tm = min(tm, l)
    row = pl.BlockSpec((1, tm, d), lambda b, i: (b, i, 0))
    vec = pl.BlockSpec((1, d), lambda b, i: (0, 0))
    bvec = pl.BlockSpec((1, 1, d), lambda b, i: (b, 0, 0))
    return pl.pallas_call(
        _norm_mod_kernel,
        grid=(bsz, l // tm),
        in_specs=[row, vec, bvec, bvec],
        out_specs=row,
        out_shape=jax.ShapeDtypeStruct((bsz, l, d), BF16),
        compiler_params=_cparams("parallel", "parallel"),
        name="norm_mod",
    )(x, g.reshape(1, d), sc, sh)


def _resid_kernel(x_ref, y_ref, gate_ref, g1_ref, g2_ref, sc_ref, sh_ref, xo_ref, h_ref):
    xn = x_ref[0] + gate_ref[0] * (_rms(y_ref[0]) * g1_ref[...])
    xo_ref[0] = xn
    hn = _rms(xn) * g2_ref[...]
    h_ref[0] = (hn * (1.0 + sc_ref[0]) + sh_ref[0]).astype(h_ref.dtype)


def _resid_last_kernel(x_ref, y_ref, gate_ref, g1_ref, xo_ref):
    xo_ref[0] = x_ref[0] + gate_ref[0] * (_rms(y_ref[0]) * g1_ref[...])


def _resid_norm(x, y, gate, g1, nxt=None, tm=128):
    bsz, l, d = x.shape
    tm = min(tm, l)
    row = pl.BlockSpec((1, tm, d), lambda b, i: (b, i, 0))
    vec = pl.BlockSpec((1, d), lambda b, i: (0, 0))
    bvec = pl.BlockSpec((1, 1, d), lambda b, i: (b, 0, 0))
    if nxt is None:
        return pl.pallas_call(
            _resid_last_kernel,
            grid=(bsz, l // tm),
            in_specs=[row, row, bvec, vec],
            out_specs=row,
            out_shape=jax.ShapeDtypeStruct((bsz, l, d), F32),
            compiler_params=_cparams("parallel", "parallel"),
            name="resid_last",
        )(x, y, gate, g1.reshape(1, d)), None
    g2, sc, sh = nxt
    return pl.pallas_call(
        _resid_kernel,
        grid=(bsz, l // tm),
        in_specs=[row, row, bvec, vec, vec, bvec, bvec],
        out_specs=[row, row],
        out_shape=[jax.ShapeDtypeStruct((bsz, l, d), F32), jax.ShapeDtypeStruct((bsz, l, d), BF16)],
        compiler_params=_cparams("parallel", "parallel"),
        name="resid_norm",
    )(x, y, gate, g1.reshape(1, d), g2.reshape(1, d), sc, sh)


def _bias_kernel(rb_ref, o_ref):
    j = lax.broadcasted_iota(jnp.int32, (KEY_CHUNK, Q_BLOCK), 0)
    i = lax.broadcasted_iota(jnp.int32, (KEY_CHUNK, Q_BLOCK), 1)
    max_exact = N_BUCKETS // 2
    for v, off in enumerate((Q_BLOCK, 0, 2 * Q_BLOCK, None)):
        if off is None:
            dist = jnp.full((KEY_CHUNK, Q_BLOCK), 2 * MAX_DIST, jnp.int32)
        else:
            dist = jnp.maximum(i + off - j, 0)
        nf = jnp.maximum(dist, 1).astype(F32)
        large = max_exact + (jnp.log(nf / max_exact) / math.log(MAX_DIST / max_exact)
                             * (N_BUCKETS - max_exact)).astype(jnp.int32)
        large = jnp.minimum(large, N_BUCKETS - 1)
        bucket = jnp.where(dist < max_exact, dist, large)
        for h in range(H_A):
            def body(b, acc, h=h, bucket=bucket):
                return jnp.where(bucket == b, rb_ref[b * H_A + h], acc)
            o_ref[v, h] = lax.fori_loop(0, N_BUCKETS, body, jnp.zeros((KEY_CHUNK, Q_BLOCK), F32))


def _bias_tiles(rel_bias):
    return pl.pallas_call(
        _bias_kernel,
        in_specs=[pl.BlockSpec(memory_space=pltpu.SMEM)],
        out_specs=pl.BlockSpec(memory_space=pltpu.VMEM),
        out_shape=jax.ShapeDtypeStruct((4, H_A, KEY_CHUNK, Q_BLOCK), F32),
        name="dsa_bias_tiles",
    )(rel_bias.reshape(-1))


def _dsa_prep_kernel(pa_ref, gq_ref, gkv_ref, wuq_ref, wuk_ref, wqi_ref,
                     qlat_ref, qidx_ref, widx_ref, kidx_ref, ckv_ref):
    pa = pa_ref[0]
    cq = pa[:, :R_Q]
    ckv = pa[:, R_Q:R_Q + R_KV]
    kid = pa[:, R_Q + R_KV:R_Q + R_KV + D_I]
    wid = pa[:, R_Q + R_KV + LANES:R_Q + R_KV + LANES + H_I]
    cqn = (_rms(cq) * gq_ref[...]).astype(BF16)
    ckv_ref[0] = (_rms(ckv) * gkv_ref[...]).astype(BF16)
    kidx_ref[0] = kid.astype(BF16)
    widx_ref[0] = wid * (H_I ** -0.5 * D_I ** -0.5)
    q = jnp.dot(cqn, wuq_ref[...], preferred_element_type=F32)
    for h in range(H_A):
        qh = q[:, h * DH_A:(h + 1) * DH_A].astype(BF16)
        ql = lax.dot_general(qh, wuk_ref[h], NT_DIMS, preferred_element_type=F32)
        qlat_ref[0, 0, h * Q_BLOCK:(h + 1) * Q_BLOCK, :] = (ql * DH_A ** -0.5).astype(BF16)
    for h in range(H_I):
        qi = jnp.dot(cqn, wqi_ref[h], preferred_element_type=F32)
        qidx_ref[0, 0, h * Q_BLOCK:(h + 1) * Q_BLOCK, :] = qi.astype(BF16)


def _dsa_prep(pa, gq, gkv, wuq, wuk, wqi):
    bsz, l, _ = pa.shape
    nb = l // Q_BLOCK
    full = lambda shp: pl.BlockSpec(shp, lambda b, i: (0,) * len(shp))
    return pl.pallas_call(
        _dsa_prep_kernel,
        grid=(bsz, nb),
        in_specs=[pl.BlockSpec((1, Q_BLOCK, A_PAD), lambda b, i: (b, i, 0)),
                  full((1, R_Q)), full((1, R_KV)), full((R_Q, H_A * DH_A)),
                  full((H_A, R_KV, DH_A)), full((H_I, R_Q, D_I))],
        out_specs=[pl.BlockSpec((1, 1, H_A * Q_BLOCK, R_KV), lambda b, i: (b, i, 0, 0)),
                   pl.BlockSpec((1, 1, H_I * Q_BLOCK, D_I), lambda b, i: (b, i, 0, 0)),
                   pl.BlockSpec((1, Q_BLOCK, H_I), lambda b, i: (b, i, 0)),
                   pl.BlockSpec((1, Q_BLOCK, D_I), lambda b, i: (b, i, 0)),
                   pl.BlockSpec((1, Q_BLOCK, R_KV), lambda b, i: (b, i, 0))],
        out_shape=[jax.ShapeDtypeStruct((bsz, nb, H_A * Q_BLOCK, R_KV), BF16),
                   jax.ShapeDtypeStruct((bsz, nb, H_I * Q_BLOCK, D_I), BF16),
                   jax.ShapeDtypeStruct((bsz, l, H_I), F32),
                   jax.ShapeDtypeStruct((bsz, l, D_I), BF16),
                   jax.ShapeDtypeStruct((bsz, l, R_KV), BF16)],
        compiler_params=_cparams("parallel", "parallel"),
        name="dsa_prep",
    )(pa, gq.reshape(1, R_Q), gkv.reshape(1, R_KV), wuq, wuk, wqi)


def _fold8(x, op, ways=1):
    n = x.shape[0] // 8
    ways = min(ways, n)
    acc = [x[g * 8:(g + 1) * 8] for g in range(ways)]
    for g in range(ways, n):
        acc[g % ways] = op(acc[g % ways], x[g * 8:(g + 1) * 8])
    while len(acc) > 1:
        acc = [op(acc[i], acc[i + 1]) for i in range(0, len(acc), 2)]
    return acc[0]


def _dsa_attn_kernel(qlat_ref, qidx_ref, widx_ref, kidx_ref, ckv_ref, ckvt_ref, wuvt_ref, bias_ref, o_ref,
                     key_ref, ls_ref, acc_ref, sa_ref, sb_ref, lga_ref, lgb_ref, pa_ref, pb_ref, *, topk):
    qi = pl.program_id(1)
    nch = (qi * Q_BLOCK) // KEY_CHUNK + 1
    krow = lax.broadcasted_iota(jnp.int32, (KEY_CHUNK, Q_BLOCK), 0)
    qlane = lax.broadcasted_iota(jnp.int32, (KEY_CHUNK, Q_BLOCK), 1)
    t_abs = qi * Q_BLOCK + qlane
    w = widx_ref[0, 0]

    npair = (nch + 1) // 2
    last_chunk = kidx_ref.shape[1] // KEY_CHUNK - 1

    def chunk_start(c):
        return pl.multiple_of(jnp.minimum(c, last_chunk) * KEY_CHUNK, KEY_CHUNK)

    def score_mm(c, s_ref):
        kc = kidx_ref[0, pl.ds(chunk_start(c), KEY_CHUNK), :]
        s_ref[...] = lax.dot_general(kc, qidx_ref[0, 0], NT_DIMS, preferred_element_type=F32)

    def score_keys(c, s_ref):
        acc = jnp.zeros((KEY_CHUNK, Q_BLOCK), F32)
        for h in range(H_I):
            acc = acc + w[h:h + 1] * jnp.maximum(s_ref[:, h * Q_BLOCK:(h + 1) * Q_BLOCK], 0.0)
        acc = jnp.where(c * KEY_CHUNK + krow <= t_abs, acc, -jnp.inf)
        bits = lax.bitcast_convert_type(acc, jnp.int32)
        key_ref[c] = bits ^ ((bits >> 31) & 0x7FFFFFFF)

    def score_body(j, carry):
        c = 2 * j
        score_mm(c + 1, sb_ref)
        score_keys(c, sa_ref)
        score_mm(c + 2, sa_ref)
        score_keys(c + 1, sb_ref)
        return carry

    score_mm(0, sa_ref)
    lax.fori_loop(0, npair, score_body, 0)

    def count_ge(cand):
        def body(c, cnt):
            return (cnt + _fold8(jnp.where(key_ref[2 * c] >= cand, 1.0, 0.0), jnp.add, 4)
                    + _fold8(jnp.where(key_ref[2 * c + 1] >= cand, 1.0, 0.0), jnp.add, 4))
        cnt = lax.fori_loop(0, npair, body, jnp.zeros((8, Q_BLOCK), F32))
        for shift in (4, 2, 1):
            cnt = cnt + pltpu.roll(cnt, shift, 0)
        return cnt[0:1]

    kf = float(topk)
    cnt0 = count_ge(jnp.zeros((1, Q_BLOCK), jnp.int32))
    thr = jnp.where(cnt0 >= kf, 0, INT_MIN).astype(jnp.int32)
    cnt_thr = jnp.where(cnt0 >= kf, cnt0, ((nch + 1) // 2 * (2 * KEY_CHUNK)).astype(F32))
    few = (qi * Q_BLOCK + lax.broadcasted_iota(jnp.int32, (1, Q_BLOCK), 1)) < topk

    def unsettled(cnt_thr):
        return jnp.max(jnp.where(few | (cnt_thr == kf), 0.0, 1.0)) > 0.5

    def group_cond(st):
        return (st[0] < 8) & st[3]

    def group_body(st):
        g, thr, cnt_thr, _ = st
        for k in range(4):
            i = 4 * g + k
            bit = jnp.where(i < 31, jnp.left_shift(jnp.int32(1), jnp.maximum(30 - i, 0)), 0)
            cand = thr + bit
            cnt = count_ge(cand)
            take = cnt >= kf
            thr = jnp.where(take, cand, thr)
            cnt_thr = jnp.where(take, cnt, cnt_thr)
        return g + 1, thr, cnt_thr, unsettled(cnt_thr)

    thr = lax.while_loop(group_cond, group_body, (jnp.int32(0), thr, cnt_thr, unsettled(cnt_thr)))[1]

    odd = (qi % 2) == 1
    hq = H_A * Q_BLOCK

    def logits_mm(c, lg_ref):
        kv = ckv_ref[0, pl.ds(chunk_start(c), KEY_CHUNK), :]
        lg_ref[...] = lax.dot_general(kv, qlat_ref[0, 0], NT_DIMS, preferred_element_type=F32)

    def chunk_mask(c):
        sel = (key_ref[c] >= thr) & (c * KEY_CHUNK + krow <= t_abs)
        idx = jnp.where(c == nch - 1, jnp.where(odd, 0, 1),
                        jnp.where((c == nch - 2) & jnp.logical_not(odd), 2, 3))
        return sel, idx

    def chunk_max(c, lg_ref, mx):
        sel, idx = chunk_mask(c)
        parts = []
        for h in range(H_A):
            cs = slice(h * Q_BLOCK, (h + 1) * Q_BLOCK)
            parts.append(_fold8(jnp.where(sel, lg_ref[:, cs] + bias_ref[idx, h], -1e30), jnp.maximum))
        return jnp.maximum(mx, jnp.concatenate(parts, axis=1))

    def max_body(j, mx):
        c = 2 * j
        logits_mm(c + 1, lgb_ref)
        mx = chunk_max(c, lga_ref, mx)
        logits_mm(c + 2, lga_ref)
        return chunk_max(c + 1, lgb_ref, mx)

    logits_mm(0, lga_ref)
    mx = lax.fori_loop(0, npair, max_body, jnp.full((8, hq), -1e30, F32))
    m = jnp.max(mx, axis=0, keepdims=True)
    ls_ref[...] = jnp.zeros(ls_ref.shape, F32)
    acc_ref[...] = jnp.zeros(acc_ref.shape, F32)
    pb_ref[...] = jnp.zeros(pb_ref.shape, BF16)

    def chunk_probs(c, lg_ref, p_ref):
        sel, idx = chunk_mask(c)
        for h in range(H_A):
            cs = slice(h * Q_BLOCK, (h + 1) * Q_BLOCK)
            p = jnp.where(sel, jnp.exp(lg_ref[:, cs] + bias_ref[idx, h] - m[:, cs]), 0.0)
            ls_ref[:, cs] += _fold8(p, jnp.add)
            p_ref[:, cs] = p.astype(BF16)

    def values_mm(c, p_ref):
        acc_ref[...] += jnp.dot(ckvt_ref[0, :, pl.ds(chunk_start(c), KEY_CHUNK)], p_ref[...],
                                preferred_element_type=F32)

    def attn_body(j, carry):
        c = 2 * j
        logits_mm(c + 1, lgb_ref)
        chunk_probs(c, lga_ref, pa_ref)
        values_mm(jnp.maximum(c - 1, 0), pb_ref)
        logits_mm(c + 2, lga_ref)
        chunk_probs(c + 1, lgb_ref, pb_ref)
        values_mm(c, pa_ref)
        return carry

    logits_mm(0, lga_ref)
    lax.fori_loop(0, npair, attn_body, 0)
    values_mm(2 * npair - 1, pb_ref)
    lsum = jnp.sum(ls_ref[...], axis=0, keepdims=True)

    for h in range(H_A):
        cs = slice(h * Q_BLOCK, (h + 1) * Q_BLOCK)
        o_lat = (acc_ref[:, cs] / lsum[:, cs]).astype(BF16)
        out = jnp.dot(wuvt_ref[h], o_lat, preferred_element_type=F32)
        o_ref[0, h * DH_A:(h + 1) * DH_A, :] = out.astype(o_ref.dtype)


def _dsa_attn(qlat, qidx, widx, kidx, ckv, wuv, bias):
    bsz, l, _ = ckv.shape
    nb = l // Q_BLOCK
    topk = min(TOPK_MAX, l // 4)
    nchunks = 2 * ((l + 2 * KEY_CHUNK - 1) // (2 * KEY_CHUNK))
    hq = H_A * Q_BLOCK
    widx_t = widx.reshape(bsz, nb, Q_BLOCK, H_I).transpose(0, 1, 3, 2)
    ckv_t = ckv.transpose(0, 2, 1)
    wuv_t = wuv.transpose(0, 2, 1)
    return pl.pallas_call(
        functools.partial(_dsa_attn_kernel, topk=topk),
        grid=(bsz, nb),
        in_specs=[pl.BlockSpec((1, 1, hq, R_KV), lambda b, i: (b, i, 0, 0)),
                  pl.BlockSpec((1, 1, H_I * Q_BLOCK, D_I), lambda b, i: (b, i, 0, 0)),
                  pl.BlockSpec((1, 1, H_I, Q_BLOCK), lambda b, i: (b, i, 0, 0)),
                  pl.BlockSpec((1, l, D_I), lambda b, i: (b, 0, 0)),
                  pl.BlockSpec((1, l, R_KV), lambda b, i: (b, 0, 0)),
                  pl.BlockSpec((1, R_KV, l), lambda b, i: (b, 0, 0)),
                  pl.BlockSpec((H_A, DH_A, R_KV), lambda b, i: (0, 0, 0)),
                  pl.BlockSpec((4, H_A, KEY_CHUNK, Q_BLOCK), lambda b, i: (0, 0, 0, 0))],
        out_specs=pl.BlockSpec((1, H_A * DH_A, Q_BLOCK), lambda b, i: (b, 0, i)),
        out_shape=jax.ShapeDtypeStruct((bsz, H_A * DH_A, l), BF16),
        scratch_shapes=[pltpu.VMEM((nchunks, KEY_CHUNK, Q_BLOCK), jnp.int32),
                        pltpu.VMEM((8, hq), F32), pltpu.VMEM((R_KV, hq), F32),
                        pltpu.VMEM((KEY_CHUNK, H_I * Q_BLOCK), F32), pltpu.VMEM((KEY_CHUNK, H_I * Q_BLOCK), F32),
                        pltpu.VMEM((KEY_CHUNK, hq), F32), pltpu.VMEM((KEY_CHUNK, hq), F32),
                        pltpu.VMEM((KEY_CHUNK, hq), BF16), pltpu.VMEM((KEY_CHUNK, hq), BF16)],
        compiler_params=_cparams("parallel", "arbitrary"),
        name="dsa_attn",
    )(qlat, qidx, widx_t, kidx, ckv, ckv_t, wuv_t, bias)


def _seg_sum(x, bd):
    hi = x.astype(BF16)
    lo = (x - hi.astype(F32)).astype(BF16)
    outs = []
    for p in range(x.shape[1] // LANES):
        sl = slice(p * LANES, (p + 1) * LANES)
        outs.append(jnp.dot(hi[:, sl], bd, preferred_element_type=F32)
                    + jnp.dot(lo[:, sl], bd, preferred_element_type=F32))
    return outs[0] if len(outs) == 1 else jnp.concatenate(outs, axis=1)


def _rwkv_pre_kernel(pb_ref, mu_ref, w0_ref, a0_ref, kk_ref, ka_ref, rk_ref, wup_ref, aup_ref, gup_ref,
                     bd_ref, tril_ref, ab_ref, bb_ref, kb_ref, rb_ref, v_ref, p_ref, g_ref, bonus_ref,
                     carry_ref):
    @pl.when(pl.program_id(1) == 0)
    def _():
        carry_ref[...] = jnp.zeros_like(carry_ref)

    x = pb_ref[0]
    tm = x.shape[0]
    rows = lax.broadcasted_iota(jnp.int32, x.shape, 0)
    prev = jnp.where(rows == 0, carry_ref[...], pltpu.roll(x, 1, 0))
    carry_ref[...] = x[tm - 1:tm]
    ps = x + mu_ref[...] * (prev - x)
    r = ps[:, :W_B]
    k = ps[:, W_B:2 * W_B]
    v = ps[:, 2 * W_B:3 * W_B]
    o = 3 * W_B
    wd = ps[:, o:o + LORA_W]
    ad = ps[:, o + LORA_W:o + LORA_W + LORA_A]
    gd = ps[:, o + LORA_W + LORA_A:]
    hp = lax.Precision.HIGHEST
    z = w0_ref[...] + jnp.dot(jnp.tanh(wd), wup_ref[...], precision=hp, preferred_element_type=F32)
    u = -z
    w = -(jnp.maximum(u, 0.0) + jnp.log(1.0 + jnp.exp(-jnp.abs(u)))) - 0.5
    logd = -jnp.exp(w)
    logp = jnp.dot(tril_ref[...], logd, precision=hp, preferred_element_type=F32)
    pinv = jnp.exp(-logp)
    a = _sigmoid(a0_ref[...] + jnp.dot(ad, aup_ref[...], precision=hp, preferred_element_type=F32))
    g_ref[0] = jnp.dot(_sigmoid(gd).astype(BF16), gup_ref[...], preferred_element_type=F32)
    bd = bd_ref[...]
    kk = k * kk_ref[...]
    kk = kk * lax.rsqrt(jnp.maximum(_seg_sum(kk * kk, bd), 1e-24))
    k2 = k * (1.0 + (a - 1.0) * ka_ref[...])
    p = jnp.exp(logp)
    ab_ref[0] = -kk * jnp.exp(logp - logd)
    bb_ref[0] = kk * a * pinv
    kb_ref[0] = k2 * pinv
    rb_ref[0] = r * p
    v_ref[0] = v
    p_ref[0] = p
    bonus_ref[0] = _seg_sum(r * k2 * rk_ref[...], bd) * v


def _rwkv_pre(pb, mu, w0, a0, k_k, k_a, r_k, w_up, a_up, g_up, bd, tm=128):
    bsz, l, _ = pb.shape
    tm = min(tm, l)
    assert tm % RW_CHUNK == 0
    tril = jnp.kron(jnp.eye(tm // RW_CHUNK, dtype=F32), jnp.tril(jnp.ones((RW_CHUNK, RW_CHUNK), F32)))
    full = lambda shp: pl.BlockSpec(shp, lambda b, i: (0,) * len(shp))
    row = pl.BlockSpec((1, tm, W_B), lambda b, i: (b, i, 0))
    vec = full((1, W_B))
    return pl.pallas_call(
        _rwkv_pre_kernel,
        grid=(bsz, l // tm),
        in_specs=[pl.BlockSpec((1, tm, B_PAD), lambda b, i: (b, i, 0)), full((1, B_PAD)),
                  vec, vec, vec, vec, vec,
                  full((LORA_W, W_B)), full((LORA_A, W_B)), full((LORA_G_PAD, W_B)), full((LANES, LANES)),
                  full((tm, tm))],
        out_specs=[row] * 8,
        out_shape=[jax.ShapeDtypeStruct((bsz, l, W_B), F32)] * 8,
        scratch_shapes=[pltpu.VMEM((1, B_PAD), F32)],
        compiler_params=_cparams("parallel", "arbitrary"),
        name="rwkv_pre",
    )(pb, mu, w0.reshape(1, W_B), a0.reshape(1, W_B), k_k.reshape(1, W_B), k_a.reshape(1, W_B),
      r_k.reshape(1, W_B), w_up, a_up, g_up, bd, tril)


def _mm(a, b, dims, passes):
    dg = lambda x, y: lax.dot_general(x, y, dims, preferred_element_type=F32)
    ah, bh = a.astype(BF16), b.astype(BF16)
    if passes == 1:
        return dg(ah, bh)
    al = (a - ah.astype(F32)).astype(BF16)
    bl = (b - bh.astype(F32)).astype(BF16)
    return dg(ah, bh) + (dg(ah, bl) + dg(al, bh))


def _rwkv_chunk_kernel(ab_ref, bb_ref, kb_ref, rb_ref, vt_ref, pe_ref, yt_ref, s_ref, *, hb):
    @pl.when(pl.program_id(2) == 0)
    def _():
        s_ref[...] = jnp.zeros_like(s_ref)

    c = RW_CHUNK
    ri = lax.broadcasted_iota(jnp.int32, (c, c), 0)
    ci = lax.broadcasted_iota(jnp.int32, (c, c), 1)
    strict, incl = ci < ri, ci <= ri
    eye = jnp.where(ri == ci, 1.0, 0.0)
    nt = (((2,), (2,)), ((0,), (0,)))
    nn = (((2,), (1,)), ((0,), (0,)))
    ab, bb, kb, rb = ab_ref[0], bb_ref[0], kb_ref[0], rb_ref[0]
    vt = vt_ref[0, :, 0]
    s0 = s_ref[...]
    ar = jnp.concatenate([ab, rb], axis=1)
    gb = _mm(ar, bb, nt, RW_PASSES)
    gk = _mm(ar, kb, nt, RW_PASSES)
    m_ab = jnp.where(strict, gb[:, :c], 0.0)
    n_rb = jnp.where(incl, gb[:, c:], 0.0)
    m_ak = jnp.where(strict, gk[:, :c], 0.0)
    n_rk = jnp.where(incl, gk[:, c:], 0.0)
    tinv = eye + m_ab
    pw = m_ab
    for _ in range(5):
        pw = _mm(pw, pw, nn, RW_PASSES)
        tinv = tinv + _mm(tinv, pw, nn, RW_PASSES)
    za = _mm(s0, ab, nt, RW_PASSES)
    zr = _mm(s0, rb, nt, RW_PASSES)
    ut = _mm(za + _mm(vt, m_ak, nt, RW_PASSES), tinv, nt, RW_PASSES)
    yt_ref[0, :, 0] = zr + _mm(ut, n_rb, nt, RW_PASSES) + _mm(vt, n_rk, nt, RW_PASSES)
    s_ref[...] = (s0 + _mm(ut, bb, nn, RW_PASSES) + _mm(vt, kb, nn, RW_PASSES)) * pe_ref[0, :, 0]


def _rwkv_chunks(ab, bb, kb, rb, vt, pe, hb=12):
    bsz, nh, l, n = ab.shape
    nt = l // RW_CHUNK
    row = pl.BlockSpec((1, hb, RW_CHUNK, n), lambda b, h, i: (b, h, i, 0))
    tile = pl.BlockSpec((1, hb, 1, n, RW_CHUNK), lambda b, h, i: (b, h, i, 0, 0))
    return pl.pallas_call(
        functools.partial(_rwkv_chunk_kernel, hb=hb),
        grid=(bsz, nh // hb, nt),
        in_specs=[row] * 4 + [tile, pl.BlockSpec((1, hb, 1, 1, n), lambda b, h, i: (b, h, i, 0, 0))],
        out_specs=tile,
        out_shape=jax.ShapeDtypeStruct((bsz, nh, nt, n, RW_CHUNK), F32),
        scratch_shapes=[pltpu.VMEM((hb, n, n), F32)],
        compiler_params=_cparams("parallel", "parallel", "arbitrary"),
        name="rwkv_chunks",
    )(ab, bb, kb, rb, vt, pe)


def _rwkv_post_kernel(y_ref, bonus_ref, g_ref, lw_ref, lb_ref, bd_ref, o_ref):
    bd = bd_ref[...]
    y = y_ref[0]
    mean = _seg_sum(y, bd) * (1.0 / N_B)
    c = y - mean
    var = _seg_sum(c * c, bd) * (1.0 / N_B)
    yn = c * lax.rsqrt(var + GN_EPS) * lw_ref[...] + lb_ref[...]
    o_ref[0] = ((yn + bonus_ref[0]) * g_ref[0]).astype(o_ref.dtype)


def _rwkv_post(y, bonus, g, lnx_w, lnx_b, bd, tm=256):
    bsz, l, _ = y.shape
    tm = min(tm, l)
    row = pl.BlockSpec((1, tm, W_B), lambda b, i: (b, i, 0))
    vec = pl.BlockSpec((1, W_B), lambda b, i: (0, 0))
    return pl.pallas_call(
        _rwkv_post_kernel,
        grid=(bsz, l // tm),
        in_specs=[row, row, row, vec, vec, pl.BlockSpec((LANES, LANES), lambda b, i: (0, 0))],
        out_specs=row,
        out_shape=jax.ShapeDtypeStruct((bsz, l, W_B), BF16),
        compiler_params=_cparams("parallel", "parallel"),
        name="rwkv_post",
    )(y, bonus, g, lnx_w.reshape(1, W_B), lnx_b.reshape(1, W_B), bd)


def _rwkv_mixer(pb, mu, w0, w_up, a0, a_up, g_up, k_k, k_a, r_k, lnx_w, lnx_b):
    bsz, l, _ = pb.shape
    nt = l // RW_CHUNK
    eye2 = jnp.kron(jnp.eye(2, dtype=F32), jnp.ones((N_B, N_B), F32)).astype(BF16)
    ab, bb, kb, rb, v, p, g, bonus = _rwkv_pre(pb, mu, w0, a0, k_k, k_a, r_k, w_up, a_up, g_up, eye2)
    heads = lambda a: a.reshape(bsz, l, H_B, N_B).transpose(0, 2, 1, 3)
    vt = v.reshape(bsz, nt, RW_CHUNK, H_B, N_B).transpose(0, 3, 1, 4, 2)
    pe = p[:, RW_CHUNK - 1::RW_CHUNK].reshape(bsz, nt, H_B, 1, N_B).transpose(0, 2, 1, 3, 4)
    yt = _rwkv_chunks(heads(ab), heads(bb), heads(kb), heads(rb), vt, pe)
    y = yt.transpose(0, 2, 4, 1, 3).reshape(bsz, l, W_B)
    return _rwkv_post(y, bonus, g, lnx_w, lnx_b, eye2)


def _hgrn_kernel(q_ref, f_ref, i_ref, g_ref, lb_ref, on_ref, tril_ref, o_ref, st_ref, *, layer, hb, tt):
    @pl.when(pl.program_id(2) == 0)
    def _():
        st_ref[...] = jnp.zeros_like(st_ref)

    x = lb_ref[...]
    e = jnp.exp(x - jnp.max(x, axis=0, keepdims=True))
    sm = e / jnp.sum(e, axis=0, keepdims=True)
    cs = sm[0:1]
    for i in range(1, layer + 1):
        cs = cs + sm[i:i + 1]
    lb_all = cs - sm[0:1]
    rowi = lax.broadcasted_iota(jnp.int32, (SUB, LANES), 0)
    lanei = lax.broadcasted_iota(jnp.int32, (SUB, LANES), 1)
    prep = []
    for hh in range(hb):
        sl = slice(hh * LANES, (hh + 1) * LANES)
        lb = lb_all[:, sl]
        qraw = q_ref[0][:, sl]
        ff = f_ref[0][:, sl]
        q = qraw * _sigmoid(qraw)
        logsig = jnp.minimum(ff, 0.0) - jnp.log(1.0 + jnp.exp(-jnp.abs(ff)))
        la = jnp.log(lb)
        lbv = jnp.log(1.0 - lb) + logsig
        mx = jnp.maximum(la, lbv)
        logf = mx + jnp.log(jnp.exp(la - mx) + jnp.exp(lbv - mx))
        k = (1.0 - lb) * _sigmoid(-ff)
        bcum = jnp.dot(tril_ref[...], logf, precision=lax.Precision.HIGHEST, preferred_element_type=F32)
        prep.append((q, k, i_ref[0][:, sl], bcum))
    sts = [st_ref[hh] for hh in range(hb)]
    outs = [[] for _ in range(hb)]
    for n in range(tt // SUB):
        rs = slice(n * SUB, (n + 1) * SUB)
        for hh in range(hb):
            q, k, v, bcum = prep[hh]
            st = sts[hh]
            bq, qn, kn, vn = bcum[rs], q[rs], k[rs], v[rs]
            bend = bq[SUB - 1:SUB]
            qd = (qn * jnp.exp(bq)).astype(BF16)
            o = lax.dot_general(qd, st.astype(BF16), NT_DIMS, preferred_element_type=F32)
            a = jnp.zeros((SUB, LANES), F32)
            for s in range(SUB):
                wgt = kn[s:s + 1] * jnp.exp(jnp.minimum(bq - bq[s:s + 1], 0.0))
                a = jnp.where(lanei == s, jnp.sum(qn * wgt, axis=1, keepdims=True), a)
            a = jnp.where(lanei <= rowi, a, 0.0)[:, :SUB]
            o = o + jnp.dot(a.astype(BF16), vn.astype(BF16), preferred_element_type=F32)
            kd = (kn * jnp.exp(bend - bq)).astype(BF16)
            sts[hh] = st * jnp.exp(bend) + lax.dot_general(vn.astype(BF16), kd, TN_DIMS,
                                                           preferred_element_type=F32)
            outs[hh].append(o)
    for hh in range(hb):
        sl = slice(hh * LANES, (hh + 1) * LANES)
        st_ref[hh] = sts[hh]
        gg = g_ref[0][:, sl]
        o = jnp.concatenate(outs[hh], axis=0)
        o = _rms(o) * on_ref[:, sl] * (gg * _sigmoid(gg))
        o_ref[0, :, sl] = o.astype(o_ref.dtype)


def _hgrn_mixer(pc, hgrn_lb, onorm, layer, hb=12, tt=128):
    bsz, l, _ = pc.shape
    depth = hgrn_lb.shape[0]
    tt = min(tt, l)
    ng = H_C // hb
    wblk = hb * LANES
    tril = jnp.kron(jnp.eye(tt // SUB, dtype=F32), jnp.tril(jnp.ones((SUB, SUB), F32)))
    col = lambda j: pl.BlockSpec((1, tt, wblk), lambda b, h, i, j=j: (b, i, j * ng + h))
    return pl.pallas_call(
        functools.partial(_hgrn_kernel, layer=layer, hb=hb, tt=tt),
        grid=(bsz, ng, l // tt),
        in_specs=[col(0), col(1), col(2), col(3),
                  pl.BlockSpec((depth, wblk), lambda b, h, i: (0, h)),
                  pl.BlockSpec((1, wblk), lambda b, h, i: (0, h)),
                  pl.BlockSpec((tt, tt), lambda b, h, i: (0, 0))],
        out_specs=pl.BlockSpec((1, tt, wblk), lambda b, h, i: (b, i, h)),
        out_shape=jax.ShapeDtypeStruct((bsz, l, W_C), BF16),
        scratch_shapes=[pltpu.VMEM((hb, DV_C, DK_C), F32)],
        compiler_params=_cparams("parallel", "parallel", "arbitrary"),
        name="hgrn2",
    )(pc, pc, pc, pc, hgrn_lb, onorm.reshape(1, W_C), tril)


def _pack_w_in(w):
    d = w.shape[0]
    a_cols = R_Q + R_KV + D_I + H_I
    b_cols = 3 * W_B + LORA_W + LORA_A + LORA_G
    wa, wb, wc = w[:, :a_cols], w[:, a_cols:a_cols + b_cols], w[:, a_cols + b_cols:]
    z = lambda n: jnp.zeros((d, n), w.dtype)
    o = R_Q + R_KV
    wa = jnp.concatenate([wa[:, :o + D_I], z(LANES - D_I), wa[:, o + D_I:], z(LANES - H_I)], axis=1)
    wb = jnp.concatenate([wb, z(LORA_G_PAD - LORA_G)], axis=1)
    return wa.astype(BF16), wb.astype(BF16), wc.astype(BF16)


def kernel(x, c, rel_bias, hgrn_lb, ada_w, ada_b, norm_g, w_in, w_out, mla_q_norm, mla_kv_norm, w_uq, w_uk, w_uv, w_qidx, rwkv_mu, rwkv_w0, rwkv_w_up, rwkv_a0, rwkv_a_up, rwkv_g_up, rwkv_k_k, rwkv_k_a, rwkv_r_k, rwkv_lnx_w, rwkv_lnx_b, hgrn_onorm, w_ff1, w_ff2):
    bsz, l, d = x.shape
    depth = ada_w.shape[0]
    m = bsz * l
    mod = _ada_mod(c, ada_w, ada_b)
    mods = [[mod[i, :, None, j * d:(j + 1) * d] for j in range(6)] for i in range(depth)]
    bias = _bias_tiles(rel_bias)
    h = _norm_mod(x, norm_g[0, 0], mods[0][1], mods[0][0])
    for i in range(depth):
        sh_m, sc_m, g_m, sh_f, sc_f, g_f = mods[i]
        wa, wb, wc = _pack_w_in(w_in[i])
        h2d = h.reshape(m, d)
        pa = _matmul(h2d, wa, F32, 512, A_PAD, 4096, name="in_proj_a").reshape(bsz, l, -1)
        pb = _matmul(h2d, wb, F32, 1024, 768, 4096, name="in_proj_b").reshape(bsz, l, -1)
        pc = _matmul(h2d, wc, F32, 1024, 1024, 4096, name="in_proj_c").reshape(bsz, l, -1)
        wuq = w_uq[i].reshape(R_Q, H_A * DH_A).astype(BF16)
        wqi = jnp.transpose(w_qidx[i], (1, 0, 2)).astype(BF16)
        qlat, qidx, widx, kidx, ckv = _dsa_prep(pa, mla_q_norm[i], mla_kv_norm[i], wuq,
                                                w_uk[i].astype(BF16), wqi)
        y_a = _dsa_attn(qlat, qidx, widx, kidx, ckv, w_uv[i].astype(BF16), bias).transpose(0, 2, 1)
        mu = jnp.concatenate([rwkv_mu[i], jnp.zeros((LORA_G_PAD - LORA_G,), F32)]).reshape(1, B_PAD)
        gup = jnp.concatenate([rwkv_g_up[i], jnp.zeros((LORA_G_PAD - LORA_G, W_B), F32)]).astype(BF16)
        y_b = _rwkv_mixer(pb, mu, rwkv_w0[i], rwkv_w_up[i], rwkv_a0[i], rwkv_a_up[i], gup,
                          rwkv_k_k[i], rwkv_k_a[i], rwkv_r_k[i], rwkv_lnx_w[i], rwkv_lnx_b[i])
        y_c = _hgrn_mixer(pc, hgrn_lb, hgrn_onorm[i], i)
        ycat = jnp.concatenate([y_a, y_b, y_c], axis=-1).reshape(m, -1)
        y = _matmul(ycat, w_out[i].astype(BF16), F32, 1024, 1024, 4096, name="out_proj").reshape(bsz, l, d)
        x, h = _resid_norm(x, y, g_m, norm_g[i, 1], (norm_g[i, 2], sc_f, sh_f))
        u = _matmul(h.reshape(m, d), w_ff1[i].astype(BF16), BF16, 1024, 1024, 4096, relu2=True, name="ffn_up")
        y = _matmul(u, w_ff2[i].astype(BF16), F32, 1024, 1024, 4096, name="ffn_down").reshape(bsz, l, d)
        nxt = None if i == depth - 1 else (norm_g[i + 1, 0], mods[i + 1][1], mods[i + 1][0])
        x, h = _resid_norm(x, y, g_f, norm_g[i, 3], nxt)
    return x
```

```python
import functools
import math

import jax
import jax.numpy as jnp
from jax import lax
from jax.experimental import pallas as pl
from jax.experimental.pallas import tpu as pltpu

H_A, DH_A, R_Q, R_KV, H_I, D_I = 8, 128, 768, 256, 16, 64
TOPK_MAX, Q_BLOCK, N_BUCKETS, MAX_DIST = 256, 128, 32, 128
H_B, N_B, LORA_W, LORA_A, LORA_G = 24, 64, 128, 128, 480
W_B = H_B * N_B
H_C, DK_C, DV_C = 12, 128, 128
W_C = H_C * DK_C
EPS = 1e-6
GN_EPS = 64e-5

LANES = 128
KEY_CHUNK = 256
SUB = 8
RW_CHUNK = 64
RW_PASSES = 1
LORA_G_PAD = 512
A_PAD = 1280
B_PAD = 3 * W_B + LORA_W + LORA_A + LORA_G_PAD
VMEM_LIMIT = 56 * 1024 * 1024

F32 = jnp.float32
BF16 = jnp.bfloat16
NT_DIMS = (((1,), (1,)), ((), ()))
TN_DIMS = (((0,), (0,)), ((), ()))
INT_MIN = -2 ** 31
LOG2E = math.log2(math.e)


def _cparams(*sem):
    return pltpu.CompilerParams(dimension_semantics=sem, vmem_limit_bytes=VMEM_LIMIT)


def _sigmoid(x):
    return 1.0 / (1.0 + jnp.exp(-x))


def _rms(x, eps=EPS):
    return x * lax.rsqrt(jnp.mean(x * x, axis=-1, keepdims=True) + eps)


def _mm_kernel(a_ref, b_ref, o_ref, acc_ref, *, nk, relu2):
    k = pl.program_id(2)

    @pl.when(k == 0)
    def _():
        acc_ref[...] = jnp.zeros_like(acc_ref)

    acc_ref[...] += jnp.dot(a_ref[...], b_ref[...], preferred_element_type=F32)

    @pl.when(k == nk - 1)
    def _():
        r = acc_ref[...]
        if relu2:
            r = jnp.square(jnp.maximum(r, 0.0))
        o_ref[...] = r.astype(o_ref.dtype)


def _mm_full_k_kernel(a_ref, b_ref, o_ref, *, relu2):
    r = jnp.dot(a_ref[...], b_ref[...], preferred_element_type=F32)
    if relu2:
        r = jnp.square(jnp.maximum(r, 0.0))
    o_ref[...] = r.astype(o_ref.dtype)


def _matmul(a, b, out_dtype, tm, tn, tk, relu2=False, name="matmul"):
    m, kdim = a.shape
    _, n = b.shape
    tm, tn, tk = min(tm, m), min(tn, n), min(tk, kdim)
    assert m % tm == 0 and n % tn == 0 and kdim % tk == 0, (a.shape, b.shape, tm, tn, tk)
    nk = kdim // tk
    if nk == 1:
        return pl.pallas_call(
            functools.partial(_mm_full_k_kernel, relu2=relu2),
            grid=(m // tm, n // tn),
            in_specs=[pl.BlockSpec((tm, kdim), lambda i, j: (i, 0)),
                      pl.BlockSpec((kdim, tn), lambda i, j: (0, j))],
            out_specs=pl.BlockSpec((tm, tn), lambda i, j: (i, j)),
            out_shape=jax.ShapeDtypeStruct((m, n), out_dtype),
            compiler_params=_cparams("parallel", "parallel"),
            name=name,
        )(a, b)
    return pl.pallas_call(
        functools.partial(_mm_kernel, nk=nk, relu2=relu2),
        grid=(m // tm, n // tn, nk),
        in_specs=[pl.BlockSpec((tm, tk), lambda i, j, k: (i, k)),
                  pl.BlockSpec((tk, tn), lambda i, j, k: (k, j))],
        out_specs=pl.BlockSpec((tm, tn), lambda i, j, k: (i, j)),
        out_shape=jax.ShapeDtypeStruct((m, n), out_dtype),
        scratch_shapes=[pltpu.VMEM((tm, tn), F32)],
        compiler_params=_cparams("parallel", "parallel", "arbitrary"),
        name=name,
    )(a, b)


def _ada_kernel(c_ref, w_ref, b_ref, o_ref):
    k = pl.program_id(2)

    @pl.when(k == 0)
    def _():
        o_ref[0] = jnp.broadcast_to(b_ref[0], o_ref.shape[1:])

    c = c_ref[...]
    ca = (c * _sigmoid(c)).astype(BF16)
    o_ref[0] += jnp.dot(ca, w_ref[0].astype(BF16), preferred_element_type=F32)


def _ada_mod(c, ada_w, ada_b):
    nl, d, n = ada_w.shape
    bsz = c.shape[0]
    rows = 8
    cp = jnp.zeros((rows, d), F32).at[:bsz].set(c)
    tn, tk = min(2048, n), min(1024, d)
    out = pl.pallas_call(
        _ada_kernel,
        grid=(nl, n // tn, d // tk),
        in_specs=[pl.BlockSpec((rows, tk), lambda l, j, k: (0, k)),
                  pl.BlockSpec((1, tk, tn), lambda l, j, k: (l, k, j)),
                  pl.BlockSpec((1, 1, tn), lambda l, j, k: (l, 0, j))],
        out_specs=pl.BlockSpec((1, rows, tn), lambda l, j, k: (l, 0, j)),
        out_shape=jax.ShapeDtypeStruct((nl, rows, n), F32),
        compiler_params=_cparams("parallel", "parallel", "arbitrary"),
        name="ada_mod",
    )(cp, ada_w, ada_b.reshape(nl, 1, n))
    return out[:, :bsz]


def _norm_mod_kernel(x_ref, g_ref, sc_ref, sh_ref, h_ref):
    y = _rms(x_ref[0]) * g_ref[...]
    h_ref[0] = (y * (1.0 + sc_ref[0]) + sh_ref[0]).astype(h_ref.dtype)


def _norm_mod(x, g, sc, sh, tm=256):
    bsz, l, d = x.shape
    tm = min(tm, l)
    row = pl.BlockSpec((1, tm, d), lambda b, i: (b, i, 0))
    vec = pl.BlockSpec((1, d), lambda b, i: (0, 0))
    bvec = pl.BlockSpec((1, 1, d), lambda b, i: (b, 0, 0))
    return pl.pallas_call(
        _norm_mod_kernel,
        grid=(bsz, l // tm),
        in_specs=[row, vec, bvec, bvec],
        out_specs=row,
        out_shape=jax.ShapeDtypeStruct((bsz, l, d), BF16),
        compiler_params=_cparams("parallel", "parallel"),
        name="norm_mod",
    )(x, g.reshape(1, d), sc, sh)


def _resid_kernel(x_ref, y_ref, gate_ref, g1_ref, g2_ref, sc_ref, sh_ref, xo_ref, h_ref):
    xn = x_ref[0] + gate_ref[0] * (_rms(y_ref[0]) * g1_ref[...])
    xo_ref[0] = xn
    hn = _rms(xn) * g2_ref[...]
    h_ref[0] = (hn * (1.0 + sc_ref[0]) + sh_ref[0]).astype(h_ref.dtype)


def _resid_last_kernel(x_ref, y_ref, gate_ref, g1_ref, xo_ref):
    xo_ref[0] = x_ref[0] + gate_ref[0] * (_rms(y_ref[0]) * g1_ref[...])


def _resid_norm(x, y, gate, g1, nxt=None, tm=128):
    bsz, l, d = x.shape
    tm = min(tm, l)
    row = pl.BlockSpec((1, tm, d), lambda b, i: (b, i, 0))
    vec = pl.BlockSpec((1, d), lambda b, i: (0, 0))
    bvec = pl.BlockSpec((1, 1, d), lambda b, i: (b, 0, 0))
    if nxt is None:
        return pl.pallas_call(
            _resid_last_kernel,
            grid=(bsz, l // tm),
            in_specs=[row, row, bvec, vec],
            out_specs=row,
            out_shape=jax.ShapeDtypeStruct((bsz, l, d), F32),
            compiler_params=_cparams("parallel", "parallel"),
            name="resid_last",
        )(x, y, gate, g1.reshape(1, d)), None
    g2, sc, sh = nxt
    return pl.pallas_call(
        _resid_kernel,
        grid=(bsz, l // tm),
        in_specs=[row, row, bvec, vec, vec, bvec, bvec],
        out_specs=[row, row],
        out_shape=[jax.ShapeDtypeStruct((bsz, l, d), F32), jax.ShapeDtypeStruct((bsz, l, d), BF16)],
        compiler_params=_cparams("parallel", "parallel"),
        name="resid_norm",
    )(x, y, gate, g1.reshape(1, d), g2.reshape(1, d), sc, sh)


def _bias_kernel(rb_ref, o_ref):
    j = lax.broadcasted_iota(jnp.int32, (KEY_CHUNK, Q_BLOCK), 0)
    i = lax.broadcasted_iota(jnp.int32, (KEY_CHUNK, Q_BLOCK), 1)
    max_exact = N_BUCKETS // 2
    for v, off in enumerate((Q_BLOCK, 0, 2 * Q_BLOCK, None)):
        if off is None:
            dist = jnp.full((KEY_CHUNK, Q_BLOCK), 2 * MAX_DIST, jnp.int32)
        else:
            dist = jnp.maximum(i + off - j, 0)
        nf = jnp.maximum(dist, 1).astype(F32)
        large = max_exact + (jnp.log(nf / max_exact) / math.log(MAX_DIST / max_exact)
                             * (N_BUCKETS - max_exact)).astype(jnp.int32)
        large = jnp.minimum(large, N_BUCKETS - 1)
        bucket = jnp.where(dist < max_exact, dist, large)
        for h in range(H_A):
            def body(b, acc, h=h, bucket=bucket):
                return jnp.where(bucket == b, rb_ref[b * H_A + h] * LOG2E, acc)
            o_ref[v, h] = lax.fori_loop(0, N_BUCKETS, body, jnp.zeros((KEY_CHUNK, Q_BLOCK), F32))


def _bias_tiles(rel_bias):
    return pl.pallas_call(
        _bias_kernel,
        in_specs=[pl.BlockSpec(memory_space=pltpu.SMEM)],
        out_specs=pl.BlockSpec(memory_space=pltpu.VMEM),
        out_shape=jax.ShapeDtypeStruct((4, H_A, KEY_CHUNK, Q_BLOCK), F32),
        name="dsa_bias_tiles",
    )(rel_bias.reshape(-1))


def _dsa_prep_kernel(pa_ref, gq_ref, gkv_ref, wuq_ref, wuk_ref, wqi_ref,
                     qlat_ref, qidx_ref, widx_ref, kidx_ref, ckv_ref):
    pa = pa_ref[0]
    cq = pa[:, :R_Q]
    ckv = pa[:, R_Q:R_Q + R_KV]
    kid = pa[:, R_Q + R_KV:R_Q + R_KV + D_I]
    wid = pa[:, R_Q + R_KV + LANES:R_Q + R_KV + LANES + H_I]
    cqn = (_rms(cq) * gq_ref[...]).astype(BF16)
    ckv_ref[0] = (_rms(ckv) * gkv_ref[...]).astype(BF16)
    kidx_ref[0] = kid.astype(BF16)
    widx_ref[0] = wid * (H_I ** -0.5 * D_I ** -0.5)
    q = jnp.dot(cqn, wuq_ref[...], preferred_element_type=F32)
    for h in range(H_A):
        qh = q[:, h * DH_A:(h + 1) * DH_A].astype(BF16)
        ql = lax.dot_general(qh, wuk_ref[h], NT_DIMS, preferred_element_type=F32)
        qlat_ref[0, 0, h * Q_BLOCK:(h + 1) * Q_BLOCK, :] = (ql * (DH_A ** -0.5 * LOG2E)).astype(BF16)
    for h in range(H_I):
        qi = jnp.dot(cqn, wqi_ref[h], preferred_element_type=F32)
        qidx_ref[0, 0, h * Q_BLOCK:(h + 1) * Q_BLOCK, :] = qi.astype(BF16)


def _dsa_prep(pa, gq, gkv, wuq, wuk, wqi):
    bsz, l, _ = pa.shape
    nb = l // Q_BLOCK
    full = lambda shp: pl.BlockSpec(shp, lambda b, i: (0,) * len(shp))
    return pl.pallas_call(
        _dsa_prep_kernel,
        grid=(bsz, nb),
        in_specs=[pl.BlockSpec((1, Q_BLOCK, A_PAD), lambda b, i: (b, i, 0)),
                  full((1, R_Q)), full((1, R_KV)), full((R_Q, H_A * DH_A)),
                  full((H_A, R_KV, DH_A)), full((H_I, R_Q, D_I))],
        out_specs=[pl.BlockSpec((1, 1, H_A * Q_BLOCK, R_KV), lambda b, i: (b, i, 0, 0)),
                   pl.BlockSpec((1, 1, H_I * Q_BLOCK, D_I), lambda b, i: (b, i, 0, 0)),
                   pl.BlockSpec((1, Q_BLOCK, H_I), lambda b, i: (b, i, 0)),
                   pl.BlockSpec((1, Q_BLOCK, D_I), lambda b, i: (b, i, 0)),
                   pl.BlockSpec((1, Q_BLOCK, R_KV), lambda b, i: (b, i, 0))],
        out_shape=[jax.ShapeDtypeStruct((bsz, nb, H_A * Q_BLOCK, R_KV), BF16),
                   jax.ShapeDtypeStruct((bsz, nb, H_I * Q_BLOCK, D_I), BF16),
                   jax.ShapeDtypeStruct((bsz, l, H_I), F32),
                   jax.ShapeDtypeStruct((bsz, l, D_I), BF16),
                   jax.ShapeDtypeStruct((bsz, l, R_KV), BF16)],
        compiler_params=_cparams("parallel", "parallel"),
        name="dsa_prep",
    )(pa, gq.reshape(1, R_Q), gkv.reshape(1, R_KV), wuq, wuk, wqi)


def _fold8(x, op, ways=1):
    n = x.shape[0] // 8
    ways = min(ways, n)
    acc = [x[g * 8:(g + 1) * 8] for g in range(ways)]
    for g in range(ways, n):
        acc[g % ways] = op(acc[g % ways], x[g * 8:(g + 1) * 8])
    while len(acc) > 1:
        acc = [op(acc[i], acc[i + 1]) for i in range(0, len(acc), 2)]
    return acc[0]


def _dsa_attn_kernel(qlat_ref, qidx_ref, widx_ref, kidx_ref, ckv_ref, ckvt_ref, wuvt_ref, bias_ref, o_ref,
                     key_ref, m_ref, ls_ref, acc_ref, sa_ref, sb_ref, lga_ref, lgb_ref, pa_ref, pb_ref,
                     aa_ref, ab_ref, *, topk):
    qi = pl.program_id(1)
    nch = (qi * Q_BLOCK) // KEY_CHUNK + 1
    krow = lax.broadcasted_iota(jnp.int32, (KEY_CHUNK, Q_BLOCK), 0)
    qlane = lax.broadcasted_iota(jnp.int32, (KEY_CHUNK, Q_BLOCK), 1)
    t_abs = qi * Q_BLOCK + qlane
    w = widx_ref[0, 0]

    npair = (nch + 1) // 2
    last_chunk = kidx_ref.shape[1] // KEY_CHUNK - 1

    def chunk_start(c):
        return pl.multiple_of(jnp.minimum(c, last_chunk) * KEY_CHUNK, KEY_CHUNK)

    def score_mm(c, s_ref):
        kc = kidx_ref[0, pl.ds(chunk_start(c), KEY_CHUNK), :]
        s_ref[...] = lax.dot_general(kc, qidx_ref[0, 0], NT_DIMS, preferred_element_type=F32)

    def score_keys(c, s_ref):
        acc = jnp.zeros((KEY_CHUNK, Q_BLOCK), F32)
        for h in range(H_I):
            acc = acc + w[h:h + 1] * jnp.maximum(s_ref[:, h * Q_BLOCK:(h + 1) * Q_BLOCK], 0.0)
        acc = jnp.where(c * KEY_CHUNK + krow <= t_abs, acc, -jnp.inf)
        bits = lax.bitcast_convert_type(acc, jnp.int32)
        key_ref[c] = bits ^ ((bits >> 31) & 0x7FFFFFFF)

    def score_body(j, carry):
        c = 2 * j
        score_mm(c + 1, sb_ref)
        score_keys(c, sa_ref)
        score_mm(c + 2, sa_ref)
        score_keys(c + 1, sb_ref)
        return carry

    score_mm(0, sa_ref)
    lax.fori_loop(0, npair, score_body, 0)

    def count_ge(cand):
        def body(c, cnt):
            return (cnt + _fold8(jnp.where(key_ref[2 * c] >= cand, 1.0, 0.0), jnp.add, 4)
                    + _fold8(jnp.where(key_ref[2 * c + 1] >= cand, 1.0, 0.0), jnp.add, 4))
        cnt = lax.fori_loop(0, npair, body, jnp.zeros((8, Q_BLOCK), F32))
        for shift in (4, 2, 1):
            cnt = cnt + pltpu.roll(cnt, shift, 0)
        return cnt[0:1]

    kf = float(topk)
    cnt0 = count_ge(jnp.zeros((1, Q_BLOCK), jnp.int32))
    thr = jnp.where(cnt0 >= kf, 0, INT_MIN).astype(jnp.int32)
    cnt_thr = jnp.where(cnt0 >= kf, cnt0, ((nch + 1) // 2 * (2 * KEY_CHUNK)).astype(F32))
    few = (qi * Q_BLOCK + lax.broadcasted_iota(jnp.int32, (1, Q_BLOCK), 1)) < topk

    def unsettled(cnt_thr):
        return jnp.max(jnp.where(few | (cnt_thr == kf), 0.0, 1.0)) > 0.5

    def group_cond(st):
        return (st[0] < 8) & st[3]

    def group_body(st):
        g, thr, cnt_thr, _ = st
        for k in range(4):
            i = 4 * g + k
            bit = jnp.where(i < 31, jnp.left_shift(jnp.int32(1), jnp.maximum(30 - i, 0)), 0)
            cand = thr + bit
            cnt = count_ge(cand)
            take = cnt >= kf
            thr = jnp.where(take, cand, thr)
            cnt_thr = jnp.where(take, cnt, cnt_thr)
        return g + 1, thr, cnt_thr, unsettled(cnt_thr)

    thr = lax.while_loop(group_cond, group_body, (jnp.int32(0), thr, cnt_thr, unsettled(cnt_thr)))[1]

    odd = (qi % 2) == 1

    def logits_mm(c, lg_ref):
        kv = ckv_ref[0, pl.ds(chunk_start(c), KEY_CHUNK), :]
        lg_ref[...] = lax.dot_general(kv, qlat_ref[0, 0], NT_DIMS, preferred_element_type=F32)

    def chunk_mask(c):
        sel = (key_ref[c] >= thr) & (c * KEY_CHUNK + krow <= t_abs)
        idx = jnp.where(c == nch - 1, jnp.where(odd, 0, 1),
                        jnp.where((c == nch - 2) & jnp.logical_not(odd), 2, 3))
        return sel, idx

    m_ref[...] = jnp.full(m_ref.shape, -1e30, F32)
    ls_ref[...] = jnp.zeros(ls_ref.shape, F32)
    acc_ref[...] = jnp.zeros(acc_ref.shape, F32)
    pb_ref[...] = jnp.zeros(pb_ref.shape, BF16)
    ab_ref[...] = jnp.ones(ab_ref.shape, F32)

    def chunk_probs(c, lg_ref, p_ref, a_ref):
        sel, idx = chunk_mask(c)
        for h in range(H_A):
            cs = slice(h * Q_BLOCK, (h + 1) * Q_BLOCK)
            lg = jnp.where(sel, lg_ref[:, cs] + bias_ref[idx, h], -1e30)
            cm = _fold8(lg, jnp.maximum)
            for shift in (4, 2, 1):
                cm = jnp.maximum(cm, pltpu.roll(cm, shift, 0))
            m_old = m_ref[:, cs]
            m_new = jnp.maximum(m_old, cm[0:1])
            alpha = jnp.exp2(m_old - m_new)
            p = jnp.where(sel, jnp.exp2(lg - m_new), 0.0)
            ls_ref[:, cs] = alpha * ls_ref[:, cs] + _fold8(p, jnp.add)
            m_ref[:, cs] = m_new
            a_ref[:, cs] = alpha
            p_ref[:, cs] = p.astype(BF16)

    def values_mm(c, p_ref, a_ref):
        acc_ref[...] = acc_ref[...] * a_ref[...] + jnp.dot(
            ckvt_ref[0, :, pl.ds(chunk_start(c), KEY_CHUNK)], p_ref[...], preferred_element_type=F32)

    def attn_body(j, carry):
        c = 2 * j
        logits_mm(c + 1, lgb_ref)
        chunk_probs(c, lga_ref, pa_ref, aa_ref)
        values_mm(jnp.maximum(c - 1, 0), pb_ref, ab_ref)
        logits_mm(c + 2, lga_ref)
        chunk_probs(c + 1, lgb_ref, pb_ref, ab_ref)
        values_mm(c, pa_ref, aa_ref)
        return carry

    logits_mm(0, lga_ref)
    lax.fori_loop(0, npair, attn_body, 0)
    values_mm(2 * npair - 1, pb_ref, ab_ref)
    lsum = jnp.sum(ls_ref[...], axis=0, keepdims=True)

    for h in range(H_A):
        cs = slice(h * Q_BLOCK, (h + 1) * Q_BLOCK)
        o_lat = (acc_ref[:, cs] / lsum[:, cs]).astype(BF16)
        out = jnp.dot(wuvt_ref[h], o_lat, preferred_element_type=F32)
        o_ref[0, h * DH_A:(h + 1) * DH_A, :] = out.astype(o_ref.dtype)


def _dsa_attn(qlat, qidx, widx, kidx, ckv, wuv, bias):
    bsz, l, _ = ckv.shape
    nb = l // Q_BLOCK
    topk = min(TOPK_MAX, l // 4)
    nchunks = 2 * ((l + 2 * KEY_CHUNK - 1) // (2 * KEY_CHUNK))
    hq = H_A * Q_BLOCK
    widx_t = widx.reshape(bsz, nb, Q_BLOCK, H_I).transpose(0, 1, 3, 2)
    ckv_t = ckv.transpose(0, 2, 1)
    wuv_t = wuv.transpose(0, 2, 1)
    return pl.pallas_call(
        functools.partial(_dsa_attn_kernel, topk=topk),
        grid=(bsz, nb),
        in_specs=[pl.BlockSpec((1, 1, hq, R_KV), lambda b, i: (b, i, 0, 0)),
                  pl.BlockSpec((1, 1, H_I * Q_BLOCK, D_I), lambda b, i: (b, i, 0, 0)),
                  pl.BlockSpec((1, 1, H_I, Q_BLOCK), lambda b, i: (b, i, 0, 0)),
                  pl.BlockSpec((1, l, D_I), lambda b, i: (b, 0, 0)),
                  pl.BlockSpec((1, l, R_KV), lambda b, i: (b, 0, 0)),
                  pl.BlockSpec((1, R_KV, l), lambda b, i: (b, 0, 0)),
                  pl.BlockSpec((H_A, DH_A, R_KV), lambda b, i: (0, 0, 0)),
                  pl.BlockSpec((4, H_A, KEY_CHUNK, Q_BLOCK), lambda b, i: (0, 0, 0, 0))],
        out_specs=pl.BlockSpec((1, H_A * DH_A, Q_BLOCK), lambda b, i: (b, 0, i)),
        out_shape=jax.ShapeDtypeStruct((bsz, H_A * DH_A, l), BF16),
        scratch_shapes=[pltpu.VMEM((nchunks, KEY_CHUNK, Q_BLOCK), jnp.int32),
                        pltpu.VMEM((1, hq), F32), pltpu.VMEM((8, hq), F32), pltpu.VMEM((R_KV, hq), F32),
                        pltpu.VMEM((KEY_CHUNK, H_I * Q_BLOCK), F32), pltpu.VMEM((KEY_CHUNK, H_I * Q_BLOCK), F32),
                        pltpu.VMEM((KEY_CHUNK, hq), F32), pltpu.VMEM((KEY_CHUNK, hq), F32),
                        pltpu.VMEM((KEY_CHUNK, hq), BF16), pltpu.VMEM((KEY_CHUNK, hq), BF16),
                        pltpu.VMEM((1, hq), F32), pltpu.VMEM((1, hq), F32)],
        compiler_params=_cparams("parallel", "arbitrary"),
        name="dsa_attn",
    )(qlat, qidx, widx_t, kidx, ckv, ckv_t, wuv_t, bias)


def _seg_sum(x, bd):
    hi = x.astype(BF16)
    lo = (x - hi.astype(F32)).astype(BF16)
    outs = []
    for p in range(x.shape[1] // LANES):
        sl = slice(p * LANES, (p + 1) * LANES)
        outs.append(jnp.dot(hi[:, sl], bd, preferred_element_type=F32)
                    + jnp.dot(lo[:, sl], bd, preferred_element_type=F32))
    return outs[0] if len(outs) == 1 else jnp.concatenate(outs, axis=1)


def _rwkv_pre_kernel(pb_ref, mu_ref, w0_ref, a0_ref, kk_ref, ka_ref, rk_ref, wup_ref, aup_ref, gup_ref,
                     bd_ref, tril_ref, ab_ref, bb_ref, kb_ref, rb_ref, v_ref, p_ref, g_ref, bonus_ref,
                     carry_ref):
    @pl.when(pl.program_id(1) == 0)
    def _():
        carry_ref[...] = jnp.zeros_like(carry_ref)

    x = pb_ref[0]
    tm = x.shape[0]
    rows = lax.broadcasted_iota(jnp.int32, x.shape, 0)
    prev = jnp.where(rows == 0, carry_ref[...], pltpu.roll(x, 1, 0))
    carry_ref[...] = x[tm - 1:tm]
    ps = x + mu_ref[...] * (prev - x)
    r = ps[:, :W_B]
    k = ps[:, W_B:2 * W_B]
    v = ps[:, 2 * W_B:3 * W_B]
    o = 3 * W_B
    wd = ps[:, o:o + LORA_W]
    ad = ps[:, o + LORA_W:o + LORA_W + LORA_A]
    gd = ps[:, o + LORA_W + LORA_A:]
    hp = lax.Precision.HIGHEST
    z = w0_ref[...] + jnp.dot(jnp.tanh(wd), wup_ref[...], precision=hp, preferred_element_type=F32)
    u = -z
    w = -(jnp.maximum(u, 0.0) + jnp.log(1.0 + jnp.exp(-jnp.abs(u)))) - 0.5
    logd = -jnp.exp(w)
    logp = jnp.dot(tril_ref[...], logd, precision=hp, preferred_element_type=F32)
    pinv = jnp.exp(-logp)
    a = _sigmoid(a0_ref[...] + jnp.dot(ad, aup_ref[...], precision=hp, preferred_element_type=F32))
    g_ref[0] = jnp.dot(_sigmoid(gd).astype(BF16), gup_ref[...], preferred_element_type=F32)
    bd = bd_ref[...]
    kk = k * kk_ref[...]
    kk = kk * lax.rsqrt(jnp.maximum(_seg_sum(kk * kk, bd), 1e-24))
    k2 = k * (1.0 + (a - 1.0) * ka_ref[...])
    p = jnp.exp(logp)
    ab_ref[0] = -kk * jnp.exp(logp - logd)
    bb_ref[0] = kk * a * pinv
    kb_ref[0] = k2 * pinv
    rb_ref[0] = r * p
    v_ref[0] = v
    p_ref[0] = p
    bonus_ref[0] = _seg_sum(r * k2 * rk_ref[...], bd) * v


def _rwkv_pre(pb, mu, w0, a0, k_k, k_a, r_k, w_up, a_up, g_up, bd, tm=128):
    bsz, l, _ = pb.shape
    tm = min(tm, l)
    assert tm % RW_CHUNK == 0
    tril = jnp.kron(jnp.eye(tm // RW_CHUNK, dtype=F32), jnp.tril(jnp.ones((RW_CHUNK, RW_CHUNK), F32)))
    full = lambda shp: pl.BlockSpec(shp, lambda b, i: (0,) * len(shp))
    row = pl.BlockSpec((1, tm, W_B), lambda b, i: (b, i, 0))
    vec = full((1, W_B))
    return pl.pallas_call(
        _rwkv_pre_kernel,
        grid=(bsz, l // tm),
        in_specs=[pl.BlockSpec((1, tm, B_PAD), lambda b, i: (b, i, 0)), full((1, B_PAD)),
                  vec, vec, vec, vec, vec,
                  full((LORA_W, W_B)), full((LORA_A, W_B)), full((LORA_G_PAD, W_B)), full((LANES, LANES)),
                  full((tm, tm))],
        out_specs=[row] * 8,
        out_shape=[jax.ShapeDtypeStruct((bsz, l, W_B), F32)] * 8,
        scratch_shapes=[pltpu.VMEM((1, B_PAD), F32)],
        compiler_params=_cparams("parallel", "arbitrary"),
        name="rwkv_pre",
    )(pb, mu, w0.reshape(1, W_B), a0.reshape(1, W_B), k_k.reshape(1, W_B), k_a.reshape(1, W_B),
      r_k.reshape(1, W_B), w_up, a_up, g_up, bd, tril)


def _mm(a, b, dims, passes):
    dg = lambda x, y: lax.dot_general(x, y, dims, preferred_element_type=F32)
    ah, bh = a.astype(BF16), b.astype(BF16)
    if passes == 1:
        return dg(ah, bh)
    al = (a - ah.astype(F32)).astype(BF16)
    bl = (b - bh.astype(F32)).astype(BF16)
    return dg(ah, bh) + (dg(ah, bl) + dg(al, bh))


def _rwkv_chunk_kernel(ab_ref, bb_ref, kb_ref, rb_ref, v_ref, pe_ref, y_ref, s_ref, *, npair):
    @pl.when(pl.program_id(1) == 0)
    def _():
        s_ref[...] = jnp.zeros_like(s_ref)

    c = RW_CHUNK
    ri = lax.broadcasted_iota(jnp.int32, (c, c), 0)
    ci = lax.broadcasted_iota(jnp.int32, (c, c), 1)
    strict, incl = ci < ri, ci <= ri
    eye = jnp.where(ri == ci, 1.0, 0.0)
    head0 = lax.broadcasted_iota(jnp.int32, (1, LANES), 1) < N_B
    bi = lax.broadcasted_iota(jnp.int32, (LANES, LANES), 0) < N_B
    bj = lax.broadcasted_iota(jnp.int32, (LANES, LANES), 1) < N_B
    blockdiag = bi == bj
    nt = (((2,), (2,)), ((0,), (0,)))
    nn = (((2,), (1,)), ((0,), (0,)))
    tn = (((1,), (1,)), ((0,), (0,)))
    tiles = lambda ref: jnp.stack([ref[0, :, p * LANES:(p + 1) * LANES] for p in range(npair)])
    ab, bb, kb, rb, v = tiles(ab_ref), tiles(bb_ref), tiles(kb_ref), tiles(rb_ref), tiles(v_ref)
    pe = jnp.stack([pe_ref[0, 0, :, p * LANES:(p + 1) * LANES] for p in range(npair)])
    s0 = s_ref[...]
    both = lambda x: jnp.concatenate([x, x], axis=0)
    pick = lambda x: jnp.where(head0, x[:npair], x[npair:])
    split = lambda x: jnp.concatenate([jnp.where(head0, x, 0.0), jnp.where(head0, 0.0, x)], axis=0)
    ar = jnp.concatenate([ab, rb], axis=1)
    ars = split(ar)
    gb = _mm(ars, both(bb), nt, RW_PASSES)
    gk = _mm(ars, both(kb), nt, RW_PASSES)
    m_ab = jnp.where(strict, gb[:, :c], 0.0)
    n_rb = jnp.where(incl, gb[:, c:], 0.0)
    m_ak = jnp.where(strict, gk[:, :c], 0.0)
    n_rk = jnp.where(incl, gk[:, c:], 0.0)
    tinv = eye + m_ab
    pw = m_ab
    for _ in range(5):
        pw = _mm(pw, pw, nn, RW_PASSES)
        tinv = tinv + _mm(tinv, pw, nn, RW_PASSES)
    x0 = _mm(ar, s0, nt, RW_PASSES)
    v2 = both(v)
    x = x0[:, :c] + pick(_mm(m_ak, v2, nn, RW_PASSES))
    u = pick(_mm(tinv, both(x), nn, RW_PASSES))
    y = x0[:, c:] + pick(_mm(n_rb, both(u), nn, RW_PASSES)) + pick(_mm(n_rk, v2, nn, RW_PASSES))
    for p in range(npair):
        y_ref[0, :, p * LANES:(p + 1) * LANES] = y[p]
    upd = _mm(jnp.concatenate([u, v], axis=1), jnp.concatenate([bb, kb], axis=1), tn, RW_PASSES)
    s_ref[...] = (s0 + jnp.where(blockdiag, upd, 0.0)) * pe


def _rwkv_chunks(ab, bb, kb, rb, v, pe):
    bsz, l, wb = ab.shape
    nt = l // RW_CHUNK
    npair = wb // LANES
    row = pl.BlockSpec((1, RW_CHUNK, wb), lambda b, i: (b, i, 0))
    return pl.pallas_call(
        functools.partial(_rwkv_chunk_kernel, npair=npair),
        grid=(bsz, nt),
        in_specs=[row] * 5 + [pl.BlockSpec((1, 1, 1, wb), lambda b, i: (b, i, 0, 0))],
        out_specs=row,
        out_shape=jax.ShapeDtypeStruct((bsz, l, wb), F32),
        scratch_shapes=[pltpu.VMEM((npair, LANES, LANES), F32)],
        compiler_params=_cparams("parallel", "arbitrary"),
        name="rwkv_chunks",
    )(ab, bb, kb, rb, v, pe)


def _rwkv_post_kernel(y_ref, bonus_ref, g_ref, lw_ref, lb_ref, bd_ref, o_ref):
    bd = bd_ref[...]
    y = y_ref[0]
    mean = _seg_sum(y, bd) * (1.0 / N_B)
    c = y - mean
    var = _seg_sum(c * c, bd) * (1.0 / N_B)
    yn = c * lax.rsqrt(var + GN_EPS) * lw_ref[...] + lb_ref[...]
    o_ref[0] = ((yn + bonus_ref[0]) * g_ref[0]).astype(o_ref.dtype)


def _rwkv_post(y, bonus, g, lnx_w, lnx_b, bd, tm=256):
    bsz, l, _ = y.shape
    tm = min(tm, l)
    row = pl.BlockSpec((1, tm, W_B), lambda b, i: (b, i, 0))
    vec = pl.BlockSpec((1, W_B), lambda b, i: (0, 0))
    return pl.pallas_call(
        _rwkv_post_kernel,
        grid=(bsz, l // tm),
        in_specs=[row, row, row, vec, vec, pl.BlockSpec((LANES, LANES), lambda b, i: (0, 0))],
        out_specs=row,
        out_shape=jax.ShapeDtypeStruct((bsz, l, W_B), BF16),
        compiler_params=_cparams("parallel", "parallel"),
        name="rwkv_post",
    )(y, bonus, g, lnx_w.reshape(1, W_B), lnx_b.reshape(1, W_B), bd)


def _rwkv_mixer(pb, mu, w0, w_up, a0, a_up, g_up, k_k, k_a, r_k, lnx_w, lnx_b):
    bsz, l, _ = pb.shape
    nt = l // RW_CHUNK
    eye2 = jnp.kron(jnp.eye(2, dtype=F32), jnp.ones((N_B, N_B), F32)).astype(BF16)
    ab, bb, kb, rb, v, p, g, bonus = _rwkv_pre(pb, mu, w0, a0, k_k, k_a, r_k, w_up, a_up, g_up, eye2)
    pe = p[:, RW_CHUNK - 1::RW_CHUNK].reshape(bsz, nt, 1, W_B)
    y = _rwkv_chunks(ab, bb, kb, rb, v, pe)
    return _rwkv_post(y, bonus, g, lnx_w, lnx_b, eye2)


def _hgrn_kernel(q_ref, f_ref, i_ref, g_ref, lb_ref, on_ref, tril_ref, o_ref, st_ref, *, layer, hb, tt):
    @pl.when(pl.program_id(2) == 0)
    def _():
        st_ref[...] = jnp.zeros_like(st_ref)

    x = lb_ref[...]
    e = jnp.exp(x - jnp.max(x, axis=0, keepdims=True))
    sm = e / jnp.sum(e, axis=0, keepdims=True)
    cs = sm[0:1]
    for i in range(1, layer + 1):
        cs = cs + sm[i:i + 1]
    lb_all = cs - sm[0:1]
    rowi = lax.broadcasted_iota(jnp.int32, (SUB, LANES), 0)
    lanei = lax.broadcasted_iota(jnp.int32, (SUB, LANES), 1)
    prep = []
    for hh in range(hb):
        sl = slice(hh * LANES, (hh + 1) * LANES)
        lb = lb_all[:, sl]
        qraw = q_ref[0][:, sl]
        ff = f_ref[0][:, sl]
        q = qraw * _sigmoid(qraw)
        logsig = jnp.minimum(ff, 0.0) - jnp.log(1.0 + jnp.exp(-jnp.abs(ff)))
        la = jnp.log(lb)
        lbv = jnp.log(1.0 - lb) + logsig
        mx = jnp.maximum(la, lbv)
        logf = mx + jnp.log(jnp.exp(la - mx) + jnp.exp(lbv - mx))
        k = (1.0 - lb) * _sigmoid(-ff)
        bcum = jnp.dot(tril_ref[...], logf, precision=lax.Precision.HIGHEST, preferred_element_type=F32)
        prep.append((q, k, i_ref[0][:, sl], bcum))
    sts = [st_ref[hh] for hh in range(hb)]
    outs = [[] for _ in range(hb)]
    for n in range(tt // SUB):
        rs = slice(n * SUB, (n + 1) * SUB)
        for hh in range(hb):
            q, k, v, bcum = prep[hh]
            st = sts[hh]
            bq, qn, kn, vn = bcum[rs], q[rs], k[rs], v[rs]
            bend = bq[SUB - 1:SUB]
            qd = (qn * jnp.exp(bq)).astype(BF16)
            o = lax.dot_general(qd, st.astype(BF16), NT_DIMS, preferred_element_type=F32)
            a = jnp.zeros((SUB, LANES), F32)
            for s in range(SUB):
                wgt = kn[s:s + 1] * jnp.exp(jnp.minimum(bq - bq[s:s + 1], 0.0))
                a = jnp.where(lanei == s, jnp.sum(qn * wgt, axis=1, keepdims=True), a)
            a = jnp.where(lanei <= rowi, a, 0.0)[:, :SUB]
            o = o + jnp.dot(a.astype(BF16), vn.astype(BF16), preferred_element_type=F32)
            kd = (kn * jnp.exp(bend - bq)).astype(BF16)
            sts[hh] = st * jnp.exp(bend) + lax.dot_general(vn.astype(BF16), kd, TN_DIMS,
                                                           preferred_element_type=F32)
            outs[hh].append(o)
    for hh in range(hb):
        sl = slice(hh * LANES, (hh + 1) * LANES)
        st_ref[hh] = sts[hh]
        gg = g_ref[0][:, sl]
        o = jnp.concatenate(outs[hh], axis=0)
        o = _rms(o) * on_ref[:, sl] * (gg * _sigmoid(gg))
        o_ref[0, :, sl] = o.astype(o_ref.dtype)


def _hgrn_mixer(pc, hgrn_lb, onorm, layer, hb=12, tt=128):
    bsz, l, _ = pc.shape
    depth = hgrn_lb.shape[0]
    tt = min(tt, l)
    ng = H_C // hb
    wblk = hb * LANES
    tril = jnp.kron(jnp.eye(tt // SUB, dtype=F32), jnp.tril(jnp.ones((SUB, SUB), F32)))
    col = lambda j: pl.BlockSpec((1, tt, wblk), lambda b, h, i, j=j: (b, i, j * ng + h))
    return pl.pallas_call(
        functools.partial(_hgrn_kernel, layer=layer, hb=hb, tt=tt),
        grid=(bsz, ng, l // tt),
        in_specs=[col(0), col(1), col(2), col(3),
                  pl.BlockSpec((depth, wblk), lambda b, h, i: (0, h)),
                  pl.BlockSpec((1, wblk), lambda b, h, i: (0, h)),
                  pl.BlockSpec((tt, tt), lambda b, h, i: (0, 0))],
        out_specs=pl.BlockSpec((1, tt, wblk), lambda b, h, i: (b, i, h)),
        out_shape=jax.ShapeDtypeStruct((bsz, l, W_C), BF16),
        scratch_shapes=[pltpu.VMEM((hb, DV_C, DK_C), F32)],
        compiler_params=_cparams("parallel", "parallel", "arbitrary"),
        name="hgrn2",
    )(pc, pc, pc, pc, hgrn_lb, onorm.reshape(1, W_C), tril)


def _pack_w_in(w):
    d = w.shape[0]
    a_cols = R_Q + R_KV + D_I + H_I
    b_cols = 3 * W_B + LORA_W + LORA_A + LORA_G
    wa, wb, wc = w[:, :a_cols], w[:, a_cols:a_cols + b_cols], w[:, a_cols + b_cols:]
    z = lambda n: jnp.zeros((d, n), w.dtype)
    o = R_Q + R_KV
    wa = jnp.concatenate([wa[:, :o + D_I], z(LANES - D_I), wa[:, o + D_I:], z(LANES - H_I)], axis=1)
    wb = jnp.concatenate([wb, z(LORA_G_PAD - LORA_G)], axis=1)
    return wa.astype(BF16), wb.astype(BF16), wc.astype(BF16)


def kernel(x, c, rel_bias, hgrn_lb, ada_w, ada_b, norm_g, w_in, w_out, mla_q_norm, mla_kv_norm, w_uq, w_uk, w_uv, w_qidx, rwkv_mu, rwkv_w0, rwkv_w_up, rwkv_a0, rwkv_a_up, rwkv_g_up, rwkv_k_k, rwkv_k_a, rwkv_r_k, rwkv_lnx_w, rwkv_lnx_b, hgrn_onorm, w_ff1, w_ff2):
    bsz, l, d = x.shape
    depth = ada_w.shape[0]
    m = bsz * l
    mod = _ada_mod(c, ada_w, ada_b)
    mods = [[mod[i, :, None, j * d:(j + 1) * d] for j in range(6)] for i in range(depth)]
    bias = _bias_tiles(rel_bias)
    h = _norm_mod(x, norm_g[0, 0], mods[0][1], mods[0][0])
    for i in range(depth):
        sh_m, sc_m, g_m, sh_f, sc_f, g_f = mods[i]
        wa, wb, wc = _pack_w_in(w_in[i])
        h2d = h.reshape(m, d)
        pa = _matmul(h2d, wa, F32, 512, A_PAD, 4096, name="in_proj_a").reshape(bsz, l, -1)
        pb = _matmul(h2d, wb, F32, 1024, 768, 4096, name="in_proj_b").reshape(bsz, l, -1)
        pc = _matmul(h2d, wc, F32, 1024, 1024, 4096, name="in_proj_c").reshape(bsz, l, -1)
        wuq = w_uq[i].reshape(R_Q, H_A * DH_A).astype(BF16)
        wqi = jnp.transpose(w_qidx[i], (1, 0, 2)).astype(BF16)
        qlat, qidx, widx, kidx, ckv = _dsa_prep(pa, mla_q_norm[i], mla_kv_norm[i], wuq,
                                                w_uk[i].astype(BF16), wqi)
        y_a = _dsa_attn(qlat, qidx, widx, kidx, ckv, w_uv[i].astype(BF16), bias).transpose(0, 2, 1)
        mu = jnp.concatenate([rwkv_mu[i], jnp.zeros((LORA_G_PAD - LORA_G,), F32)]).reshape(1, B_PAD)
        gup = jnp.concatenate([rwkv_g_up[i], jnp.zeros((LORA_G_PAD - LORA_G, W_B), F32)]).astype(BF16)
        y_b = _rwkv_mixer(pb, mu, rwkv_w0[i], rwkv_w_up[i], rwkv_a0[i], rwkv_a_up[i], gup,
                          rwkv_k_k[i], rwkv_k_a[i], rwkv_r_k[i], rwkv_lnx_w[i], rwkv_lnx_b[i])
        y_c = _hgrn_mixer(pc, hgrn_lb, hgrn_onorm[i], i)
        ycat = jnp.concatenate([y_a, y_b, y_c], axis=-1).reshape(m, -1)
        y = _matmul(ycat, w_out[i].astype(BF16), F32, 1024, 1024, 4096, name="out_proj").reshape(bsz, l, d)
        x, h = _resid_norm(x, y, g_m, norm_g[i, 1], (norm_g[i, 2], sc_f, sh_f))
        u = _matmul(h.reshape(m, d), w_ff1[i].astype(BF16), BF16, 1024, 1024, 4096, relu2=True, name="ffn_up")
        y = _matmul(u, w_ff2[i].astype(BF16), F32, 1024, 1024, 4096, name="ffn_down").reshape(bsz, l, d)
        nxt = None if i == depth - 1 else (norm_g[i + 1, 0], mods[i + 1][1], mods[i + 1][0])
        x, h = _resid_norm(x, y, g_f, norm_g[i, 3], nxt)
    return x
```

```python
import functools
import math

import jax
import jax.numpy as jnp
from jax import lax
from jax.experimental import pallas as pl
from jax.experimental.pallas import tpu as pltpu

H_A, DH_A, R_Q, R_KV, H_I, D_I = 8, 128, 768, 256, 16, 64
TOPK_MAX, Q_BLOCK, N_BUCKETS, MAX_DIST = 256, 128, 32, 128
H_B, N_B, LORA_W, LORA_A, LORA_G = 24, 64, 128, 128, 480
W_B = H_B * N_B
H_C, DK_C, DV_C = 12, 128, 128
W_C = H_C * DK_C
EPS = 1e-6
GN_EPS = 64e-5

LANES = 128
KEY_CHUNK = 256
SUB = 8
RW_CHUNK = 64
RW_PASSES = 1
LORA_G_PAD = 512
A_PAD = 1280
B_PAD = 3 * W_B + LORA_W + LORA_A + LORA_G_PAD
VMEM_LIMIT = 56 * 1024 * 1024

F32 = jnp.float32
BF16 = jnp.bfloat16
NT_DIMS = (((1,), (1,)), ((), ()))
TN_DIMS = (((0,), (0,)), ((), ()))
INT_MIN = -2 ** 31
LOG2E = math.log2(math.e)


def _cparams(*sem):
    return pltpu.CompilerParams(dimension_semantics=sem, vmem_limit_bytes=VMEM_LIMIT)


def _sigmoid(x):
    return 1.0 / (1.0 + jnp.exp(-x))


def _rms(x, eps=EPS):
    return x * lax.rsqrt(jnp.mean(x * x, axis=-1, keepdims=True) + eps)


def _mm_kernel(a_ref, b_ref, o_ref, acc_ref, *, nk, relu2):
    k = pl.program_id(2)

    @pl.when(k == 0)
    def _():
        acc_ref[...] = jnp.zeros_like(acc_ref)

    acc_ref[...] += jnp.dot(a_ref[...], b_ref[...], preferred_element_type=F32)

    @pl.when(k == nk - 1)
    def _():
        r = acc_ref[...]
        if relu2:
            r = jnp.square(jnp.maximum(r, 0.0))
        o_ref[...] = r.astype(o_ref.dtype)


def _mm_full_k_kernel(a_ref, b_ref, o_ref, *, relu2):
    r = jnp.dot(a_ref[...], b_ref[...], preferred_element_type=F32)
    if relu2:
        r = jnp.square(jnp.maximum(r, 0.0))
    o_ref[...] = r.astype(o_ref.dtype)


def _matmul(a, b, out_dtype, tm, tn, tk, relu2=False, name="matmul"):
    m, kdim = a.shape
    _, n = b.shape
    tm, tn, tk = min(tm, m), min(tn, n), min(tk, kdim)
    assert m % tm == 0 and n % tn == 0 and kdim % tk == 0, (a.shape, b.shape, tm, tn, tk)
    nk = kdim // tk
    if nk == 1:
        return pl.pallas_call(
            functools.partial(_mm_full_k_kernel, relu2=relu2),
            grid=(m // tm, n // tn),
            in_specs=[pl.BlockSpec((tm, kdim), lambda i, j: (i, 0)),
                      pl.BlockSpec((kdim, tn), lambda i, j: (0, j))],
            out_specs=pl.BlockSpec((tm, tn), lambda i, j: (i, j)),
            out_shape=jax.ShapeDtypeStruct((m, n), out_dtype),
            compiler_params=_cparams("parallel", "parallel"),
            name=name,
        )(a, b)
    return pl.pallas_call(
        functools.partial(_mm_kernel, nk=nk, relu2=relu2),
        grid=(m // tm, n // tn, nk),
        in_specs=[pl.BlockSpec((tm, tk), lambda i, j, k: (i, k)),
                  pl.BlockSpec((tk, tn), lambda i, j, k: (k, j))],
        out_specs=pl.BlockSpec((tm, tn), lambda i, j, k: (i, j)),
        out_shape=jax.ShapeDtypeStruct((m, n), out_dtype),
        scratch_shapes=[pltpu.VMEM((tm, tn), F32)],
        compiler_params=_cparams("parallel", "parallel", "arbitrary"),
        name=name,
    )(a, b)


def _ada_kernel(c_ref, w_ref, b_ref, o_ref):
    k = pl.program_id(2)

    @pl.when(k == 0)
    def _():
        o_ref[0] = jnp.broadcast_to(b_ref[0], o_ref.shape[1:])

    c = c_ref[...]
    ca = (c * _sigmoid(c)).astype(BF16)
    o_ref[0] += jnp.dot(ca, w_ref[0].astype(BF16), preferred_element_type=F32)


def _ada_mod(c, ada_w, ada_b):
    nl, d, n = ada_w.shape
    bsz = c.shape[0]
    rows = 8
    cp = jnp.zeros((rows, d), F32).at[:bsz].set(c)
    tn, tk = min(2048, n), min(1024, d)
    out = pl.pallas_call(
        _ada_kernel,
        grid=(nl, n // tn, d // tk),
        in_specs=[pl.BlockSpec((rows, tk), lambda l, j, k: (0, k)),
                  pl.BlockSpec((1, tk, tn), lambda l, j, k: (l, k, j)),
                  pl.BlockSpec((1, 1, tn), lambda l, j, k: (l, 0, j))],
        out_specs=pl.BlockSpec((1, rows, tn), lambda l, j, k: (l, 0, j)),
        out_shape=jax.ShapeDtypeStruct((nl, rows, n), F32),
        compiler_params=_cparams("parallel", "parallel", "arbitrary"),
        name="ada_mod",
    )(cp, ada_w, ada_b.reshape(nl, 1, n))
    return out[:, :bsz]


def _norm_mod_kernel(x_ref, g_ref, sc_ref, sh_ref, h_ref):
    y = _rms(x_ref[0]) * g_ref[...]
    h_ref[0] = (y * (1.0 + sc_ref[0]) + sh_ref[0]).astype(h_ref.dtype)


def _norm_mod(x, g, sc, sh, tm=256):
    bsz, l, d = x.shape
    tm = min(tm, l)
    row = pl.BlockSpec((1, tm, d), lambda b, i: (b, i, 0))
    vec = pl.BlockSpec((1, d), lambda b, i: (0, 0))
    bvec = pl.BlockSpec((1, 1, d), lambda b, i: (b, 0, 0))
    return pl.pallas_call(
        _norm_mod_kernel,
        grid=(bsz, l // tm),
        in_specs=[row, vec, bvec, bvec],
        out_specs=row,
        out_shape=jax.ShapeDtypeStruct((bsz, l, d), BF16),
        compiler_params=_cparams("parallel", "parallel"),
        name="norm_mod",
    )(x, g.reshape(1, d), sc, sh)


def _resid_kernel(x_ref, y_ref, gate_ref, g1_ref, g2_ref, sc_ref, sh_ref, xo_ref, h_ref):
    xn = x_ref[0] + gate_ref[0] * (_rms(y_ref[0]) * g1_ref[...])
    xo_ref[0] = xn
    hn = _rms(xn) * g2_ref[...]
    h_ref[0] = (hn * (1.0 + sc_ref[0]) + sh_ref[0]).astype(h_ref.dtype)


def _resid_last_kernel(x_ref, y_ref, gate_ref, g1_ref, xo_ref):
    xo_ref[0] = x_ref[0] + gate_ref[0] * (_rms(y_ref[0]) * g1_ref[...])


def _resid_norm(x, y, gate, g1, nxt=None, tm=128):
    bsz, l, d = x.shape
    tm = min(tm, l)
    row = pl.BlockSpec((1, tm, d), lambda b, i: (b, i, 0))
    vec = pl.BlockSpec((1, d), lambda b, i: (0, 0))
    bvec = pl.BlockSpec((1, 1, d), lambda b, i: (b, 0, 0))
    if nxt is None:
        return pl.pallas_call(
            _resid_last_kernel,
            grid=(bsz, l // tm),
            in_specs=[row, row, bvec, vec],
            out_specs=row,
            out_shape=jax.ShapeDtypeStruct((bsz, l, d), F32),
            compiler_params=_cparams("parallel", "parallel"),
            name="resid_last",
        )(x, y, gate, g1.reshape(1, d)), None
    g2, sc, sh = nxt
    return pl.pallas_call(
        _resid_kernel,
        grid=(bsz, l // tm),
        in_specs=[row, row, bvec, vec, vec, bvec, bvec],
        out_specs=[row, row],
        out_shape=[jax.ShapeDtypeStruct((bsz, l, d), F32), jax.ShapeDtypeStruct((bsz, l, d), BF16)],
        compiler_params=_cparams("parallel", "parallel"),
        name="resid_norm",
    )(x, y, gate, g1.reshape(1, d), g2.reshape(1, d), sc, sh)


def _bias_kernel(rb_ref, o_ref):
    j = lax.broadcasted_iota(jnp.int32, (KEY_CHUNK, Q_BLOCK), 0)
    i = lax.broadcasted_iota(jnp.int32, (KEY_CHUNK, Q_BLOCK), 1)
    max_exact = N_BUCKETS // 2
    for v, off in enumerate((Q_BLOCK, 0, 2 * Q_BLOCK, None)):
        if off is None:
            dist = jnp.full((KEY_CHUNK, Q_BLOCK), 2 * MAX_DIST, jnp.int32)
        else:
            dist = jnp.maximum(i + off - j, 0)
        nf = jnp.maximum(dist, 1).astype(F32)
        large = max_exact + (jnp.log(nf / max_exact) / math.log(MAX_DIST / max_exact)
                             * (N_BUCKETS - max_exact)).astype(jnp.int32)
        large = jnp.minimum(large, N_BUCKETS - 1)
        bucket = jnp.where(dist < max_exact, dist, large)
        for h in range(H_A):
            def body(b, acc, h=h, bucket=bucket):
                return jnp.where(bucket == b, rb_ref[b * H_A + h] * LOG2E, acc)
            o_ref[v, h] = lax.fori_loop(0, N_BUCKETS, body, jnp.zeros((KEY_CHUNK, Q_BLOCK), F32))


def _bias_tiles(rel_bias):
    return pl.pallas_call(
        _bias_kernel,
        in_specs=[pl.BlockSpec(memory_space=pltpu.SMEM)],
        out_specs=pl.BlockSpec(memory_space=pltpu.VMEM),
        out_shape=jax.ShapeDtypeStruct((4, H_A, KEY_CHUNK, Q_BLOCK), F32),
        name="dsa_bias_tiles",
    )(rel_bias.reshape(-1))


def _dsa_prep_kernel(pa_ref, gq_ref, gkv_ref, wuq_ref, wuk_ref, wqi_ref,
                     qlat_ref, qidx_ref, widx_ref, kidx_ref, ckv_ref):
    pa = pa_ref[0]
    cq = pa[:, :R_Q]
    ckv = pa[:, R_Q:R_Q + R_KV]
    kid = pa[:, R_Q + R_KV:R_Q + R_KV + D_I]
    wid = pa[:, R_Q + R_KV + LANES:R_Q + R_KV + LANES + H_I]
    cqn = (_rms(cq) * gq_ref[...]).astype(BF16)
    ckv_ref[0] = (_rms(ckv) * gkv_ref[...]).astype(BF16)
    kidx_ref[0] = kid.astype(BF16)
    widx_ref[0] = wid * (H_I ** -0.5 * D_I ** -0.5)
    q = jnp.dot(cqn, wuq_ref[...], preferred_element_type=F32)
    for h in range(H_A):
        qh = q[:, h * DH_A:(h + 1) * DH_A].astype(BF16)
        ql = lax.dot_general(wuk_ref[h], qh, NT_DIMS, preferred_element_type=F32)
        qlat_ref[0, 0, :, h * Q_BLOCK:(h + 1) * Q_BLOCK] = (ql * (DH_A ** -0.5 * LOG2E)).astype(BF16)
    for h in range(H_I):
        qi = lax.dot_general(wqi_ref[h], cqn, NT_DIMS, preferred_element_type=F32)
        qidx_ref[0, 0, :, h * Q_BLOCK:(h + 1) * Q_BLOCK] = qi.astype(BF16)


def _dsa_prep(pa, gq, gkv, wuq, wuk, wqi):
    bsz, l, _ = pa.shape
    nb = l // Q_BLOCK
    full = lambda shp: pl.BlockSpec(shp, lambda b, i: (0,) * len(shp))
    return pl.pallas_call(
        _dsa_prep_kernel,
        grid=(bsz, nb),
        in_specs=[pl.BlockSpec((1, Q_BLOCK, A_PAD), lambda b, i: (b, i, 0)),
                  full((1, R_Q)), full((1, R_KV)), full((R_Q, H_A * DH_A)),
                  full((H_A, R_KV, DH_A)), full((H_I, D_I, R_Q))],
        out_specs=[pl.BlockSpec((1, 1, R_KV, H_A * Q_BLOCK), lambda b, i: (b, i, 0, 0)),
                   pl.BlockSpec((1, 1, D_I, H_I * Q_BLOCK), lambda b, i: (b, i, 0, 0)),
                   pl.BlockSpec((1, Q_BLOCK, H_I), lambda b, i: (b, i, 0)),
                   pl.BlockSpec((1, Q_BLOCK, D_I), lambda b, i: (b, i, 0)),
                   pl.BlockSpec((1, Q_BLOCK, R_KV), lambda b, i: (b, i, 0))],
        out_shape=[jax.ShapeDtypeStruct((bsz, nb, R_KV, H_A * Q_BLOCK), BF16),
                   jax.ShapeDtypeStruct((bsz, nb, D_I, H_I * Q_BLOCK), BF16),
                   jax.ShapeDtypeStruct((bsz, l, H_I), F32),
                   jax.ShapeDtypeStruct((bsz, l, D_I), BF16),
                   jax.ShapeDtypeStruct((bsz, l, R_KV), BF16)],
        compiler_params=_cparams("parallel", "parallel"),
        name="dsa_prep",
    )(pa, gq.reshape(1, R_Q), gkv.reshape(1, R_KV), wuq, wuk, wqi)


def _fold8(x, op, ways=1):
    n = x.shape[0] // 8
    ways = min(ways, n)
    acc = [x[g * 8:(g + 1) * 8] for g in range(ways)]
    for g in range(ways, n):
        acc[g % ways] = op(acc[g % ways], x[g * 8:(g + 1) * 8])
    while len(acc) > 1:
        acc = [op(acc[i], acc[i + 1]) for i in range(0, len(acc), 2)]
    return acc[0]


def _dsa_attn_kernel(qlat_ref, qidx_ref, widx_ref, kidx_ref, ckv_ref, ckvt_ref, wuvt_ref, bias_ref, o_ref,
                     key_ref, m_ref, ls_ref, acc_ref, sa_ref, sb_ref, lga_ref, lgb_ref, pa_ref, pb_ref,
                     aa_ref, ab_ref, *, topk):
    qi = pl.program_id(1)
    nch = (qi * Q_BLOCK) // KEY_CHUNK + 1
    krow = lax.broadcasted_iota(jnp.int32, (KEY_CHUNK, Q_BLOCK), 0)
    qlane = lax.broadcasted_iota(jnp.int32, (KEY_CHUNK, Q_BLOCK), 1)
    t_abs = qi * Q_BLOCK + qlane
    w = widx_ref[0, 0]

    npair = (nch + 1) // 2
    last_chunk = kidx_ref.shape[1] // KEY_CHUNK - 1

    def chunk_start(c):
        return pl.multiple_of(jnp.minimum(c, last_chunk) * KEY_CHUNK, KEY_CHUNK)

    def score_mm(c, s_ref):
        kc = kidx_ref[0, pl.ds(chunk_start(c), KEY_CHUNK), :]
        s_ref[...] = jnp.dot(kc, qidx_ref[0, 0], preferred_element_type=F32)

    def score_keys(c, s_ref):
        acc = jnp.zeros((KEY_CHUNK, Q_BLOCK), F32)
        for h in range(H_I):
            acc = acc + w[h:h + 1] * jnp.maximum(s_ref[:, h * Q_BLOCK:(h + 1) * Q_BLOCK], 0.0)
        acc = jnp.where(c * KEY_CHUNK + krow <= t_abs, acc, -jnp.inf)
        bits = lax.bitcast_convert_type(acc, jnp.int32)
        key_ref[c] = bits ^ ((bits >> 31) & 0x7FFFFFFF)

    def score_body(j, carry):
        c = 2 * j
        score_mm(c + 1, sb_ref)
        score_keys(c, sa_ref)
        score_mm(c + 2, sa_ref)
        score_keys(c + 1, sb_ref)
        return carry

    score_mm(0, sa_ref)
    lax.fori_loop(0, npair, score_body, 0)

    def count_ge(cand):
        def body(c, cnt):
            return (cnt + _fold8(jnp.where(key_ref[2 * c] >= cand, 1.0, 0.0), jnp.add, 4)
                    + _fold8(jnp.where(key_ref[2 * c + 1] >= cand, 1.0, 0.0), jnp.add, 4))
        cnt = lax.fori_loop(0, npair, body, jnp.zeros((8, Q_BLOCK), F32))
        for shift in (4, 2, 1):
            cnt = cnt + pltpu.roll(cnt, shift, 0)
        return cnt[0:1]

    kf = float(topk)
    cnt0 = count_ge(jnp.zeros((1, Q_BLOCK), jnp.int32))
    thr = jnp.where(cnt0 >= kf, 0, INT_MIN).astype(jnp.int32)
    cnt_thr = jnp.where(cnt0 >= kf, cnt0, ((nch + 1) // 2 * (2 * KEY_CHUNK)).astype(F32))
    few = (qi * Q_BLOCK + lax.broadcasted_iota(jnp.int32, (1, Q_BLOCK), 1)) < topk

    def unsettled(cnt_thr):
        return jnp.max(jnp.where(few | (cnt_thr == kf), 0.0, 1.0)) > 0.5

    def group_cond(st):
        return (st[0] < 8) & st[3]

    def group_body(st):
        g, thr, cnt_thr, _ = st
        for k in range(4):
            i = 4 * g + k
            bit = jnp.where(i < 31, jnp.left_shift(jnp.int32(1), jnp.maximum(30 - i, 0)), 0)
            cand = thr + bit
            cnt = count_ge(cand)
            take = cnt >= kf
            thr = jnp.where(take, cand, thr)
            cnt_thr = jnp.where(take, cnt, cnt_thr)
        return g + 1, thr, cnt_thr, unsettled(cnt_thr)

    thr = lax.while_loop(group_cond, group_body, (jnp.int32(0), thr, cnt_thr, unsettled(cnt_thr)))[1]

    odd = (qi % 2) == 1

    def logits_mm(c, lg_ref):
        kv = ckv_ref[0, pl.ds(chunk_start(c), KEY_CHUNK), :]
        lg_ref[...] = jnp.dot(kv, qlat_ref[0, 0], preferred_element_type=F32)

    def chunk_mask(c):
        sel = (key_ref[c] >= thr) & (c * KEY_CHUNK + krow <= t_abs)
        idx = jnp.where(c == nch - 1, jnp.where(odd, 0, 1),
                        jnp.where((c == nch - 2) & jnp.logical_not(odd), 2, 3))
        return sel, idx

    m_ref[...] = jnp.full(m_ref.shape, -1e29, F32)
    ls_ref[...] = jnp.zeros(ls_ref.shape, F32)
    acc_ref[...] = jnp.zeros(acc_ref.shape, F32)
    pb_ref[...] = jnp.zeros(pb_ref.shape, BF16)
    ab_ref[...] = jnp.ones(ab_ref.shape, F32)

    def chunk_probs(c, lg_ref, p_ref, a_ref):
        sel, idx = chunk_mask(c)
        for h in range(H_A):
            cs = slice(h * Q_BLOCK, (h + 1) * Q_BLOCK)
            lg = jnp.where(sel, lg_ref[:, cs] + bias_ref[idx, h], -1e30)
            cm = _fold8(lg, jnp.maximum)
            for shift in (4, 2, 1):
                cm = jnp.maximum(cm, pltpu.roll(cm, shift, 0))
            m_old = m_ref[:, cs]
            m_new = jnp.maximum(m_old, cm[0:1])
            alpha = jnp.exp2(m_old - m_new)
            p = jnp.exp2(lg - m_new)
            ls_ref[:, cs] = alpha * ls_ref[:, cs] + _fold8(p, jnp.add)
            m_ref[:, cs] = m_new
            a_ref[:, cs] = alpha
            p_ref[:, cs] = p.astype(BF16)

    def values_mm(c, p_ref, a_ref):
        acc_ref[...] = acc_ref[...] * a_ref[...] + jnp.dot(
            ckvt_ref[0, :, pl.ds(chunk_start(c), KEY_CHUNK)], p_ref[...], preferred_element_type=F32)

    def attn_body(j, carry):
        c = 2 * j
        logits_mm(c + 1, lgb_ref)
        chunk_probs(c, lga_ref, pa_ref, aa_ref)
        values_mm(jnp.maximum(c - 1, 0), pb_ref, ab_ref)
        logits_mm(c + 2, lga_ref)
        chunk_probs(c + 1, lgb_ref, pb_ref, ab_ref)
        values_mm(c, pa_ref, aa_ref)
        return carry

    logits_mm(0, lga_ref)
    lax.fori_loop(0, npair, attn_body, 0)
    values_mm(2 * npair - 1, pb_ref, ab_ref)
    lsum = jnp.sum(ls_ref[...], axis=0, keepdims=True)

    for h in range(H_A):
        cs = slice(h * Q_BLOCK, (h + 1) * Q_BLOCK)
        o_lat = (acc_ref[:, cs] / lsum[:, cs]).astype(BF16)
        out = jnp.dot(wuvt_ref[h], o_lat, preferred_element_type=F32)
        o_ref[0, h * DH_A:(h + 1) * DH_A, :] = out.astype(o_ref.dtype)


def _dsa_attn(qlat, qidx, widx, kidx, ckv, wuv, bias):
    bsz, l, _ = ckv.shape
    nb = l // Q_BLOCK
    topk = min(TOPK_MAX, l // 4)
    nchunks = 2 * ((l + 2 * KEY_CHUNK - 1) // (2 * KEY_CHUNK))
    hq = H_A * Q_BLOCK
    widx_t = widx.reshape(bsz, nb, Q_BLOCK, H_I).transpose(0, 1, 3, 2)
    ckv_t = ckv.transpose(0, 2, 1)
    wuv_t = wuv.transpose(0, 2, 1)
    return pl.pallas_call(
        functools.partial(_dsa_attn_kernel, topk=topk),
        grid=(bsz, nb),
        in_specs=[pl.BlockSpec((1, 1, R_KV, hq), lambda b, i: (b, i, 0, 0)),
                  pl.BlockSpec((1, 1, D_I, H_I * Q_BLOCK), lambda b, i: (b, i, 0, 0)),
                  pl.BlockSpec((1, 1, H_I, Q_BLOCK), lambda b, i: (b, i, 0, 0)),
                  pl.BlockSpec((1, l, D_I), lambda b, i: (b, 0, 0)),
                  pl.BlockSpec((1, l, R_KV), lambda b, i: (b, 0, 0)),
                  pl.BlockSpec((1, R_KV, l), lambda b, i: (b, 0, 0)),
                  pl.BlockSpec((H_A, DH_A, R_KV), lambda b, i: (0, 0, 0)),
                  pl.BlockSpec((4, H_A, KEY_CHUNK, Q_BLOCK), lambda b, i: (0, 0, 0, 0))],
        out_specs=pl.BlockSpec((1, H_A * DH_A, Q_BLOCK), lambda b, i: (b, 0, i)),
        out_shape=jax.ShapeDtypeStruct((bsz, H_A * DH_A, l), BF16),
        scratch_shapes=[pltpu.VMEM((nchunks, KEY_CHUNK, Q_BLOCK), jnp.int32),
                        pltpu.VMEM((1, hq), F32), pltpu.VMEM((8, hq), F32), pltpu.VMEM((R_KV, hq), F32),
                        pltpu.VMEM((KEY_CHUNK, H_I * Q_BLOCK), F32), pltpu.VMEM((KEY_CHUNK, H_I * Q_BLOCK), F32),
                        pltpu.VMEM((KEY_CHUNK, hq), F32), pltpu.VMEM((KEY_CHUNK, hq), F32),
                        pltpu.VMEM((KEY_CHUNK, hq), BF16), pltpu.VMEM((KEY_CHUNK, hq), BF16),
                        pltpu.VMEM((1, hq), F32), pltpu.VMEM((1, hq), F32)],
        compiler_params=_cparams("parallel", "arbitrary"),
        name="dsa_attn",
    )(qlat, qidx, widx_t, kidx, ckv, ckv_t, wuv_t, bias)


def _seg_sum(x, bd):
    hi = x.astype(BF16)
    lo = (x - hi.astype(F32)).astype(BF16)
    outs = []
    for p in range(x.shape[1] // LANES):
        sl = slice(p * LANES, (p + 1) * LANES)
        outs.append(jnp.dot(hi[:, sl], bd, preferred_element_type=F32)
                    + jnp.dot(lo[:, sl], bd, preferred_element_type=F32))
    return outs[0] if len(outs) == 1 else jnp.concatenate(outs, axis=1)


def _rwkv_pre_kernel(pb_ref, mu_ref, w0_ref, a0_ref, kk_ref, ka_ref, rk_ref, wup_ref, aup_ref, gup_ref,
                     bd_ref, tril_ref, ab_ref, bb_ref, kb_ref, rb_ref, v_ref, p_ref, g_ref, bonus_ref,
                     carry_ref):
    @pl.when(pl.program_id(1) == 0)
    def _():
        carry_ref[...] = jnp.zeros_like(carry_ref)

    x = pb_ref[0]
    tm = x.shape[0]
    rows = lax.broadcasted_iota(jnp.int32, x.shape, 0)
    prev = jnp.where(rows == 0, carry_ref[...], pltpu.roll(x, 1, 0))
    carry_ref[...] = x[tm - 1:tm]
    ps = x + mu_ref[...] * (prev - x)
    r = ps[:, :W_B]
    k = ps[:, W_B:2 * W_B]
    v = ps[:, 2 * W_B:3 * W_B]
    o = 3 * W_B
    wd = ps[:, o:o + LORA_W]
    ad = ps[:, o + LORA_W:o + LORA_W + LORA_A]
    gd = ps[:, o + LORA_W + LORA_A:]
    hp = lax.Precision.HIGHEST
    z = w0_ref[...] + jnp.dot(jnp.tanh(wd), wup_ref[...], precision=hp, preferred_element_type=F32)
    u = -z
    w = -(jnp.maximum(u, 0.0) + jnp.log(1.0 + jnp.exp(-jnp.abs(u)))) - 0.5
    logd = -jnp.exp(w)
    logp = jnp.dot(tril_ref[...], logd, precision=hp, preferred_element_type=F32)
    pinv = jnp.exp(-logp)
    a = _sigmoid(a0_ref[...] + jnp.dot(ad, aup_ref[...], precision=hp, preferred_element_type=F32))
    g_ref[0] = jnp.dot(_sigmoid(gd).astype(BF16), gup_ref[...], preferred_element_type=F32)
    bd = bd_ref[...]
    kk = k * kk_ref[...]
    kk = kk * lax.rsqrt(jnp.maximum(_seg_sum(kk * kk, bd), 1e-24))
    k2 = k * (1.0 + (a - 1.0) * ka_ref[...])
    p = jnp.exp(logp)
    ab_ref[0] = -kk * jnp.exp(logp - logd)
    bb_ref[0] = kk * a * pinv
    kb_ref[0] = k2 * pinv
    rb_ref[0] = r * p
    v_ref[0] = v
    p_ref[0] = p
    bonus_ref[0] = _seg_sum(r * k2 * rk_ref[...], bd) * v


def _rwkv_pre(pb, mu, w0, a0, k_k, k_a, r_k, w_up, a_up, g_up, bd, tm=128):
    bsz, l, _ = pb.shape
    tm = min(tm, l)
    assert tm % RW_CHUNK == 0
    tril = jnp.kron(jnp.eye(tm // RW_CHUNK, dtype=F32), jnp.tril(jnp.ones((RW_CHUNK, RW_CHUNK), F32)))
    full = lambda shp: pl.BlockSpec(shp, lambda b, i: (0,) * len(shp))
    row = pl.BlockSpec((1, tm, W_B), lambda b, i: (b, i, 0))
    vec = full((1, W_B))
    return pl.pallas_call(
        _rwkv_pre_kernel,
        grid=(bsz, l // tm),
        in_specs=[pl.BlockSpec((1, tm, B_PAD), lambda b, i: (b, i, 0)), full((1, B_PAD)),
                  vec, vec, vec, vec, vec,
                  full((LORA_W, W_B)), full((LORA_A, W_B)), full((LORA_G_PAD, W_B)), full((LANES, LANES)),
                  full((tm, tm))],
        out_specs=[row] * 8,
        out_shape=[jax.ShapeDtypeStruct((bsz, l, W_B), F32)] * 8,
        scratch_shapes=[pltpu.VMEM((1, B_PAD), F32)],
        compiler_params=_cparams("parallel", "arbitrary"),
        name="rwkv_pre",
    )(pb, mu, w0.reshape(1, W_B), a0.reshape(1, W_B), k_k.reshape(1, W_B), k_a.reshape(1, W_B),
      r_k.reshape(1, W_B), w_up, a_up, g_up, bd, tril)


def _mm(a, b, dims, passes):
    dg = lambda x, y: lax.dot_general(x, y, dims, preferred_element_type=F32)
    ah, bh = a.astype(BF16), b.astype(BF16)
    if passes == 1:
        return dg(ah, bh)
    al = (a - ah.astype(F32)).astype(BF16)
    bl = (b - bh.astype(F32)).astype(BF16)
    return dg(ah, bh) + (dg(ah, bl) + dg(al, bh))


def _rwkv_chunk_kernel(ab_ref, bb_ref, kb_ref, rb_ref, v_ref, pe_ref, y_ref, s_ref, *, npair):
    @pl.when(pl.program_id(1) == 0)
    def _():
        s_ref[...] = jnp.zeros_like(s_ref)

    c = RW_CHUNK
    ri = lax.broadcasted_iota(jnp.int32, (c, c), 0)
    ci = lax.broadcasted_iota(jnp.int32, (c, c), 1)
    strict, incl = ci < ri, ci <= ri
    eye = jnp.where(ri == ci, 1.0, 0.0)
    head0 = lax.broadcasted_iota(jnp.int32, (1, LANES), 1) < N_B
    bi = lax.broadcasted_iota(jnp.int32, (LANES, LANES), 0) < N_B
    bj = lax.broadcasted_iota(jnp.int32, (LANES, LANES), 1) < N_B
    blockdiag = bi == bj
    nt = (((2,), (2,)), ((0,), (0,)))
    nn = (((2,), (1,)), ((0,), (0,)))
    tn = (((1,), (1,)), ((0,), (0,)))
    tiles = lambda ref: jnp.stack([ref[0, :, p * LANES:(p + 1) * LANES] for p in range(npair)])
    ab, bb, kb, rb, v = tiles(ab_ref), tiles(bb_ref), tiles(kb_ref), tiles(rb_ref), tiles(v_ref)
    pe = jnp.stack([pe_ref[0, 0, :, p * LANES:(p + 1) * LANES] for p in range(npair)])
    s0 = s_ref[...]
    both = lambda x: jnp.concatenate([x, x], axis=0)
    pick = lambda x: jnp.where(head0, x[:npair], x[npair:])
    split = lambda x: jnp.concatenate([jnp.where(head0, x, 0.0), jnp.where(head0, 0.0, x)], axis=0)
    ar = jnp.concatenate([ab, rb], axis=1)
    ars = split(ar)
    gb = _mm(ars, both(bb), nt, RW_PASSES)
    gk = _mm(ars, both(kb), nt, RW_PASSES)
    m_ab = jnp.where(strict, gb[:, :c], 0.0)
    n_rb = jnp.where(incl, gb[:, c:], 0.0)
    m_ak = jnp.where(strict, gk[:, :c], 0.0)
    n_rk = jnp.where(incl, gk[:, c:], 0.0)
    tinv = eye + m_ab
    pw = m_ab
    for _ in range(5):
        pw = _mm(pw, pw, nn, RW_PASSES)
        tinv = tinv + _mm(tinv, pw, nn, RW_PASSES)
    x0 = _mm(ar, s0, nt, RW_PASSES)
    v2 = both(v)
    x = x0[:, :c] + pick(_mm(m_ak, v2, nn, RW_PASSES))
    u = pick(_mm(tinv, both(x), nn, RW_PASSES))
    y = x0[:, c:] + pick(_mm(n_rb, both(u), nn, RW_PASSES)) + pick(_mm(n_rk, v2, nn, RW_PASSES))
    for p in range(npair):
        y_ref[0, :, p * LANES:(p + 1) * LANES] = y[p]
    upd = _mm(jnp.concatenate([u, v], axis=1), jnp.concatenate([bb, kb], axis=1), tn, RW_PASSES)
    s_ref[...] = (s0 + jnp.where(blockdiag, upd, 0.0)) * pe


def _rwkv_chunks(ab, bb, kb, rb, v, pe):
    bsz, l, wb = ab.shape
    nt = l // RW_CHUNK
    npair = wb // LANES
    row = pl.BlockSpec((1, RW_CHUNK, wb), lambda b, i: (b, i, 0))
    return pl.pallas_call(
        functools.partial(_rwkv_chunk_kernel, npair=npair),
        grid=(bsz, nt),
        in_specs=[row] * 5 + [pl.BlockSpec((1, 1, 1, wb), lambda b, i: (b, i, 0, 0))],
        out_specs=row,
        out_shape=jax.ShapeDtypeStruct((bsz, l, wb), F32),
        scratch_shapes=[pltpu.VMEM((npair, LANES, LANES), F32)],
        compiler_params=_cparams("parallel", "arbitrary"),
        name="rwkv_chunks",
    )(ab, bb, kb, rb, v, pe)


def _rwkv_post_kernel(y_ref, bonus_ref, g_ref, lw_ref, lb_ref, bd_ref, o_ref):
    bd = bd_ref[...]
    y = y_ref[0]
    mean = _seg_sum(y, bd) * (1.0 / N_B)
    c = y - mean
    var = _seg_sum(c * c, bd) * (1.0 / N_B)
    yn = c * lax.rsqrt(var + GN_EPS) * lw_ref[...] + lb_ref[...]
    o_ref[0] = ((yn + bonus_ref[0]) * g_ref[0]).astype(o_ref.dtype)


def _rwkv_post(y, bonus, g, lnx_w, lnx_b, bd, tm=256):
    bsz, l, _ = y.shape
    tm = min(tm, l)
    row = pl.BlockSpec((1, tm, W_B), lambda b, i: (b, i, 0))
    vec = pl.BlockSpec((1, W_B), lambda b, i: (0, 0))
    return pl.pallas_call(
        _rwkv_post_kernel,
        grid=(bsz, l // tm),
        in_specs=[row, row, row, vec, vec, pl.BlockSpec((LANES, LANES), lambda b, i: (0, 0))],
        out_specs=row,
        out_shape=jax.ShapeDtypeStruct((bsz, l, W_B), BF16),
        compiler_params=_cparams("parallel", "parallel"),
        name="rwkv_post",
    )(y, bonus, g, lnx_w.reshape(1, W_B), lnx_b.reshape(1, W_B), bd)


def _rwkv_mixer(pb, mu, w0, w_up, a0, a_up, g_up, k_k, k_a, r_k, lnx_w, lnx_b):
    bsz, l, _ = pb.shape
    nt = l // RW_CHUNK
    eye2 = jnp.kron(jnp.eye(2, dtype=F32), jnp.ones((N_B, N_B), F32)).astype(BF16)
    ab, bb, kb, rb, v, p, g, bonus = _rwkv_pre(pb, mu, w0, a0, k_k, k_a, r_k, w_up, a_up, g_up, eye2)
    pe = p[:, RW_CHUNK - 1::RW_CHUNK].reshape(bsz, nt, 1, W_B)
    y = _rwkv_chunks(ab, bb, kb, rb, v, pe)
    return _rwkv_post(y, bonus, g, lnx_w, lnx_b, eye2)


def _hgrn_kernel(q_ref, f_ref, i_ref, g_ref, lb_ref, on_ref, tril_ref, o_ref, st_ref, *, layer, hb, tt):
    @pl.when(pl.program_id(2) == 0)
    def _():
        st_ref[...] = jnp.zeros_like(st_ref)

    x = lb_ref[...]
    e = jnp.exp(x - jnp.max(x, axis=0, keepdims=True))
    sm = e / jnp.sum(e, axis=0, keepdims=True)
    cs = sm[0:1]
    for i in range(1, layer + 1):
        cs = cs + sm[i:i + 1]
    lb_all = cs - sm[0:1]
    rowi = lax.broadcasted_iota(jnp.int32, (SUB, LANES), 0)
    lanei = lax.broadcasted_iota(jnp.int32, (SUB, LANES), 1)
    prep = []
    for hh in range(hb):
        sl = slice(hh * LANES, (hh + 1) * LANES)
        lb = lb_all[:, sl]
        qraw = q_ref[0][:, sl]
        ff = f_ref[0][:, sl]
        q = qraw * _sigmoid(qraw)
        logsig = jnp.minimum(ff, 0.0) - jnp.log(1.0 + jnp.exp(-jnp.abs(ff)))
        la = jnp.log(lb)
        lbv = jnp.log(1.0 - lb) + logsig
        mx = jnp.maximum(la, lbv)
        logf = mx + jnp.log(jnp.exp(la - mx) + jnp.exp(lbv - mx))
        k = (1.0 - lb) * _sigmoid(-ff)
        bcum = jnp.dot(tril_ref[...], logf * LOG2E, precision=lax.Precision.HIGHEST, preferred_element_type=F32)
        prep.append((q, k, i_ref[0][:, sl], bcum))
    sts = [st_ref[hh] for hh in range(hb)]
    outs = [[] for _ in range(hb)]
    for n in range(tt // SUB):
        rs = slice(n * SUB, (n + 1) * SUB)
        for hh in range(hb):
            q, k, v, bcum = prep[hh]
            st = sts[hh]
            bq, qn, kn, vn = bcum[rs], q[rs], k[rs], v[rs]
            bend = bq[SUB - 1:SUB]
            qd = (qn * jnp.exp2(bq)).astype(BF16)
            o = lax.dot_general(qd, st.astype(BF16), NT_DIMS, preferred_element_type=F32)
            a = jnp.zeros((SUB, LANES), F32)
            for s in range(SUB):
                wgt = kn[s:s + 1] * jnp.exp2(jnp.minimum(bq - bq[s:s + 1], 0.0))
                a = jnp.where(lanei == s, jnp.sum(qn * wgt, axis=1, keepdims=True), a)
            a = jnp.where(lanei <= rowi, a, 0.0)[:, :SUB]
            o = o + jnp.dot(a.astype(BF16), vn.astype(BF16), preferred_element_type=F32)
            kd = (kn * jnp.exp2(bend - bq)).astype(BF16)
            sts[hh] = st * jnp.exp2(bend) + lax.dot_general(vn.astype(BF16), kd, TN_DIMS,
                                                           preferred_element_type=F32)
            outs[hh].append(o)
    for hh in range(hb):
        sl = slice(hh * LANES, (hh + 1) * LANES)
        st_ref[hh] = sts[hh]
        gg = g_ref[0][:, sl]
        o = jnp.concatenate(outs[hh], axis=0)
        o = _rms(o) * on_ref[:, sl] * (gg * _sigmoid(gg))
        o_ref[0, :, sl] = o.astype(o_ref.dtype)


def _hgrn_mixer(pc, hgrn_lb, onorm, layer, hb=12, tt=128):
    bsz, l, _ = pc.shape
    depth = hgrn_lb.shape[0]
    tt = min(tt, l)
    ng = H_C // hb
    wblk = hb * LANES
    tril = jnp.kron(jnp.eye(tt // SUB, dtype=F32), jnp.tril(jnp.ones((SUB, SUB), F32)))
    col = lambda j: pl.BlockSpec((1, tt, wblk), lambda b, h, i, j=j: (b, i, j * ng + h))
    return pl.pallas_call(
        functools.partial(_hgrn_kernel, layer=layer, hb=hb, tt=tt),
        grid=(bsz, ng, l // tt),
        in_specs=[col(0), col(1), col(2), col(3),
                  pl.BlockSpec((depth, wblk), lambda b, h, i: (0, h)),
                  pl.BlockSpec((1, wblk), lambda b, h, i: (0, h)),
                  pl.BlockSpec((tt, tt), lambda b, h, i: (0, 0))],
        out_specs=pl.BlockSpec((1, tt, wblk), lambda b, h, i: (b, i, h)),
        out_shape=jax.ShapeDtypeStruct((bsz, l, W_C), BF16),
        scratch_shapes=[pltpu.VMEM((hb, DV_C, DK_C), F32)],
        compiler_params=_cparams("parallel", "parallel", "arbitrary"),
        name="hgrn2",
    )(pc, pc, pc, pc, hgrn_lb, onorm.reshape(1, W_C), tril)


def _pack_w_in(w):
    d = w.shape[0]
    a_cols = R_Q + R_KV + D_I + H_I
    b_cols = 3 * W_B + LORA_W + LORA_A + LORA_G
    wa, wb, wc = w[:, :a_cols], w[:, a_cols:a_cols + b_cols], w[:, a_cols + b_cols:]
    z = lambda n: jnp.zeros((d, n), w.dtype)
    o = R_Q + R_KV
    wa = jnp.concatenate([wa[:, :o + D_I], z(LANES - D_I), wa[:, o + D_I:], z(LANES - H_I)], axis=1)
    wb = jnp.concatenate([wb, z(LORA_G_PAD - LORA_G)], axis=1)
    return wa.astype(BF16), wb.astype(BF16), wc.astype(BF16)


def kernel(x, c, rel_bias, hgrn_lb, ada_w, ada_b, norm_g, w_in, w_out, mla_q_norm, mla_kv_norm, w_uq, w_uk, w_uv, w_qidx, rwkv_mu, rwkv_w0, rwkv_w_up, rwkv_a0, rwkv_a_up, rwkv_g_up, rwkv_k_k, rwkv_k_a, rwkv_r_k, rwkv_lnx_w, rwkv_lnx_b, hgrn_onorm, w_ff1, w_ff2):
    bsz, l, d = x.shape
    depth = ada_w.shape[0]
    m = bsz * l
    mod = _ada_mod(c, ada_w, ada_b)
    mods = [[mod[i, :, None, j * d:(j + 1) * d] for j in range(6)] for i in range(depth)]
    bias = _bias_tiles(rel_bias)
    h = _norm_mod(x, norm_g[0, 0], mods[0][1], mods[0][0])
    for i in range(depth):
        sh_m, sc_m, g_m, sh_f, sc_f, g_f = mods[i]
        wa, wb, wc = _pack_w_in(w_in[i])
        h2d = h.reshape(m, d)
        pa = _matmul(h2d, wa, F32, 512, A_PAD, 4096, name="in_proj_a").reshape(bsz, l, -1)
        pb = _matmul(h2d, wb, F32, 1024, 768, 4096, name="in_proj_b").reshape(bsz, l, -1)
        pc = _matmul(h2d, wc, F32, 1024, 1024, 4096, name="in_proj_c").reshape(bsz, l, -1)
        wuq = w_uq[i].reshape(R_Q, H_A * DH_A).astype(BF16)
        wqi = jnp.transpose(w_qidx[i], (1, 2, 0)).astype(BF16)
        qlat, qidx, widx, kidx, ckv = _dsa_prep(pa, mla_q_norm[i], mla_kv_norm[i], wuq,
                                                w_uk[i].astype(BF16), wqi)
        y_a = _dsa_attn(qlat, qidx, widx, kidx, ckv, w_uv[i].astype(BF16), bias).transpose(0, 2, 1)
        mu = jnp.concatenate([rwkv_mu[i], jnp.zeros((LORA_G_PAD - LORA_G,), F32)]).reshape(1, B_PAD)
        gup = jnp.concatenate([rwkv_g_up[i], jnp.zeros((LORA_G_PAD - LORA_G, W_B), F32)]).astype(BF16)
        y_b = _rwkv_mixer(pb, mu, rwkv_w0[i], rwkv_w_up[i], rwkv_a0[i], rwkv_a_up[i], gup,
                          rwkv_k_k[i], rwkv_k_a[i], rwkv_r_k[i], rwkv_lnx_w[i], rwkv_lnx_b[i])
        y_c = _hgrn_mixer(pc, hgrn_lb, hgrn_onorm[i], i)
        ycat = jnp.concatenate([y_a, y_b, y_c], axis=-1).reshape(m, -1)
        y = _matmul(ycat, w_out[i].astype(BF16), F32, 1024, 1024, 4096, name="out_proj").reshape(bsz, l, d)
        x, h = _resid_norm(x, y, g_m, norm_g[i, 1], (norm_g[i, 2], sc_f, sh_f))
        u = _matmul(h.reshape(m, d), w_ff1[i].astype(BF16), BF16, 1024, 1024, 4096, relu2=True, name="ffn_up")
        y = _matmul(u, w_ff2[i].astype(BF16), F32, 1024, 1024, 4096, name="ffn_down").reshape(bsz, l, d)
        nxt = None if i == depth - 1 else (norm_g[i + 1, 0], mods[i + 1][1], mods[i + 1][0])
        x, h = _resid_norm(x, y, g_f, norm_g[i, 3], nxt)
    return x
```

```python
import functools
import math

import jax
import jax.numpy as jnp
from jax import lax
from jax.experimental import pallas as pl
from jax.experimental.pallas import tpu as pltpu

H_A, DH_A, R_Q, R_KV, H_I, D_I = 8, 128, 768, 256, 16, 64
TOPK_MAX, Q_BLOCK, N_BUCKETS, MAX_DIST = 256, 128, 32, 128
H_B, N_B, LORA_W, LORA_A, LORA_G = 24, 64, 128, 128, 480
W_B = H_B * N_B
H_C, DK_C, DV_C = 12, 128, 128
W_C = H_C * DK_C
EPS = 1e-6
GN_EPS = 64e-5

LANES = 128
KEY_CHUNK = 256
SUB = 8
RW_CHUNK = 64
RW_PASSES = 1
LORA_G_PAD = 512
A_PAD = 1280
B_PAD = 3 * W_B + LORA_W + LORA_A + LORA_G_PAD
VMEM_LIMIT = 56 * 1024 * 1024

F32 = jnp.float32
BF16 = jnp.bfloat16
NT_DIMS = (((1,), (1,)), ((), ()))
TN_DIMS = (((0,), (0,)), ((), ()))
INT_MIN = -2 ** 31
LOG2E = math.log2(math.e)


def _cparams(*sem):
    return pltpu.CompilerParams(dimension_semantics=sem, vmem_limit_bytes=VMEM_LIMIT)


def _sigmoid(x):
    return 1.0 / (1.0 + jnp.exp(-x))


def _rms(x, eps=EPS):
    return x * lax.rsqrt(jnp.mean(x * x, axis=-1, keepdims=True) + eps)


def _mm_kernel(a_ref, b_ref, o_ref, acc_ref, *, nk, relu2):
    k = pl.program_id(2)

    @pl.when(k == 0)
    def _():
        acc_ref[...] = jnp.zeros_like(acc_ref)

    acc_ref[...] += jnp.dot(a_ref[...], b_ref[...], preferred_element_type=F32)

    @pl.when(k == nk - 1)
    def _():
        r = acc_ref[...]
        if relu2:
            r = jnp.square(jnp.maximum(r, 0.0))
        o_ref[...] = r.astype(o_ref.dtype)


def _mm_full_k_kernel(a_ref, b_ref, o_ref, *, relu2):
    r = jnp.dot(a_ref[...], b_ref[...], preferred_element_type=F32)
    if relu2:
        r = jnp.square(jnp.maximum(r, 0.0))
    o_ref[...] = r.astype(o_ref.dtype)


def _matmul(a, b, out_dtype, tm, tn, tk, relu2=False, name="matmul"):
    m, kdim = a.shape
    _, n = b.shape
    tm, tn, tk = min(tm, m), min(tn, n), min(tk, kdim)
    assert m % tm == 0 and n % tn == 0 and kdim % tk == 0, (a.shape, b.shape, tm, tn, tk)
    nk = kdim // tk
    if nk == 1:
        return pl.pallas_call(
            functools.partial(_mm_full_k_kernel, relu2=relu2),
            grid=(m // tm, n // tn),
            in_specs=[pl.BlockSpec((tm, kdim), lambda i, j: (i, 0)),
                      pl.BlockSpec((kdim, tn), lambda i, j: (0, j))],
            out_specs=pl.BlockSpec((tm, tn), lambda i, j: (i, j)),
            out_shape=jax.ShapeDtypeStruct((m, n), out_dtype),
            compiler_params=_cparams("parallel", "parallel"),
            name=name,
        )(a, b)
    return pl.pallas_call(
        functools.partial(_mm_kernel, nk=nk, relu2=relu2),
        grid=(m // tm, n // tn, nk),
        in_specs=[pl.BlockSpec((tm, tk), lambda i, j, k: (i, k)),
                  pl.BlockSpec((tk, tn), lambda i, j, k: (k, j))],
        out_specs=pl.BlockSpec((tm, tn), lambda i, j, k: (i, j)),
        out_shape=jax.ShapeDtypeStruct((m, n), out_dtype),
        scratch_shapes=[pltpu.VMEM((tm, tn), F32)],
        compiler_params=_cparams("parallel", "parallel", "arbitrary"),
        name=name,
    )(a, b)


def _ada_kernel(c_ref, w_ref, b_ref, o_ref):
    k = pl.program_id(2)

    @pl.when(k == 0)
    def _():
        o_ref[0] = jnp.broadcast_to(b_ref[0], o_ref.shape[1:])

    c = c_ref[...]
    ca = (c * _sigmoid(c)).astype(BF16)
    o_ref[0] += jnp.dot(ca, w_ref[0].astype(BF16), preferred_element_type=F32)


def _ada_mod(c, ada_w, ada_b):
    nl, d, n = ada_w.shape
    bsz = c.shape[0]
    rows = 8
    cp = jnp.zeros((rows, d), F32).at[:bsz].set(c)
    tn, tk = min(2048, n), min(1024, d)
    out = pl.pallas_call(
        _ada_kernel,
        grid=(nl, n // tn, d // tk),
        in_specs=[pl.BlockSpec((rows, tk), lambda l, j, k: (0, k)),
                  pl.BlockSpec((1, tk, tn), lambda l, j, k: (l, k, j)),
                  pl.BlockSpec((1, 1, tn), lambda l, j, k: (l, 0, j))],
        out_specs=pl.BlockSpec((1, rows, tn), lambda l, j, k: (l, 0, j)),
        out_shape=jax.ShapeDtypeStruct((nl, rows, n), F32),
        compiler_params=_cparams("parallel", "parallel", "arbitrary"),
        name="ada_mod",
    )(cp, ada_w, ada_b.reshape(nl, 1, n))
    return out[:, :bsz]


def _norm_mod_kernel(x_ref, g_ref, sc_ref, sh_ref, h_ref):
    y = _rms(x_ref[0]) * g_ref[...]
    h_ref[0] = (y * (1.0 + sc_ref[0]) + sh_ref[0]).astype(h_ref.dtype)


def _norm_mod(x, g, sc, sh, tm=256):
    bsz, l, d = x.shape
    tm = min(tm, l)
    row = pl.BlockSpec((1, tm, d), lambda b, i: (b, i, 0))
    vec = pl.BlockSpec((1, d), lambda b, i: (0, 0))
    bvec = pl.BlockSpec((1, 1, d), lambda b, i: (b, 0, 0))
    return pl.pallas_call(
        _norm_mod_kernel,
        grid=(bsz, l // tm),
        in_specs=[row, vec, bvec, bvec],
        out_specs=row,
        out_shape=jax.ShapeDtypeStruct((bsz, l, d), BF16),
        compiler_params=_cparams("parallel", "parallel"),
        name="norm_mod",
    )(x, g.reshape(1, d), sc, sh)


def _resid_kernel(x_ref, y_ref, gate_ref, g1_ref, g2_ref, sc_ref, sh_ref, xo_ref, h_ref):
    xn = x_ref[0] + gate_ref[0] * (_rms(y_ref[0]) * g1_ref[...])
    xo_ref[0] = xn
    hn = _rms(xn) * g2_ref[...]
    h_ref[0] = (hn * (1.0 + sc_ref[0]) + sh_ref[0]).astype(h_ref.dtype)


def _resid_last_kernel(x_ref, y_ref, gate_ref, g1_ref, xo_ref):
    xo_ref[0] = x_ref[0] + gate_ref[0] * (_rms(y_ref[0]) * g1_ref[...])


def _resid_norm(x, y, gate, g1, nxt=None, tm=128):
    bsz, l, d = x.shape
    tm = min(tm, l)
    row = pl.BlockSpec((1, tm, d), lambda b, i: (b, i, 0))
    vec = pl.BlockSpec((1, d), lambda b, i: (0, 0))
    bvec = pl.BlockSpec((1, 1, d), lambda b, i: (b, 0, 0))
    if nxt is None:
        return pl.pallas_call(
            _resid_last_kernel,
            grid=(bsz, l // tm),
            in_specs=[row, row, bvec, vec],
            out_specs=row,
            out_shape=jax.ShapeDtypeStruct((bsz, l, d), F32),
            compiler_params=_cparams("parallel", "parallel"),
            name="resid_last",
        )(x, y, gate, g1.reshape(1, d)), None
    g2, sc, sh = nxt
    return pl.pallas_call(
        _resid_kernel,
        grid=(bsz, l // tm),
        in_specs=[row, row, bvec, vec, vec, bvec, bvec],
        out_specs=[row, row],
        out_shape=[jax.ShapeDtypeStruct((bsz, l, d), F32), jax.ShapeDtypeStruct((bsz, l, d), BF16)],
        compiler_params=_cparams("parallel", "parallel"),
        name="resid_norm",
    )(x, y, gate, g1.reshape(1, d), g2.reshape(1, d), sc, sh)


def _bias_kernel(rb_ref, o_ref):
    j = lax.broadcasted_iota(jnp.int32, (KEY_CHUNK, Q_BLOCK), 0)
    i = lax.broadcasted_iota(jnp.int32, (KEY_CHUNK, Q_BLOCK), 1)
    max_exact = N_BUCKETS // 2
    for v, off in enumerate((Q_BLOCK, 0, 2 * Q_BLOCK, None)):
        if off is None:
            dist = jnp.full((KEY_CHUNK, Q_BLOCK), 2 * MAX_DIST, jnp.int32)
        else:
            dist = jnp.maximum(i + off - j, 0)
        nf = jnp.maximum(dist, 1).astype(F32)
        large = max_exact + (jnp.log(nf / max_exact) / math.log(MAX_DIST / max_exact)
                             * (N_BUCKETS - max_exact)).astype(jnp.int32)
        large = jnp.minimum(large, N_BUCKETS - 1)
        bucket = jnp.where(dist < max_exact, dist, large)
        for h in range(H_A):
            def body(b, acc, h=h, bucket=bucket):
                return jnp.where(bucket == b, rb_ref[b * H_A + h] * LOG2E, acc)
            o_ref[v, h] = lax.fori_loop(0, N_BUCKETS, body, jnp.zeros((KEY_CHUNK, Q_BLOCK), F32))


def _bias_tiles(rel_bias):
    return pl.pallas_call(
        _bias_kernel,
        in_specs=[pl.BlockSpec(memory_space=pltpu.SMEM)],
        out_specs=pl.BlockSpec(memory_space=pltpu.VMEM),
        out_shape=jax.ShapeDtypeStruct((4, H_A, KEY_CHUNK, Q_BLOCK), F32),
        name="dsa_bias_tiles",
    )(rel_bias.reshape(-1))


def _dsa_prep_kernel(pa_ref, gq_ref, gkv_ref, wuq_ref, wuk_ref, wqi_ref,
                     qlat_ref, qidx_ref, widx_ref, kidx_ref, ckv_ref):
    pa = pa_ref[0]
    cq = pa[:, :R_Q]
    ckv = pa[:, R_Q:R_Q + R_KV]
    kid = pa[:, R_Q + R_KV:R_Q + R_KV + D_I]
    wid = pa[:, R_Q + R_KV + LANES:R_Q + R_KV + LANES + H_I]
    cqn = (_rms(cq) * gq_ref[...]).astype(BF16)
    ckv_ref[0] = (_rms(ckv) * gkv_ref[...]).astype(BF16)
    kidx_ref[0] = kid.astype(BF16)
    widx_ref[0] = wid * (H_I ** -0.5 * D_I ** -0.5)
    q = jnp.dot(cqn, wuq_ref[...], preferred_element_type=F32)
    for h in range(H_A):
        qh = q[:, h * DH_A:(h + 1) * DH_A].astype(BF16)
        ql = lax.dot_general(wuk_ref[h], qh, NT_DIMS, preferred_element_type=F32)
        qlat_ref[0, 0, :, h * Q_BLOCK:(h + 1) * Q_BLOCK] = (ql * (DH_A ** -0.5 * LOG2E)).astype(BF16)
    for h in range(H_I):
        qi = lax.dot_general(wqi_ref[h], cqn, NT_DIMS, preferred_element_type=F32)
        qidx_ref[0, 0, :, h * Q_BLOCK:(h + 1) * Q_BLOCK] = qi.astype(BF16)


def _dsa_prep(pa, gq, gkv, wuq, wuk, wqi):
    bsz, l, _ = pa.shape
    nb = l // Q_BLOCK
    full = lambda shp: pl.BlockSpec(shp, lambda b, i: (0,) * len(shp))
    return pl.pallas_call(
        _dsa_prep_kernel,
        grid=(bsz, nb),
        in_specs=[pl.BlockSpec((1, Q_BLOCK, A_PAD), lambda b, i: (b, i, 0)),
                  full((1, R_Q)), full((1, R_KV)), full((R_Q, H_A * DH_A)),
                  full((H_A, R_KV, DH_A)), full((H_I, D_I, R_Q))],
        out_specs=[pl.BlockSpec((1, 1, R_KV, H_A * Q_BLOCK), lambda b, i: (b, i, 0, 0)),
                   pl.BlockSpec((1, 1, D_I, H_I * Q_BLOCK), lambda b, i: (b, i, 0, 0)),
                   pl.BlockSpec((1, Q_BLOCK, H_I), lambda b, i: (b, i, 0)),
                   pl.BlockSpec((1, Q_BLOCK, D_I), lambda b, i: (b, i, 0)),
                   pl.BlockSpec((1, Q_BLOCK, R_KV), lambda b, i: (b, i, 0))],
        out_shape=[jax.ShapeDtypeStruct((bsz, nb, R_KV, H_A * Q_BLOCK), BF16),
                   jax.ShapeDtypeStruct((bsz, nb, D_I, H_I * Q_BLOCK), BF16),
                   jax.ShapeDtypeStruct((bsz, l, H_I), F32),
                   jax.ShapeDtypeStruct((bsz, l, D_I), BF16),
                   jax.ShapeDtypeStruct((bsz, l, R_KV), BF16)],
        compiler_params=_cparams("parallel", "parallel"),
        name="dsa_prep",
    )(pa, gq.reshape(1, R_Q), gkv.reshape(1, R_KV), wuq, wuk, wqi)


def _fold8(x, op, ways=1):
    n = x.shape[0] // 8
    ways = min(ways, n)
    acc = [x[g * 8:(g + 1) * 8] for g in range(ways)]
    for g in range(ways, n):
        acc[g % ways] = op(acc[g % ways], x[g * 8:(g + 1) * 8])
    while len(acc) > 1:
        acc = [op(acc[i], acc[i + 1]) for i in range(0, len(acc), 2)]
    return acc[0]


def _dsa_attn_kernel(qlat_ref, qidx_ref, widx_ref, kidx_ref, ckv_ref, ckvt_ref, wuvt_ref, bias_ref, o_ref,
                     key_ref, hi_ref, lo_ref, m_ref, ls_ref, acc_ref, sa_ref, sb_ref, lga_ref, lgb_ref,
                     pa_ref, pb_ref, aa_ref, ab_ref, *, topk):
    qi = pl.program_id(1)
    nch = (qi * Q_BLOCK) // KEY_CHUNK + 1
    krow = lax.broadcasted_iota(jnp.int32, (KEY_CHUNK, Q_BLOCK), 0)
    qlane = lax.broadcasted_iota(jnp.int32, (KEY_CHUNK, Q_BLOCK), 1)
    t_abs = qi * Q_BLOCK + qlane
    w = widx_ref[0, 0]

    npair = (nch + 1) // 2
    last_chunk = kidx_ref.shape[1] // KEY_CHUNK - 1

    def chunk_start(c):
        return pl.multiple_of(jnp.minimum(c, last_chunk) * KEY_CHUNK, KEY_CHUNK)

    def score_mm(c, s_ref):
        kc = kidx_ref[0, pl.ds(chunk_start(c), KEY_CHUNK), :]
        s_ref[...] = jnp.dot(kc, qidx_ref[0, 0], preferred_element_type=F32)

    def score_keys(c, s_ref):
        acc = jnp.zeros((KEY_CHUNK, Q_BLOCK), F32)
        for h in range(H_I):
            acc = acc + w[h:h + 1] * jnp.maximum(s_ref[:, h * Q_BLOCK:(h + 1) * Q_BLOCK], 0.0)
        acc = jnp.where(c * KEY_CHUNK + krow <= t_abs, acc, -jnp.inf)
        bits = lax.bitcast_convert_type(acc, jnp.int32)
        key = bits ^ ((bits >> 31) & 0x7FFFFFFF)
        key_ref[c] = key
        hi_ref[c] = (key >> 16).astype(jnp.int16)
        lo_ref[c] = ((key & 0xFFFF) - 0x8000).astype(jnp.int16)

    def score_body(j, carry):
        c = 2 * j
        score_mm(c + 1, sb_ref)
        score_keys(c, sa_ref)
        score_mm(c + 2, sa_ref)
        score_keys(c + 1, sb_ref)
        return carry

    score_mm(0, sa_ref)
    lax.fori_loop(0, npair, score_body, 0)

    def count_ge16(ref, cand):
        c16 = cand.astype(jnp.int16)
        rows = 16

        def fold(x):
            r = x[0:rows]
            for g in range(1, x.shape[0] // rows):
                r = r + x[g * rows:(g + 1) * rows]
            return r

        def hits(c):
            return fold(jnp.where(ref[c] >= c16, jnp.int16(1), jnp.int16(0)))

        def quad(c, cnt):
            return cnt + ((hits(4 * c) + hits(4 * c + 1)) + (hits(4 * c + 2) + hits(4 * c + 3)))

        def tail(_, cnt):
            return cnt + (hits(2 * npair - 2) + hits(2 * npair - 1))

        cnt = lax.fori_loop(0, npair // 2, quad, jnp.zeros((rows, Q_BLOCK), jnp.int16))
        cnt = lax.fori_loop(0, npair % 2, tail, cnt)
        return jnp.sum(cnt.astype(F32), axis=0, keepdims=True)

    kf = float(topk)
    total = (npair * (2 * KEY_CHUNK)).astype(F32)
    few = (qi * Q_BLOCK + lax.broadcasted_iota(jnp.int32, (1, Q_BLOCK), 1)) < topk

    def unsettled(cnt_thr):
        return jnp.max(jnp.where(few | (cnt_thr == kf), 0.0, 1.0)) > 0.5

    def search16(ref, early_exit, cnt_min):
        cnt0 = count_ge16(ref, jnp.zeros((1, Q_BLOCK), jnp.int32))
        thr = jnp.where(cnt0 >= kf, 0, -0x8000).astype(jnp.int32)
        cnt_thr = jnp.where(cnt0 >= kf, cnt0, cnt_min)

        def group_cond(st):
            return (st[0] < 4) & st[3]

        def group_body(st):
            g, thr, cnt_thr, _ = st
            for k in range(4):
                i = 4 * g + k
                bit = jnp.where(i < 15, jnp.left_shift(jnp.int32(1), jnp.maximum(14 - i, 0)), 0)
                cand = thr + bit
                cnt = count_ge16(ref, cand)
                take = cnt >= kf
                thr = jnp.where(take, cand, thr)
                cnt_thr = jnp.where(take, cnt, cnt_thr)
            return g + 1, thr, cnt_thr, (unsettled(cnt_thr) if early_exit else jnp.bool_(True))

        go = unsettled(cnt_thr) if early_exit else jnp.bool_(True)
        _, thr, cnt_thr, _ = lax.while_loop(group_cond, group_body, (jnp.int32(0), thr, cnt_thr, go))
        return thr, cnt_thr

    thr_hi, cnt_hi = search16(hi_ref, False, total)
    thi16 = thr_hi.astype(jnp.int16)

    def relabel(c, carry):
        for cc in (2 * c, 2 * c + 1):
            hi = hi_ref[cc]
            lo_ref[cc] = jnp.where(hi > thi16, jnp.int16(0x7FFF),
                                   jnp.where(hi == thi16, lo_ref[cc], jnp.int16(-0x8000)))
        return carry

    lax.fori_loop(0, npair, relabel, 0)
    thr_lo, _ = search16(lo_ref, True, cnt_hi)
    thr = thr_hi * 65536 + (thr_lo + 0x8000)

    odd = (qi % 2) == 1

    def logits_mm(c, lg_ref):
        kv = ckv_ref[0, pl.ds(chunk_start(c), KEY_CHUNK), :]
        lg_ref[...] = jnp.dot(kv, qlat_ref[0, 0], preferred_element_type=F32)

    def chunk_mask(c):
        sel = (key_ref[c] >= thr) & (c * KEY_CHUNK + krow <= t_abs)
        idx = jnp.where(c == nch - 1, jnp.where(odd, 0, 1),
                        jnp.where((c == nch - 2) & jnp.logical_not(odd), 2, 3))
        return sel, idx

    m_ref[...] = jnp.full(m_ref.shape, -1e29, F32)
    ls_ref[...] = jnp.zeros(ls_ref.shape, F32)
    acc_ref[...] = jnp.zeros(acc_ref.shape, F32)
    pb_ref[...] = jnp.zeros(pb_ref.shape, BF16)
    ab_ref[...] = jnp.ones(ab_ref.shape, F32)

    def chunk_probs(c, lg_ref, p_ref, a_ref):
        sel, idx = chunk_mask(c)
        for h in range(H_A):
            cs = slice(h * Q_BLOCK, (h + 1) * Q_BLOCK)
            lg = jnp.where(sel, lg_ref[:, cs] + bias_ref[idx, h], -1e30)
            cm = _fold8(lg, jnp.maximum)
            for shift in (4, 2, 1):
                cm = jnp.maximum(cm, pltpu.roll(cm, shift, 0))
            m_old = m_ref[:, cs]
            m_new = jnp.maximum(m_old, cm[0:1])
            alpha = jnp.exp2(m_old - m_new)
            p = jnp.exp2(lg - m_new)
            ls_ref[:, cs] = alpha * ls_ref[:, cs] + _fold8(p, jnp.add)
            m_ref[:, cs] = m_new
            a_ref[:, cs] = alpha
            p_ref[:, cs] = p.astype(BF16)

    def values_mm(c, p_ref, a_ref):
        acc_ref[...] = acc_ref[...] * a_ref[...] + jnp.dot(
            ckvt_ref[0, :, pl.ds(chunk_start(c), KEY_CHUNK)], p_ref[...], preferred_element_type=F32)

    def attn_body(j, carry):
        c = 2 * j
        logits_mm(c + 1, lgb_ref)
        chunk_probs(c, lga_ref, pa_ref, aa_ref)
        values_mm(jnp.maximum(c - 1, 0), pb_ref, ab_ref)
        logits_mm(c + 2, lga_ref)
        chunk_probs(c + 1, lgb_ref, pb_ref, ab_ref)
        values_mm(c, pa_ref, aa_ref)
        return carry

    logits_mm(0, lga_ref)
    lax.fori_loop(0, npair, attn_body, 0)
    values_mm(2 * npair - 1, pb_ref, ab_ref)
    lsum = jnp.sum(ls_ref[...], axis=0, keepdims=True)

    for h in range(H_A):
        cs = slice(h * Q_BLOCK, (h + 1) * Q_BLOCK)
        o_lat = (acc_ref[:, cs] / lsum[:, cs]).astype(BF16)
        out = jnp.dot(wuvt_ref[h], o_lat, preferred_element_type=F32)
        o_ref[0, h * DH_A:(h + 1) * DH_A, :] = out.astype(o_ref.dtype)


def _dsa_attn(qlat, qidx, widx, kidx, ckv, wuv, bias):
    bsz, l, _ = ckv.shape
    nb = l // Q_BLOCK
    topk = min(TOPK_MAX, l // 4)
    nchunks = 2 * ((l + 2 * KEY_CHUNK - 1) // (2 * KEY_CHUNK))
    hq = H_A * Q_BLOCK
    widx_t = widx.reshape(bsz, nb, Q_BLOCK, H_I).transpose(0, 1, 3, 2)
    ckv_t = ckv.transpose(0, 2, 1)
    wuv_t = wuv.transpose(0, 2, 1)
    return pl.pallas_call(
        functools.partial(_dsa_attn_kernel, topk=topk),
        grid=(bsz, nb),
        in_specs=[pl.BlockSpec((1, 1, R_KV, hq), lambda b, i: (b, i, 0, 0)),
                  pl.BlockSpec((1, 1, D_I, H_I * Q_BLOCK), lambda b, i: (b, i, 0, 0)),
                  pl.BlockSpec((1, 1, H_I, Q_BLOCK), lambda b, i: (b, i, 0, 0)),
                  pl.BlockSpec((1, l, D_I), lambda b, i: (b, 0, 0)),
                  pl.BlockSpec((1, l, R_KV), lambda b, i: (b, 0, 0)),
                  pl.BlockSpec((1, R_KV, l), lambda b, i: (b, 0, 0)),
                  pl.BlockSpec((H_A, DH_A, R_KV), lambda b, i: (0, 0, 0)),
                  pl.BlockSpec((4, H_A, KEY_CHUNK, Q_BLOCK), lambda b, i: (0, 0, 0, 0))],
        out_specs=pl.BlockSpec((1, H_A * DH_A, Q_BLOCK), lambda b, i: (b, 0, i)),
        out_shape=jax.ShapeDtypeStruct((bsz, H_A * DH_A, l), BF16),
        scratch_shapes=[pltpu.VMEM((nchunks, KEY_CHUNK, Q_BLOCK), jnp.int32),
                        pltpu.VMEM((nchunks, KEY_CHUNK, Q_BLOCK), jnp.int16),
                        pltpu.VMEM((nchunks, KEY_CHUNK, Q_BLOCK), jnp.int16),
                        pltpu.VMEM((1, hq), F32), pltpu.VMEM((8, hq), F32), pltpu.VMEM((R_KV, hq), F32),
                        pltpu.VMEM((KEY_CHUNK, H_I * Q_BLOCK), F32), pltpu.VMEM((KEY_CHUNK, H_I * Q_BLOCK), F32),
                        pltpu.VMEM((KEY_CHUNK, hq), F32), pltpu.VMEM((KEY_CHUNK, hq), F32),
                        pltpu.VMEM((KEY_CHUNK, hq), BF16), pltpu.VMEM((KEY_CHUNK, hq), BF16),
                        pltpu.VMEM((1, hq), F32), pltpu.VMEM((1, hq), F32)],
        compiler_params=_cparams("parallel", "arbitrary"),
        name="dsa_attn",
    )(qlat, qidx, widx_t, kidx, ckv, ckv_t, wuv_t, bias)


def _seg_sum(x, bd):
    hi = x.astype(BF16)
    lo = (x - hi.astype(F32)).astype(BF16)
    outs = []
    for p in range(x.shape[1] // LANES):
        sl = slice(p * LANES, (p + 1) * LANES)
        outs.append(jnp.dot(hi[:, sl], bd, preferred_element_type=F32)
                    + jnp.dot(lo[:, sl], bd, preferred_element_type=F32))
    return outs[0] if len(outs) == 1 else jnp.concatenate(outs, axis=1)


def _rwkv_pre_kernel(pb_ref, mu_ref, w0_ref, a0_ref, kk_ref, ka_ref, rk_ref, wup_ref, aup_ref, gup_ref,
                     bd_ref, tril_ref, ab_ref, bb_ref, kb_ref, rb_ref, v_ref, p_ref, g_ref, bonus_ref,
                     carry_ref):
    @pl.when(pl.program_id(1) == 0)
    def _():
        carry_ref[...] = jnp.zeros_like(carry_ref)

    x = pb_ref[0]
    tm = x.shape[0]
    rows = lax.broadcasted_iota(jnp.int32, x.shape, 0)
    prev = jnp.where(rows == 0, carry_ref[...], pltpu.roll(x, 1, 0))
    carry_ref[...] = x[tm - 1:tm]
    ps = x + mu_ref[...] * (prev - x)
    r = ps[:, :W_B]
    k = ps[:, W_B:2 * W_B]
    v = ps[:, 2 * W_B:3 * W_B]
    o = 3 * W_B
    wd = ps[:, o:o + LORA_W]
    ad = ps[:, o + LORA_W:o + LORA_W + LORA_A]
    gd = ps[:, o + LORA_W + LORA_A:]
    hp = lax.Precision.HIGHEST
    z = w0_ref[...] + jnp.dot(jnp.tanh(wd), wup_ref[...], precision=hp, preferred_element_type=F32)
    u = -z
    w = -(jnp.maximum(u, 0.0) + jnp.log(1.0 + jnp.exp(-jnp.abs(u)))) - 0.5
    logd = -jnp.exp(w)
    logp = jnp.dot(tril_ref[...], logd, precision=hp, preferred_element_type=F32)
    pinv = jnp.exp(-logp)
    a = _sigmoid(a0_ref[...] + jnp.dot(ad, aup_ref[...], precision=hp, preferred_element_type=F32))
    g_ref[0] = jnp.dot(_sigmoid(gd).astype(BF16), gup_ref[...], preferred_element_type=F32)
    bd = bd_ref[...]
    kk = k * kk_ref[...]
    kk = kk * lax.rsqrt(jnp.maximum(_seg_sum(kk * kk, bd), 1e-24))
    k2 = k * (1.0 + (a - 1.0) * ka_ref[...])
    p = jnp.exp(logp)
    ab_ref[0] = -kk * jnp.exp(logp - logd)
    bb_ref[0] = kk * a * pinv
    kb_ref[0] = k2 * pinv
    rb_ref[0] = r * p
    v_ref[0] = v
    p_ref[0] = p
    bonus_ref[0] = _seg_sum(r * k2 * rk_ref[...], bd) * v


def _rwkv_pre(pb, mu, w0, a0, k_k, k_a, r_k, w_up, a_up, g_up, bd, tm=128):
    bsz, l, _ = pb.shape
    tm = min(tm, l)
    assert tm % RW_CHUNK == 0
    tril = jnp.kron(jnp.eye(tm // RW_CHUNK, dtype=F32), jnp.tril(jnp.ones((RW_CHUNK, RW_CHUNK), F32)))
    full = lambda shp: pl.BlockSpec(shp, lambda b, i: (0,) * len(shp))
    row = pl.BlockSpec((1, tm, W_B), lambda b, i: (b, i, 0))
    vec = full((1, W_B))
    return pl.pallas_call(
        _rwkv_pre_kernel,
        grid=(bsz, l // tm),
        in_specs=[pl.BlockSpec((1, tm, B_PAD), lambda b, i: (b, i, 0)), full((1, B_PAD)),
                  vec, vec, vec, vec, vec,
                  full((LORA_W, W_B)), full((LORA_A, W_B)), full((LORA_G_PAD, W_B)), full((LANES, LANES)),
                  full((tm, tm))],
        out_specs=[row] * 8,
        out_shape=[jax.ShapeDtypeStruct((bsz, l, W_B), F32)] * 8,
        scratch_shapes=[pltpu.VMEM((1, B_PAD), F32)],
        compiler_params=_cparams("parallel", "arbitrary"),
        name="rwkv_pre",
    )(pb, mu, w0.reshape(1, W_B), a0.reshape(1, W_B), k_k.reshape(1, W_B), k_a.reshape(1, W_B),
      r_k.reshape(1, W_B), w_up, a_up, g_up, bd, tril)


def _mm(a, b, dims, passes):
    dg = lambda x, y: lax.dot_general(x, y, dims, preferred_element_type=F32)
    ah, bh = a.astype(BF16), b.astype(BF16)
    if passes == 1:
        return dg(ah, bh)
    al = (a - ah.astype(F32)).astype(BF16)
    bl = (b - bh.astype(F32)).astype(BF16)
    return dg(ah, bh) + (dg(ah, bl) + dg(al, bh))


def _rwkv_chunk_kernel(ab_ref, bb_ref, kb_ref, rb_ref, v_ref, pe_ref, y_ref, s_ref, *, npair):
    @pl.when(pl.program_id(1) == 0)
    def _():
        s_ref[...] = jnp.zeros_like(s_ref)

    c = RW_CHUNK
    ri = lax.broadcasted_iota(jnp.int32, (c, c), 0)
    ci = lax.broadcasted_iota(jnp.int32, (c, c), 1)
    strict, incl = ci < ri, ci <= ri
    eye = jnp.where(ri == ci, 1.0, 0.0)
    head0 = lax.broadcasted_iota(jnp.int32, (1, LANES), 1) < N_B
    bi = lax.broadcasted_iota(jnp.int32, (LANES, LANES), 0) < N_B
    bj = lax.broadcasted_iota(jnp.int32, (LANES, LANES), 1) < N_B
    blockdiag = bi == bj
    nt = (((2,), (2,)), ((0,), (0,)))
    nn = (((2,), (1,)), ((0,), (0,)))
    tn = (((1,), (1,)), ((0,), (0,)))
    tiles = lambda ref: jnp.stack([ref[0, :, p * LANES:(p + 1) * LANES] for p in range(npair)])
    ab, bb, kb, rb, v = tiles(ab_ref), tiles(bb_ref), tiles(kb_ref), tiles(rb_ref), tiles(v_ref)
    pe = jnp.stack([pe_ref[0, 0, :, p * LANES:(p + 1) * LANES] for p in range(npair)])
    s0 = s_ref[...]
    both = lambda x: jnp.concatenate([x, x], axis=0)
    pick = lambda x: jnp.where(head0, x[:npair], x[npair:])
    split = lambda x: jnp.concatenate([jnp.where(head0, x, 0.0), jnp.where(head0, 0.0, x)], axis=0)
    ar = jnp.concatenate([ab, rb], axis=1)
    ars = split(ar)
    gb = _mm(ars, both(bb), nt, RW_PASSES)
    gk = _mm(ars, both(kb), nt, RW_PASSES)
    m_ab = jnp.where(strict, gb[:, :c], 0.0)
    n_rb = jnp.where(incl, gb[:, c:], 0.0)
    m_ak = jnp.where(strict, gk[:, :c], 0.0)
    n_rk = jnp.where(incl, gk[:, c:], 0.0)
    tinv = eye + m_ab
    pw = m_ab
    for _ in range(5):
        pw = _mm(pw, pw, nn, RW_PASSES)
        tinv = tinv + _mm(tinv, pw, nn, RW_PASSES)
    x0 = _mm(ar, s0, nt, RW_PASSES)
    v2 = both(v)
    x = x0[:, :c] + pick(_mm(m_ak, v2, nn, RW_PASSES))
    u = pick(_mm(tinv, both(x), nn, RW_PASSES))
    y = x0[:, c:] + pick(_mm(n_rb, both(u), nn, RW_PASSES)) + pick(_mm(n_rk, v2, nn, RW_PASSES))
    for p in range(npair):
        y_ref[0, :, p * LANES:(p + 1) * LANES] = y[p]
    upd = _mm(jnp.concatenate([u, v], axis=1), jnp.concatenate([bb, kb], axis=1), tn, RW_PASSES)
    s_ref[...] = (s0 + jnp.where(blockdiag, upd, 0.0)) * pe


def _rwkv_chunks(ab, bb, kb, rb, v, pe):
    bsz, l, wb = ab.shape
    nt = l // RW_CHUNK
    npair = wb // LANES
    row = pl.BlockSpec((1, RW_CHUNK, wb), lambda b, i: (b, i, 0))
    return pl.pallas_call(
        functools.partial(_rwkv_chunk_kernel, npair=npair),
        grid=(bsz, nt),
        in_specs=[row] * 5 + [pl.BlockSpec((1, 1, 1, wb), lambda b, i: (b, i, 0, 0))],
        out_specs=row,
        out_shape=jax.ShapeDtypeStruct((bsz, l, wb), F32),
        scratch_shapes=[pltpu.VMEM((npair, LANES, LANES), F32)],
        compiler_params=_cparams("parallel", "arbitrary"),
        name="rwkv_chunks",
    )(ab, bb, kb, rb, v, pe)


def _rwkv_post_kernel(y_ref, bonus_ref, g_ref, lw_ref, lb_ref, bd_ref, o_ref):
    bd = bd_ref[...]
    y = y_ref[0]
    mean = _seg_sum(y, bd) * (1.0 / N_B)
    c = y - mean
    var = _seg_sum(c * c, bd) * (1.0 / N_B)
    yn = c * lax.rsqrt(var + GN_EPS) * lw_ref[...] + lb_ref[...]
    o_ref[0] = ((yn + bonus_ref[0]) * g_ref[0]).astype(o_ref.dtype)


def _rwkv_post(y, bonus, g, lnx_w, lnx_b, bd, tm=256):
    bsz, l, _ = y.shape
    tm = min(tm, l)
    row = pl.BlockSpec((1, tm, W_B), lambda b, i: (b, i, 0))
    vec = pl.BlockSpec((1, W_B), lambda b, i: (0, 0))
    return pl.pallas_call(
        _rwkv_post_kernel,
        grid=(bsz, l // tm),
        in_specs=[row, row, row, vec, vec, pl.BlockSpec((LANES, LANES), lambda b, i: (0, 0))],
        out_specs=row,
        out_shape=jax.ShapeDtypeStruct((bsz, l, W_B), BF16),
        compiler_params=_cparams("parallel", "parallel"),
        name="rwkv_post",
    )(y, bonus, g, lnx_w.reshape(1, W_B), lnx_b.reshape(1, W_B), bd)


def _rwkv_mixer(pb, mu, w0, w_up, a0, a_up, g_up, k_k, k_a, r_k, lnx_w, lnx_b):
    bsz, l, _ = pb.shape
    nt = l // RW_CHUNK
    eye2 = jnp.kron(jnp.eye(2, dtype=F32), jnp.ones((N_B, N_B), F32)).astype(BF16)
    ab, bb, kb, rb, v, p, g, bonus = _rwkv_pre(pb, mu, w0, a0, k_k, k_a, r_k, w_up, a_up, g_up, eye2)
    pe = p[:, RW_CHUNK - 1::RW_CHUNK].reshape(bsz, nt, 1, W_B)
    y = _rwkv_chunks(ab, bb, kb, rb, v, pe)
    return _rwkv_post(y, bonus, g, lnx_w, lnx_b, eye2)


def _hgrn_kernel(q_ref, f_ref, i_ref, g_ref, lb_ref, on_ref, tril_ref, o_ref, st_ref, *, layer, hb, tt):
    @pl.when(pl.program_id(2) == 0)
    def _():
        st_ref[...] = jnp.zeros_like(st_ref)

    x = lb_ref[...]
    e = jnp.exp(x - jnp.max(x, axis=0, keepdims=True))
    sm = e / jnp.sum(e, axis=0, keepdims=True)
    cs = sm[0:1]
    for i in range(1, layer + 1):
        cs = cs + sm[i:i + 1]
    lb_all = cs - sm[0:1]
    rowi = lax.broadcasted_iota(jnp.int32, (SUB, LANES), 0)
    lanei = lax.broadcasted_iota(jnp.int32, (SUB, LANES), 1)
    prep = []
    for hh in range(hb):
        sl = slice(hh * LANES, (hh + 1) * LANES)
        lb = lb_all[:, sl]
        qraw = q_ref[0][:, sl]
        ff = f_ref[0][:, sl]
        q = qraw * _sigmoid(qraw)
        logsig = jnp.minimum(ff, 0.0) - jnp.log(1.0 + jnp.exp(-jnp.abs(ff)))
        la = jnp.log(lb)
        lbv = jnp.log(1.0 - lb) + logsig
        mx = jnp.maximum(la, lbv)
        logf = mx + jnp.log(jnp.exp(la - mx) + jnp.exp(lbv - mx))
        k = (1.0 - lb) * _sigmoid(-ff)
        bcum = jnp.dot(tril_ref[...], logf * LOG2E, precision=lax.Precision.HIGHEST, preferred_element_type=F32)
        prep.append((q, k, i_ref[0][:, sl], bcum))
    sts = [st_ref[hh] for hh in range(hb)]
    outs = [[] for _ in range(hb)]
    for n in range(tt // SUB):
        rs = slice(n * SUB, (n + 1) * SUB)
        for hh in range(hb):
            q, k, v, bcum = prep[hh]
            st = sts[hh]
            bq, qn, kn, vn = bcum[rs], q[rs], k[rs], v[rs]
            bend = bq[SUB - 1:SUB]
            qd = (qn * jnp.exp2(bq)).astype(BF16)
            o = lax.dot_general(qd, st.astype(BF16), NT_DIMS, preferred_element_type=F32)
            a = jnp.zeros((SUB, LANES), F32)
            for s in range(SUB):
                wgt = kn[s:s + 1] * jnp.exp2(jnp.minimum(bq - bq[s:s + 1], 0.0))
                a = jnp.where(lanei == s, jnp.sum(qn * wgt, axis=1, keepdims=True), a)
            a = jnp.where(lanei <= rowi, a, 0.0)[:, :SUB]
            o = o + jnp.dot(a.astype(BF16), vn.astype(BF16), preferred_element_type=F32)
            kd = (kn * jnp.exp2(bend - bq)).astype(BF16)
            sts[hh] = st * jnp.exp2(bend) + lax.dot_general(vn.astype(BF16), kd, TN_DIMS,
                                                           preferred_element_type=F32)
            outs[hh].append(o)
    for hh in range(hb):
        sl = slice(hh * LANES, (hh + 1) * LANES)
        st_ref[hh] = sts[hh]
        gg = g_ref[0][:, sl]
        o = jnp.concatenate(outs[hh], axis=0)
        o = _rms(o) * on_ref[:, sl] * (gg * _sigmoid(gg))
        o_ref[0, :, sl] = o.astype(o_ref.dtype)


def _hgrn_mixer(pc, hgrn_lb, onorm, layer, hb=12, tt=128):
    bsz, l, _ = pc.shape
    depth = hgrn_lb.shape[0]
    tt = min(tt, l)
    ng = H_C // hb
    wblk = hb * LANES
    tril = jnp.kron(jnp.eye(tt // SUB, dtype=F32), jnp.tril(jnp.ones((SUB, SUB), F32)))
    col = lambda j: pl.BlockSpec((1, tt, wblk), lambda b, h, i, j=j: (b, i, j * ng + h))
    return pl.pallas_call(
        functools.partial(_hgrn_kernel, layer=layer, hb=hb, tt=tt),
        grid=(bsz, ng, l // tt),
        in_specs=[col(0), col(1), col(2), col(3),
                  pl.BlockSpec((depth, wblk), lambda b, h, i: (0, h)),
                  pl.BlockSpec((1, wblk), lambda b, h, i: (0, h)),
                  pl.BlockSpec((tt, tt), lambda b, h, i: (0, 0))],
        out_specs=pl.BlockSpec((1, tt, wblk), lambda b, h, i: (b, i, h)),
        out_shape=jax.ShapeDtypeStruct((bsz, l, W_C), BF16),
        scratch_shapes=[pltpu.VMEM((hb, DV_C, DK_C), F32)],
        compiler_params=_cparams("parallel", "parallel", "arbitrary"),
        name="hgrn2",
    )(pc, pc, pc, pc, hgrn_lb, onorm.reshape(1, W_C), tril)


def _pack_w_in(w):
    d = w.shape[0]
    a_cols = R_Q + R_KV + D_I + H_I
    b_cols = 3 * W_B + LORA_W + LORA_A + LORA_G
    wa, wb, wc = w[:, :a_cols], w[:, a_cols:a_cols + b_cols], w[:, a_cols + b_cols:]
    z = lambda n: jnp.zeros((d, n), w.dtype)
    o = R_Q + R_KV
    wa = jnp.concatenate([wa[:, :o + D_I], z(LANES - D_I), wa[:, o + D_I:], z(LANES - H_I)], axis=1)
    wb = jnp.concatenate([wb, z(LORA_G_PAD - LORA_G)], axis=1)
    return wa.astype(BF16), wb.astype(BF16), wc.astype(BF16)


def kernel(x, c, rel_bias, hgrn_lb, ada_w, ada_b, norm_g, w_in, w_out, mla_q_norm, mla_kv_norm, w_uq, w_uk, w_uv, w_qidx, rwkv_mu, rwkv_w0, rwkv_w_up, rwkv_a0, rwkv_a_up, rwkv_g_up, rwkv_k_k, rwkv_k_a, rwkv_r_k, rwkv_lnx_w, rwkv_lnx_b, hgrn_onorm, w_ff1, w_ff2):
    bsz, l, d = x.shape
    depth = ada_w.shape[0]
    m = bsz * l
    mod = _ada_mod(c, ada_w, ada_b)
    mods = [[mod[i, :, None, j * d:(j + 1) * d] for j in range(6)] for i in range(depth)]
    bias = _bias_tiles(rel_bias)
    h = _norm_mod(x, norm_g[0, 0], mods[0][1], mods[0][0])
    for i in range(depth):
        sh_m, sc_m, g_m, sh_f, sc_f, g_f = mods[i]
        wa, wb, wc = _pack_w_in(w_in[i])
        h2d = h.reshape(m, d)
        pa = _matmul(h2d, wa, F32, 512, A_PAD, 4096, name="in_proj_a").reshape(bsz, l, -1)
        pb = _matmul(h2d, wb, F32, 1024, 768, 4096, name="in_proj_b").reshape(bsz, l, -1)
        pc = _matmul(h2d, wc, F32, 1024, 1024, 4096, name="in_proj_c").reshape(bsz, l, -1)
        wuq = w_uq[i].reshape(R_Q, H_A * DH_A).astype(BF16)
        wqi = jnp.transpose(w_qidx[i], (1, 2, 0)).astype(BF16)
        qlat, qidx, widx, kidx, ckv = _dsa_prep(pa, mla_q_norm[i], mla_kv_norm[i], wuq,
                                                w_uk[i].astype(BF16), wqi)
        y_a = _dsa_attn(qlat, qidx, widx, kidx, ckv, w_uv[i].astype(BF16), bias).transpose(0, 2, 1)
        mu = jnp.concatenate([rwkv_mu[i], jnp.zeros((LORA_G_PAD - LORA_G,), F32)]).reshape(1, B_PAD)
        gup = jnp.concatenate([rwkv_g_up[i], jnp.zeros((LORA_G_PAD - LORA_G, W_B), F32)]).astype(BF16)
        y_b = _rwkv_mixer(pb, mu, rwkv_w0[i], rwkv_w_up[i], rwkv_a0[i], rwkv_a_up[i], gup,
                          rwkv_k_k[i], rwkv_k_a[i], rwkv_r_k[i], rwkv_lnx_w[i], rwkv_lnx_b[i])
        y_c = _hgrn_mixer(pc, hgrn_lb, hgrn_onorm[i], i)
        ycat = jnp.concatenate([y_a, y_b, y_c], axis=-1).reshape(m, -1)
        y = _matmul(ycat, w_out[i].astype(BF16), F32, 1024, 1024, 4096, name="out_proj").reshape(bsz, l, d)
        x, h = _resid_norm(x, y, g_m, norm_g[i, 1], (norm_g[i, 2], sc_f, sh_f))
        u = _matmul(h.reshape(m, d), w_ff1[i].astype(BF16), BF16, 1024, 1024, 4096, relu2=True, name="ffn_up")
        y = _matmul(u, w_ff2[i].astype(BF16), F32, 1024, 1024, 4096, name="ffn_down").reshape(bsz, l, d)
        nxt = None if i == depth - 1 else (norm_g[i + 1, 0], mods[i + 1][1], mods[i + 1][0])
        x, h = _resid_norm(x, y, g_f, norm_g[i, 3], nxt)
    return x
```

```python
import functools
import math

import jax
import jax.numpy as jnp
from jax import lax
from jax.experimental import pallas as pl
from jax.experimental.pallas import tpu as pltpu

H_A, DH_A, R_Q, R_KV, H_I, D_I = 8, 128, 768, 256, 16, 64
TOPK_MAX, Q_BLOCK, N_BUCKETS, MAX_DIST = 256, 128, 32, 128
H_B, N_B, LORA_W, LORA_A, LORA_G = 24, 64, 128, 128, 480
W_B = H_B * N_B
H_C, DK_C, DV_C = 12, 128, 128
W_C = H_C * DK_C
EPS = 1e-6
GN_EPS = 64e-5

LANES = 128
KEY_CHUNK = 256
SUB = 8
RW_CHUNK = 64
RW_PASSES = 1
LORA_G_PAD = 512
A_PAD = 1280
B_PAD = 3 * W_B + LORA_W + LORA_A + LORA_G_PAD
VMEM_LIMIT = 56 * 1024 * 1024

F32 = jnp.float32
BF16 = jnp.bfloat16
NT_DIMS = (((1,), (1,)), ((), ()))
TN_DIMS = (((0,), (0,)), ((), ()))
INT_MIN = -2 ** 31
LOG2E = math.log2(math.e)


def _cparams(*sem):
    return pltpu.CompilerParams(dimension_semantics=sem, vmem_limit_bytes=VMEM_LIMIT)


def _sigmoid(x):
    return 1.0 / (1.0 + jnp.exp(-x))


def _rms(x, eps=EPS):
    return x * lax.rsqrt(jnp.mean(x * x, axis=-1, keepdims=True) + eps)


def _mm_kernel(a_ref, b_ref, o_ref, acc_ref, *, nk, relu2):
    k = pl.program_id(2)

    @pl.when(k == 0)
    def _():
        acc_ref[...] = jnp.zeros_like(acc_ref)

    acc_ref[...] += jnp.dot(a_ref[...], b_ref[...], preferred_element_type=F32)

    @pl.when(k == nk - 1)
    def _():
        r = acc_ref[...]
        if relu2:
            r = jnp.square(jnp.maximum(r, 0.0))
        o_ref[...] = r.astype(o_ref.dtype)


def _mm_full_k_kernel(a_ref, b_ref, o_ref, *, relu2):
    r = jnp.dot(a_ref[...], b_ref[...], preferred_element_type=F32)
    if relu2:
        r = jnp.square(jnp.maximum(r, 0.0))
    o_ref[...] = r.astype(o_ref.dtype)


def _matmul(a, b, out_dtype, tm, tn, tk, relu2=False, name="matmul"):
    m, kdim = a.shape
    _, n = b.shape
    tm, tn, tk = min(tm, m), min(tn, n), min(tk, kdim)
    assert m % tm == 0 and n % tn == 0 and kdim % tk == 0, (a.shape, b.shape, tm, tn, tk)
    nk = kdim // tk
    if nk == 1:
        return pl.pallas_call(
            functools.partial(_mm_full_k_kernel, relu2=relu2),
            grid=(m // tm, n // tn),
            in_specs=[pl.BlockSpec((tm, kdim), lambda i, j: (i, 0)),
                      pl.BlockSpec((kdim, tn), lambda i, j: (0, j))],
            out_specs=pl.BlockSpec((tm, tn), lambda i, j: (i, j)),
            out_shape=jax.ShapeDtypeStruct((m, n), out_dtype),
            compiler_params=_cparams("parallel", "parallel"),
            name=name,
        )(a, b)
    return pl.pallas_call(
        functools.partial(_mm_kernel, nk=nk, relu2=relu2),
        grid=(m // tm, n // tn, nk),
        in_specs=[pl.BlockSpec((tm, tk), lambda i, j, k: (i, k)),
                  pl.BlockSpec((tk, tn), lambda i, j, k: (k, j))],
        out_specs=pl.BlockSpec((tm, tn), lambda i, j, k: (i, j)),
        out_shape=jax.ShapeDtypeStruct((m, n), out_dtype),
        scratch_shapes=[pltpu.VMEM((tm, tn), F32)],
        compiler_params=_cparams("parallel", "parallel", "arbitrary"),
        name=name,
    )(a, b)


def _ada_kernel(c_ref, w_ref, b_ref, o_ref):
    k = pl.program_id(2)

    @pl.when(k == 0)
    def _():
        o_ref[0] = jnp.broadcast_to(b_ref[0], o_ref.shape[1:])

    c = c_ref[...]
    ca = (c * _sigmoid(c)).astype(BF16)
    o_ref[0] += jnp.dot(ca, w_ref[0].astype(BF16), preferred_element_type=F32)


def _ada_mod(c, ada_w, ada_b):
    nl, d, n = ada_w.shape
    bsz = c.shape[0]
    rows = 8
    cp = jnp.zeros((rows, d), F32).at[:bsz].set(c)
    tn, tk = min(2048, n), min(1024, d)
    out = pl.pallas_call(
        _ada_kernel,
        grid=(nl, n // tn, d // tk),
        in_specs=[pl.BlockSpec((rows, tk), lambda l, j, k: (0, k)),
                  pl.BlockSpec((1, tk, tn), lambda l, j, k: (l, k, j)),
                  pl.BlockSpec((1, 1, tn), lambda l, j, k: (l, 0, j))],
        out_specs=pl.BlockSpec((1, rows, tn), lambda l, j, k: (l, 0, j)),
        out_shape=jax.ShapeDtypeStruct((nl, rows, n), F32),
        compiler_params=_cparams("parallel", "parallel", "arbitrary"),
        name="ada_mod",
    )(cp, ada_w, ada_b.reshape(nl, 1, n))
    return out[:, :bsz]


def _norm_mod_kernel(x_ref, g_ref, sc_ref, sh_ref, h_ref):
    y = _rms(x_ref[0]) * g_ref[...]
    h_ref[0] = (y * (1.0 + sc_ref[0]) + sh_ref[0]).astype(h_ref.dtype)


def _norm_mod(x, g, sc, sh, tm=256):
    bsz, l, d = x.shape
    tm = min(tm, l)
    row = pl.BlockSpec((1, tm, d), lambda b, i: (b, i, 0))
    vec = pl.BlockSpec((1, d), lambda b, i: (0, 0))
    bvec = pl.BlockSpec((1, 1, d), lambda b, i: (b, 0, 0))
    return pl.pallas_call(
        _norm_mod_kernel,
        grid=(bsz, l // tm),
        in_specs=[row, vec, bvec, bvec],
        out_specs=row,
        out_shape=jax.ShapeDtypeStruct((bsz, l, d), BF16),
        compiler_params=_cparams("parallel", "parallel"),
        name="norm_mod",
    )(x, g.reshape(1, d), sc, sh)


def _resid_kernel(x_ref, y_ref, gate_ref, g1_ref, g2_ref, sc_ref, sh_ref, xo_ref, h_ref):
    xn = x_ref[0] + gate_ref[0] * (_rms(y_ref[0]) * g1_ref[...])
    xo_ref[0] = xn
    hn = _rms(xn) * g2_ref[...]
    h_ref[0] = (hn * (1.0 + sc_ref[0]) + sh_ref[0]).astype(h_ref.dtype)


def _resid_last_kernel(x_ref, y_ref, gate_ref, g1_ref, xo_ref):
    xo_ref[0] = x_ref[0] + gate_ref[0] * (_rms(y_ref[0]) * g1_ref[...])


def _resid_norm(x, y, gate, g1, nxt=None, tm=128):
    bsz, l, d = x.shape
    tm = min(tm, l)
    row = pl.BlockSpec((1, tm, d), lambda b, i: (b, i, 0))
    vec = pl.BlockSpec((1, d), lambda b, i: (0, 0))
    bvec = pl.BlockSpec((1, 1, d), lambda b, i: (b, 0, 0))
    if nxt is None:
        return pl.pallas_call(
            _resid_last_kernel,
            grid=(bsz, l // tm),
            in_specs=[row, row, bvec, vec],
            out_specs=row,
            out_shape=jax.ShapeDtypeStruct((bsz, l, d), F32),
            compiler_params=_cparams("parallel", "parallel"),
            name="resid_last",
        )(x, y, gate, g1.reshape(1, d)), None
    g2, sc, sh = nxt
    return pl.pallas_call(
        _resid_kernel,
        grid=(bsz, l // tm),
        in_specs=[row, row, bvec, vec, vec, bvec, bvec],
        out_specs=[row, row],
        out_shape=[jax.ShapeDtypeStruct((bsz, l, d), F32), jax.ShapeDtypeStruct((bsz, l, d), BF16)],
        compiler_params=_cparams("parallel", "parallel"),
        name="resid_norm",
    )(x, y, gate, g1.reshape(1, d), g2.reshape(1, d), sc, sh)


def _bias_kernel(rb_ref, o_ref):
    j = lax.broadcasted_iota(jnp.int32, (KEY_CHUNK, Q_BLOCK), 0)
    i = lax.broadcasted_iota(jnp.int32, (KEY_CHUNK, Q_BLOCK), 1)
    max_exact = N_BUCKETS // 2
    for v, off in enumerate((Q_BLOCK, 0, 2 * Q_BLOCK, None)):
        if off is None:
            dist = jnp.full((KEY_CHUNK, Q_BLOCK), 2 * MAX_DIST, jnp.int32)
        else:
            dist = jnp.maximum(i + off - j, 0)
        nf = jnp.maximum(dist, 1).astype(F32)
        large = max_exact + (jnp.log(nf / max_exact) / math.log(MAX_DIST / max_exact)
                             * (N_BUCKETS - max_exact)).astype(jnp.int32)
        large = jnp.minimum(large, N_BUCKETS - 1)
        bucket = jnp.where(dist < max_exact, dist, large)
        for h in range(H_A):
            def body(b, acc, h=h, bucket=bucket):
                return jnp.where(bucket == b, rb_ref[b * H_A + h] * LOG2E, acc)
            o_ref[v, h] = lax.fori_loop(0, N_BUCKETS, body, jnp.zeros((KEY_CHUNK, Q_BLOCK), F32))


def _bias_tiles(rel_bias):
    return pl.pallas_call(
        _bias_kernel,
        in_specs=[pl.BlockSpec(memory_space=pltpu.SMEM)],
        out_specs=pl.BlockSpec(memory_space=pltpu.VMEM),
        out_shape=jax.ShapeDtypeStruct((4, H_A, KEY_CHUNK, Q_BLOCK), F32),
        name="dsa_bias_tiles",
    )(rel_bias.reshape(-1))


def _dsa_prep_kernel(pa_ref, gq_ref, gkv_ref, wuq_ref, wuk_ref, wqi_ref,
                     qlat_ref, qidx_ref, widx_ref, kidx_ref, ckv_ref):
    pa = pa_ref[0]
    cq = pa[:, :R_Q]
    ckv = pa[:, R_Q:R_Q + R_KV]
    kid = pa[:, R_Q + R_KV:R_Q + R_KV + D_I]
    wid = pa[:, R_Q + R_KV + LANES:R_Q + R_KV + LANES + H_I]
    cqn = (_rms(cq) * gq_ref[...]).astype(BF16)
    ckv_ref[0] = (_rms(ckv) * gkv_ref[...]).astype(BF16)
    kidx_ref[0] = kid.astype(BF16)
    widx_ref[0] = wid * (H_I ** -0.5 * D_I ** -0.5)
    q = jnp.dot(cqn, wuq_ref[...], preferred_element_type=F32)
    for h in range(H_A):
        qh = q[:, h * DH_A:(h + 1) * DH_A].astype(BF16)
        ql = lax.dot_general(wuk_ref[h], qh, NT_DIMS, preferred_element_type=F32)
        qlat_ref[0, 0, :, h * Q_BLOCK:(h + 1) * Q_BLOCK] = (ql * (DH_A ** -0.5 * LOG2E)).astype(BF16)
    for h in range(H_I):
        qi = lax.dot_general(wqi_ref[h], cqn, NT_DIMS, preferred_element_type=F32)
        qidx_ref[0, 0, :, h * Q_BLOCK:(h + 1) * Q_BLOCK] = qi.astype(BF16)


def _dsa_prep(pa, gq, gkv, wuq, wuk, wqi):
    bsz, l, _ = pa.shape
    nb = l // Q_BLOCK
    full = lambda shp: pl.BlockSpec(shp, lambda b, i: (0,) * len(shp))
    return pl.pallas_call(
        _dsa_prep_kernel,
        grid=(bsz, nb),
        in_specs=[pl.BlockSpec((1, Q_BLOCK, A_PAD), lambda b, i: (b, i, 0)),
                  full((1, R_Q)), full((1, R_KV)), full((R_Q, H_A * DH_A)),
                  full((H_A, R_KV, DH_A)), full((H_I, D_I, R_Q))],
        out_specs=[pl.BlockSpec((1, 1, R_KV, H_A * Q_BLOCK), lambda b, i: (b, i, 0, 0)),
                   pl.BlockSpec((1, 1, D_I, H_I * Q_BLOCK), lambda b, i: (b, i, 0, 0)),
                   pl.BlockSpec((1, Q_BLOCK, H_I), lambda b, i: (b, i, 0)),
                   pl.BlockSpec((1, Q_BLOCK, D_I), lambda b, i: (b, i, 0)),
                   pl.BlockSpec((1, Q_BLOCK, R_KV), lambda b, i: (b, i, 0))],
        out_shape=[jax.ShapeDtypeStruct((bsz, nb, R_KV, H_A * Q_BLOCK), BF16),
                   jax.ShapeDtypeStruct((bsz, nb, D_I, H_I * Q_BLOCK), BF16),
                   jax.ShapeDtypeStruct((bsz, l, H_I), F32),
                   jax.ShapeDtypeStruct((bsz, l, D_I), BF16),
                   jax.ShapeDtypeStruct((bsz, l, R_KV), BF16)],
        compiler_params=_cparams("parallel", "parallel"),
        name="dsa_prep",
    )(pa, gq.reshape(1, R_Q), gkv.reshape(1, R_KV), wuq, wuk, wqi)


def _fold8(x, op, ways=1):
    n = x.shape[0] // 8
    ways = min(ways, n)
    acc = [x[g * 8:(g + 1) * 8] for g in range(ways)]
    for g in range(ways, n):
        acc[g % ways] = op(acc[g % ways], x[g * 8:(g + 1) * 8])
    while len(acc) > 1:
        acc = [op(acc[i], acc[i + 1]) for i in range(0, len(acc), 2)]
    return acc[0]


def _dsa_attn_kernel(qlat_ref, qidx_ref, widx_ref, kidx_ref, ckv_ref, ckvt_ref, wuvt_ref, bias_ref, o_ref,
                     key_ref, hi_ref, lo_ref, m_ref, ls_ref, acc_ref, sa_ref, sb_ref, lga_ref, lgb_ref,
                     pa_ref, pb_ref, aa_ref, ab_ref, *, topk):
    qi = pl.program_id(1)
    nch = (qi * Q_BLOCK) // KEY_CHUNK + 1
    krow = lax.broadcasted_iota(jnp.int32, (KEY_CHUNK, Q_BLOCK), 0)
    qlane = lax.broadcasted_iota(jnp.int32, (KEY_CHUNK, Q_BLOCK), 1)
    t_abs = qi * Q_BLOCK + qlane
    w = widx_ref[0, 0]

    npair = (nch + 1) // 2
    last_chunk = kidx_ref.shape[1] // KEY_CHUNK - 1

    def chunk_start(c):
        return pl.multiple_of(jnp.minimum(c, last_chunk) * KEY_CHUNK, KEY_CHUNK)

    def score_mm(c, s_ref):
        kc = kidx_ref[0, pl.ds(chunk_start(c), KEY_CHUNK), :]
        s_ref[...] = jnp.dot(kc, qidx_ref[0, 0], preferred_element_type=F32)

    def score_keys(c, s_ref):
        acc = jnp.zeros((KEY_CHUNK, Q_BLOCK), F32)
        for h in range(H_I):
            acc = acc + w[h:h + 1] * jnp.maximum(s_ref[:, h * Q_BLOCK:(h + 1) * Q_BLOCK], 0.0)
        acc = jnp.where(c * KEY_CHUNK + krow <= t_abs, acc, -jnp.inf)
        bits = lax.bitcast_convert_type(acc, jnp.int32)
        key = bits ^ ((bits >> 31) & 0x7FFFFFFF)
        key_ref[c] = key
        hi_ref[c] = (key >> 16).astype(jnp.int16)
        lo_ref[c] = ((key & 0xFFFF) - 0x8000).astype(jnp.int16)

    def score_body(j, carry):
        c = 2 * j
        score_mm(c + 1, sb_ref)
        score_keys(c, sa_ref)
        score_mm(c + 2, sa_ref)
        score_keys(c + 1, sb_ref)
        return carry

    score_mm(0, sa_ref)
    lax.fori_loop(0, npair, score_body, 0)

    def count_ge16(ref, cand):
        c16 = cand.astype(jnp.int16)
        rows = 16

        def fold(x):
            r = x[0:rows]
            for g in range(1, x.shape[0] // rows):
                r = r + x[g * rows:(g + 1) * rows]
            return r

        def hits(c):
            return fold(jnp.where(ref[c] >= c16, jnp.int16(1), jnp.int16(0)))

        def quad(c, cnt):
            return cnt + ((hits(4 * c) + hits(4 * c + 1)) + (hits(4 * c + 2) + hits(4 * c + 3)))

        def tail(_, cnt):
            return cnt + (hits(2 * npair - 2) + hits(2 * npair - 1))

        cnt = lax.fori_loop(0, npair // 2, quad, jnp.zeros((rows, Q_BLOCK), jnp.int16))
        cnt = lax.fori_loop(0, npair % 2, tail, cnt)
        return jnp.sum(cnt.astype(F32), axis=0, keepdims=True)

    kf = float(topk)
    total = (npair * (2 * KEY_CHUNK)).astype(F32)
    few = (qi * Q_BLOCK + lax.broadcasted_iota(jnp.int32, (1, Q_BLOCK), 1)) < topk

    def unsettled(cnt_thr):
        return jnp.max(jnp.where(few | (cnt_thr == kf), 0.0, 1.0)) > 0.5

    def search16(ref, early_exit, cnt_min):
        cnt0 = count_ge16(ref, jnp.zeros((1, Q_BLOCK), jnp.int32))
        thr = jnp.where(cnt0 >= kf, 0, -0x8000).astype(jnp.int32)
        cnt_thr = jnp.where(cnt0 >= kf, cnt0, cnt_min)

        def group_cond(st):
            return (st[0] < 4) & st[3]

        def group_body(st):
            g, thr, cnt_thr, _ = st
            for k in range(4):
                i = 4 * g + k
                bit = jnp.where(i < 15, jnp.left_shift(jnp.int32(1), jnp.maximum(14 - i, 0)), 0)
                cand = thr + bit
                cnt = count_ge16(ref, cand)
                take = cnt >= kf
                thr = jnp.where(take, cand, thr)
                cnt_thr = jnp.where(take, cnt, cnt_thr)
            return g + 1, thr, cnt_thr, (unsettled(cnt_thr) if early_exit else jnp.bool_(True))

        go = unsettled(cnt_thr) if early_exit else jnp.bool_(True)
        _, thr, cnt_thr, _ = lax.while_loop(group_cond, group_body, (jnp.int32(0), thr, cnt_thr, go))
        return thr, cnt_thr

    thr_hi, cnt_hi = search16(hi_ref, False, total)
    thi16 = thr_hi.astype(jnp.int16)

    def relabel(c, carry):
        for cc in (2 * c, 2 * c + 1):
            hi = hi_ref[cc]
            lo_ref[cc] = jnp.where(hi > thi16, jnp.int16(0x7FFF),
                                   jnp.where(hi == thi16, lo_ref[cc], jnp.int16(-0x8000)))
        return carry

    lax.fori_loop(0, npair, relabel, 0)
    thr_lo, _ = search16(lo_ref, True, cnt_hi)
    thr = thr_hi * 65536 + (thr_lo + 0x8000)

    odd = (qi % 2) == 1

    def logits_mm(c, lg_ref):
        kv = ckv_ref[0, pl.ds(chunk_start(c), KEY_CHUNK), :]
        lg_ref[...] = jnp.dot(kv, qlat_ref[0, 0], preferred_element_type=F32)

    def chunk_mask(c):
        sel = (key_ref[c] >= thr) & (c * KEY_CHUNK + krow <= t_abs)
        idx = jnp.where(c == nch - 1, jnp.where(odd, 0, 1),
                        jnp.where((c == nch - 2) & jnp.logical_not(odd), 2, 3))
        return sel, idx

    m_ref[...] = jnp.full(m_ref.shape, -1e29, F32)
    ls_ref[...] = jnp.zeros(ls_ref.shape, F32)
    acc_ref[...] = jnp.zeros(acc_ref.shape, F32)
    pb_ref[...] = jnp.zeros(pb_ref.shape, BF16)
    ab_ref[...] = jnp.ones(ab_ref.shape, F32)

    def chunk_probs(c, lg_ref, p_ref, a_ref):
        sel, idx = chunk_mask(c)
        for h in range(H_A):
            cs = slice(h * Q_BLOCK, (h + 1) * Q_BLOCK)
            lg = jnp.where(sel, lg_ref[:, cs] + bias_ref[idx, h], -1e30)
            cm = _fold8(lg, jnp.maximum)
            for shift in (4, 2, 1):
                cm = jnp.maximum(cm, pltpu.roll(cm, shift, 0))
            m_old = m_ref[:, cs]
            m_new = jnp.maximum(m_old, cm[0:1])
            alpha = jnp.exp2(m_old - m_new)
            p = jnp.exp2(lg - m_new)
            ls_ref[:, cs] = alpha * ls_ref[:, cs] + _fold8(p, jnp.add)
            m_ref[:, cs] = m_new
            a_ref[:, cs] = alpha
            p_ref[:, cs] = p.astype(BF16)

    def values_mm(c, p_ref, a_ref):
        acc_ref[...] = acc_ref[...] * a_ref[...] + jnp.dot(
            ckvt_ref[0, :, pl.ds(chunk_start(c), KEY_CHUNK)], p_ref[...], preferred_element_type=F32)

    def attn_body(j, carry):
        c = 2 * j
        logits_mm(c + 1, lgb_ref)
        chunk_probs(c, lga_ref, pa_ref, aa_ref)
        values_mm(jnp.maximum(c - 1, 0), pb_ref, ab_ref)
        logits_mm(c + 2, lga_ref)
        chunk_probs(c + 1, lgb_ref, pb_ref, ab_ref)
        values_mm(c, pa_ref, aa_ref)
        return carry

    logits_mm(0, lga_ref)
    lax.fori_loop(0, npair, attn_body, 0)
    values_mm(2 * npair - 1, pb_ref, ab_ref)
    lsum = jnp.sum(ls_ref[...], axis=0, keepdims=True)

    for h in range(H_A):
        cs = slice(h * Q_BLOCK, (h + 1) * Q_BLOCK)
        o_lat = (acc_ref[:, cs] / lsum[:, cs]).astype(BF16)
        out = jnp.dot(wuvt_ref[h], o_lat, preferred_element_type=F32)
        o_ref[0, h * DH_A:(h + 1) * DH_A, :] = out.astype(o_ref.dtype)


def _dsa_attn(qlat, qidx, widx, kidx, ckv, wuv, bias):
    bsz, l, _ = ckv.shape
    nb = l // Q_BLOCK
    topk = min(TOPK_MAX, l // 4)
    nchunks = 2 * ((l + 2 * KEY_CHUNK - 1) // (2 * KEY_CHUNK))
    hq = H_A * Q_BLOCK
    widx_t = widx.reshape(bsz, nb, Q_BLOCK, H_I).transpose(0, 1, 3, 2)
    ckv_t = ckv.transpose(0, 2, 1)
    wuv_t = wuv.transpose(0, 2, 1)
    return pl.pallas_call(
        functools.partial(_dsa_attn_kernel, topk=topk),
        grid=(bsz, nb),
        in_specs=[pl.BlockSpec((1, 1, R_KV, hq), lambda b, i: (b, i, 0, 0)),
                  pl.BlockSpec((1, 1, D_I, H_I * Q_BLOCK), lambda b, i: (b, i, 0, 0)),
                  pl.BlockSpec((1, 1, H_I, Q_BLOCK), lambda b, i: (b, i, 0, 0)),
                  pl.BlockSpec((1, l, D_I), lambda b, i: (b, 0, 0)),
                  pl.BlockSpec((1, l, R_KV), lambda b, i: (b, 0, 0)),
                  pl.BlockSpec((1, R_KV, l), lambda b, i: (b, 0, 0)),
                  pl.BlockSpec((H_A, DH_A, R_KV), lambda b, i: (0, 0, 0)),
                  pl.BlockSpec((4, H_A, KEY_CHUNK, Q_BLOCK), lambda b, i: (0, 0, 0, 0))],
        out_specs=pl.BlockSpec((1, H_A * DH_A, Q_BLOCK), lambda b, i: (b, 0, i)),
        out_shape=jax.ShapeDtypeStruct((bsz, H_A * DH_A, l), BF16),
        scratch_shapes=[pltpu.VMEM((nchunks, KEY_CHUNK, Q_BLOCK), jnp.int32),
                        pltpu.VMEM((nchunks, KEY_CHUNK, Q_BLOCK), jnp.int16),
                        pltpu.VMEM((nchunks, KEY_CHUNK, Q_BLOCK), jnp.int16),
                        pltpu.VMEM((1, hq), F32), pltpu.VMEM((8, hq), F32), pltpu.VMEM((R_KV, hq), F32),
                        pltpu.VMEM((KEY_CHUNK, H_I * Q_BLOCK), F32), pltpu.VMEM((KEY_CHUNK, H_I * Q_BLOCK), F32),
                        pltpu.VMEM((KEY_CHUNK, hq), F32), pltpu.VMEM((KEY_CHUNK, hq), F32),
                        pltpu.VMEM((KEY_CHUNK, hq), BF16), pltpu.VMEM((KEY_CHUNK, hq), BF16),
                        pltpu.VMEM((1, hq), F32), pltpu.VMEM((1, hq), F32)],
        compiler_params=_cparams("parallel", "arbitrary"),
        name="dsa_attn",
    )(qlat, qidx, widx_t, kidx, ckv, ckv_t, wuv_t, bias)


def _seg_sum(x, bd):
    hi = x.astype(BF16)
    lo = (x - hi.astype(F32)).astype(BF16)
    outs = []
    for p in range(x.shape[1] // LANES):
        sl = slice(p * LANES, (p + 1) * LANES)
        outs.append(jnp.dot(hi[:, sl], bd, preferred_element_type=F32)
                    + jnp.dot(lo[:, sl], bd, preferred_element_type=F32))
    return outs[0] if len(outs) == 1 else jnp.concatenate(outs, axis=1)


def _rwkv_pre_kernel(pb_ref, mu_ref, w0_ref, a0_ref, kk_ref, ka_ref, rk_ref, wup_ref, aup_ref, gup_ref,
                     bd_ref, tril_ref, ab_ref, bb_ref, kb_ref, rb_ref, v_ref, p_ref, g_ref, bonus_ref,
                     carry_ref):
    @pl.when(pl.program_id(1) == 0)
    def _():
        carry_ref[...] = jnp.zeros_like(carry_ref)

    x = pb_ref[0].astype(F32)
    tm = x.shape[0]
    rows = lax.broadcasted_iota(jnp.int32, x.shape, 0)
    prev = jnp.where(rows == 0, carry_ref[...], pltpu.roll(x, 1, 0))
    carry_ref[...] = x[tm - 1:tm]
    ps = x + mu_ref[...] * (prev - x)
    r = ps[:, :W_B]
    k = ps[:, W_B:2 * W_B]
    v = ps[:, 2 * W_B:3 * W_B]
    o = 3 * W_B
    wd = ps[:, o:o + LORA_W]
    ad = ps[:, o + LORA_W:o + LORA_W + LORA_A]
    gd = ps[:, o + LORA_W + LORA_A:]
    hp = lax.Precision.HIGHEST
    z = w0_ref[...] + jnp.dot(jnp.tanh(wd), wup_ref[...], precision=hp, preferred_element_type=F32)
    u = -z
    w = -(jnp.maximum(u, 0.0) + jnp.log(1.0 + jnp.exp(-jnp.abs(u)))) - 0.5
    logd = -jnp.exp(w)
    logp = jnp.dot(tril_ref[...], logd, precision=hp, preferred_element_type=F32)
    pinv = jnp.exp(-logp)
    a = _sigmoid(a0_ref[...] + jnp.dot(ad, aup_ref[...], precision=hp, preferred_element_type=F32))
    g_ref[0] = jnp.dot(_sigmoid(gd).astype(BF16), gup_ref[...], preferred_element_type=F32)
    bd = bd_ref[...]
    kk = k * kk_ref[...]
    kk = kk * lax.rsqrt(jnp.maximum(_seg_sum(kk * kk, bd), 1e-24))
    k2 = k * (1.0 + (a - 1.0) * ka_ref[...])
    p = jnp.exp(logp)
    ab_ref[0] = (-kk * jnp.exp(logp - logd)).astype(BF16)
    bb_ref[0] = (kk * a * pinv).astype(BF16)
    kb_ref[0] = (k2 * pinv).astype(BF16)
    rb_ref[0] = (r * p).astype(BF16)
    v_ref[0] = v.astype(BF16)
    for j in range(tm // RW_CHUNK):
        p_ref[0, j] = p[(j + 1) * RW_CHUNK - 1:(j + 1) * RW_CHUNK]
    bonus_ref[0] = _seg_sum(r * k2 * rk_ref[...], bd) * v


def _rwkv_pre(pb, mu, w0, a0, k_k, k_a, r_k, w_up, a_up, g_up, bd, tm=128):
    bsz, l, _ = pb.shape
    tm = min(tm, l)
    assert tm % RW_CHUNK == 0
    tril = jnp.kron(jnp.eye(tm // RW_CHUNK, dtype=F32), jnp.tril(jnp.ones((RW_CHUNK, RW_CHUNK), F32)))
    full = lambda shp: pl.BlockSpec(shp, lambda b, i: (0,) * len(shp))
    row = pl.BlockSpec((1, tm, W_B), lambda b, i: (b, i, 0))
    vec = full((1, W_B))
    return pl.pallas_call(
        _rwkv_pre_kernel,
        grid=(bsz, l // tm),
        in_specs=[pl.BlockSpec((1, tm, B_PAD), lambda b, i: (b, i, 0)), full((1, B_PAD)),
                  vec, vec, vec, vec, vec,
                  full((LORA_W, W_B)), full((LORA_A, W_B)), full((LORA_G_PAD, W_B)), full((LANES, LANES)),
                  full((tm, tm))],
        out_specs=[row] * 5 + [pl.BlockSpec((1, tm // RW_CHUNK, 1, W_B), lambda b, i: (b, i, 0, 0)), row, row],
        out_shape=([jax.ShapeDtypeStruct((bsz, l, W_B), BF16)] * 5
                   + [jax.ShapeDtypeStruct((bsz, l // RW_CHUNK, 1, W_B), F32)]
                   + [jax.ShapeDtypeStruct((bsz, l, W_B), F32)] * 2),
        scratch_shapes=[pltpu.VMEM((1, B_PAD), F32)],
        compiler_params=_cparams("parallel", "arbitrary"),
        name="rwkv_pre",
    )(pb, mu, w0.reshape(1, W_B), a0.reshape(1, W_B), k_k.reshape(1, W_B), k_a.reshape(1, W_B),
      r_k.reshape(1, W_B), w_up, a_up, g_up, bd, tril)


def _mm(a, b, dims, passes):
    dg = lambda x, y: lax.dot_general(x, y, dims, preferred_element_type=F32)
    ah, bh = a.astype(BF16), b.astype(BF16)
    if passes == 1:
        return dg(ah, bh)
    al = (a - ah.astype(F32)).astype(BF16)
    bl = (b - bh.astype(F32)).astype(BF16)
    return dg(ah, bh) + (dg(ah, bl) + dg(al, bh))


def _rwkv_chunk_kernel(ab_ref, bb_ref, kb_ref, rb_ref, v_ref, pe_ref, y_ref, s_ref, *, npair):
    @pl.when(pl.program_id(1) == 0)
    def _():
        s_ref[...] = jnp.zeros_like(s_ref)

    c = RW_CHUNK
    ri = lax.broadcasted_iota(jnp.int32, (c, c), 0)
    ci = lax.broadcasted_iota(jnp.int32, (c, c), 1)
    strict, incl = ci < ri, ci <= ri
    eye = jnp.where(ri == ci, 1.0, 0.0)
    head0 = lax.broadcasted_iota(jnp.int32, (1, LANES), 1) < N_B
    bi = lax.broadcasted_iota(jnp.int32, (LANES, LANES), 0) < N_B
    bj = lax.broadcasted_iota(jnp.int32, (LANES, LANES), 1) < N_B
    blockdiag = bi == bj
    nt = (((2,), (2,)), ((0,), (0,)))
    nn = (((2,), (1,)), ((0,), (0,)))
    tn = (((1,), (1,)), ((0,), (0,)))
    tiles = lambda ref: jnp.stack([ref[0, :, p * LANES:(p + 1) * LANES] for p in range(npair)])
    ab, bb, kb, rb, v = tiles(ab_ref), tiles(bb_ref), tiles(kb_ref), tiles(rb_ref), tiles(v_ref)
    pe = jnp.stack([pe_ref[0, 0, :, p * LANES:(p + 1) * LANES] for p in range(npair)])
    s0 = s_ref[...]
    both = lambda x: jnp.concatenate([x, x], axis=0)
    pick = lambda x: jnp.where(head0, x[:npair], x[npair:])
    split = lambda x: jnp.concatenate([jnp.where(head0, x, 0.0), jnp.where(head0, 0.0, x)], axis=0)
    ar = jnp.concatenate([ab, rb], axis=1)
    ars = split(ar)
    gb = _mm(ars, both(bb), nt, RW_PASSES)
    gk = _mm(ars, both(kb), nt, RW_PASSES)
    m_ab = jnp.where(strict, gb[:, :c], 0.0)
    n_rb = jnp.where(incl, gb[:, c:], 0.0)
    m_ak = jnp.where(strict, gk[:, :c], 0.0)
    n_rk = jnp.where(incl, gk[:, c:], 0.0)
    tinv = eye + m_ab
    pw = m_ab
    for _ in range(5):
        pw = _mm(pw, pw, nn, RW_PASSES)
        tinv = tinv + _mm(tinv, pw, nn, RW_PASSES)
    x0 = _mm(ar, s0, nt, RW_PASSES)
    v2 = both(v)
    x = x0[:, :c] + pick(_mm(m_ak, v2, nn, RW_PASSES))
    u = pick(_mm(tinv, both(x), nn, RW_PASSES))
    y = x0[:, c:] + pick(_mm(n_rb, both(u), nn, RW_PASSES)) + pick(_mm(n_rk, v2, nn, RW_PASSES))
    for p in range(npair):
        y_ref[0, :, p * LANES:(p + 1) * LANES] = y[p]
    upd = _mm(jnp.concatenate([u.astype(BF16), v], axis=1), jnp.concatenate([bb, kb], axis=1), tn, RW_PASSES)
    s_ref[...] = (s0 + jnp.where(blockdiag, upd, 0.0)) * pe


def _rwkv_chunks(ab, bb, kb, rb, v, pe):
    bsz, l, wb = ab.shape
    nt = l // RW_CHUNK
    npair = wb // LANES
    row = pl.BlockSpec((1, RW_CHUNK, wb), lambda b, i: (b, i, 0))
    return pl.pallas_call(
        functools.partial(_rwkv_chunk_kernel, npair=npair),
        grid=(bsz, nt),
        in_specs=[row] * 5 + [pl.BlockSpec((1, 1, 1, wb), lambda b, i: (b, i, 0, 0))],
        out_specs=row,
        out_shape=jax.ShapeDtypeStruct((bsz, l, wb), F32),
        scratch_shapes=[pltpu.VMEM((npair, LANES, LANES), F32)],
        compiler_params=_cparams("parallel", "arbitrary"),
        name="rwkv_chunks",
    )(ab, bb, kb, rb, v, pe)


def _rwkv_post_kernel(y_ref, bonus_ref, g_ref, lw_ref, lb_ref, bd_ref, o_ref):
    bd = bd_ref[...]
    y = y_ref[0]
    mean = _seg_sum(y, bd) * (1.0 / N_B)
    c = y - mean
    var = _seg_sum(c * c, bd) * (1.0 / N_B)
    yn = c * lax.rsqrt(var + GN_EPS) * lw_ref[...] + lb_ref[...]
    o_ref[0] = ((yn + bonus_ref[0]) * g_ref[0]).astype(o_ref.dtype)


def _rwkv_post(y, bonus, g, lnx_w, lnx_b, bd, tm=256):
    bsz, l, _ = y.shape
    tm = min(tm, l)
    row = pl.BlockSpec((1, tm, W_B), lambda b, i: (b, i, 0))
    vec = pl.BlockSpec((1, W_B), lambda b, i: (0, 0))
    return pl.pallas_call(
        _rwkv_post_kernel,
        grid=(bsz, l // tm),
        in_specs=[row, row, row, vec, vec, pl.BlockSpec((LANES, LANES), lambda b, i: (0, 0))],
        out_specs=row,
        out_shape=jax.ShapeDtypeStruct((bsz, l, W_B), BF16),
        compiler_params=_cparams("parallel", "parallel"),
        name="rwkv_post",
    )(y, bonus, g, lnx_w.reshape(1, W_B), lnx_b.reshape(1, W_B), bd)


def _rwkv_mixer(pb, mu, w0, w_up, a0, a_up, g_up, k_k, k_a, r_k, lnx_w, lnx_b):
    eye2 = jnp.kron(jnp.eye(2, dtype=F32), jnp.ones((N_B, N_B), F32)).astype(BF16)
    ab, bb, kb, rb, v, pe, g, bonus = _rwkv_pre(pb, mu, w0, a0, k_k, k_a, r_k, w_up, a_up, g_up, eye2)
    y = _rwkv_chunks(ab, bb, kb, rb, v, pe)
    return _rwkv_post(y, bonus, g, lnx_w, lnx_b, eye2)


def _hgrn_kernel(q_ref, f_ref, i_ref, g_ref, lb_ref, on_ref, tril_ref, o_ref, st_ref, *, layer, hb, tt):
    @pl.when(pl.program_id(2) == 0)
    def _():
        st_ref[...] = jnp.zeros_like(st_ref)

    x = lb_ref[...]
    e = jnp.exp(x - jnp.max(x, axis=0, keepdims=True))
    sm = e / jnp.sum(e, axis=0, keepdims=True)
    cs = sm[0:1]
    for i in range(1, layer + 1):
        cs = cs + sm[i:i + 1]
    lb_all = cs - sm[0:1]
    rowi = lax.broadcasted_iota(jnp.int32, (SUB, LANES), 0)
    lanei = lax.broadcasted_iota(jnp.int32, (SUB, LANES), 1)
    prep = []
    for hh in range(hb):
        sl = slice(hh * LANES, (hh + 1) * LANES)
        lb = lb_all[:, sl]
        qraw = q_ref[0][:, sl].astype(F32)
        ff = f_ref[0][:, sl].astype(F32)
        q = qraw * _sigmoid(qraw)
        logsig = jnp.minimum(ff, 0.0) - jnp.log(1.0 + jnp.exp(-jnp.abs(ff)))
        la = jnp.log(lb)
        lbv = jnp.log(1.0 - lb) + logsig
        mx = jnp.maximum(la, lbv)
        logf = mx + jnp.log(jnp.exp(la - mx) + jnp.exp(lbv - mx))
        k = (1.0 - lb) * _sigmoid(-ff)
        bcum = jnp.dot(tril_ref[...], logf * LOG2E, precision=lax.Precision.HIGHEST, preferred_element_type=F32)
        prep.append((q, k, i_ref[0][:, sl].astype(F32), bcum))
    sts = [st_ref[hh] for hh in range(hb)]
    outs = [[] for _ in range(hb)]
    for n in range(tt // SUB):
        rs = slice(n * SUB, (n + 1) * SUB)
        for hh in range(hb):
            q, k, v, bcum = prep[hh]
            st = sts[hh]
            bq, qn, kn, vn = bcum[rs], q[rs], k[rs], v[rs]
            bend = bq[SUB - 1:SUB]
            qd = (qn * jnp.exp2(bq)).astype(BF16)
            o = lax.dot_general(qd, st.astype(BF16), NT_DIMS, preferred_element_type=F32)
            a = jnp.zeros((SUB, LANES), F32)
            for s in range(SUB):
                wgt = kn[s:s + 1] * jnp.exp2(jnp.minimum(bq - bq[s:s + 1], 0.0))
                a = jnp.where(lanei == s, jnp.sum(qn * wgt, axis=1, keepdims=True), a)
            a = jnp.where(lanei <= rowi, a, 0.0)[:, :SUB]
            o = o + jnp.dot(a.astype(BF16), vn.astype(BF16), preferred_element_type=F32)
            kd = (kn * jnp.exp2(bend - bq)).astype(BF16)
            sts[hh] = st * jnp.exp2(bend) + lax.dot_general(vn.astype(BF16), kd, TN_DIMS,
                                                           preferred_element_type=F32)
            outs[hh].append(o)
    for hh in range(hb):
        sl = slice(hh * LANES, (hh + 1) * LANES)
        st_ref[hh] = sts[hh]
        gg = g_ref[0][:, sl].astype(F32)
        o = jnp.concatenate(outs[hh], axis=0)
        o = _rms(o) * on_ref[:, sl] * (gg * _sigmoid(gg))
        o_ref[0, :, sl] = o.astype(o_ref.dtype)


def _hgrn_mixer(pc, hgrn_lb, onorm, layer, hb=12, tt=128):
    bsz, l, _ = pc.shape
    depth = hgrn_lb.shape[0]
    tt = min(tt, l)
    ng = H_C // hb
    wblk = hb * LANES
    tril = jnp.kron(jnp.eye(tt // SUB, dtype=F32), jnp.tril(jnp.ones((SUB, SUB), F32)))
    col = lambda j: pl.BlockSpec((1, tt, wblk), lambda b, h, i, j=j: (b, i, j * ng + h))
    return pl.pallas_call(
        functools.partial(_hgrn_kernel, layer=layer, hb=hb, tt=tt),
        grid=(bsz, ng, l // tt),
        in_specs=[col(0), col(1), col(2), col(3),
                  pl.BlockSpec((depth, wblk), lambda b, h, i: (0, h)),
                  pl.BlockSpec((1, wblk), lambda b, h, i: (0, h)),
                  pl.BlockSpec((tt, tt), lambda b, h, i: (0, 0))],
        out_specs=pl.BlockSpec((1, tt, wblk), lambda b, h, i: (b, i, h)),
        out_shape=jax.ShapeDtypeStruct((bsz, l, W_C), BF16),
        scratch_shapes=[pltpu.VMEM((hb, DV_C, DK_C), F32)],
        compiler_params=_cparams("parallel", "parallel", "arbitrary"),
        name="hgrn2",
    )(pc, pc, pc, pc, hgrn_lb, onorm.reshape(1, W_C), tril)


def _pack_w_in(w):
    d = w.shape[0]
    a_cols = R_Q + R_KV + D_I + H_I
    b_cols = 3 * W_B + LORA_W + LORA_A + LORA_G
    wa, wb, wc = w[:, :a_cols], w[:, a_cols:a_cols + b_cols], w[:, a_cols + b_cols:]
    z = lambda n: jnp.zeros((d, n), w.dtype)
    o = R_Q + R_KV
    wa = jnp.concatenate([wa[:, :o + D_I], z(LANES - D_I), wa[:, o + D_I:], z(LANES - H_I)], axis=1)
    wb = jnp.concatenate([wb, z(LORA_G_PAD - LORA_G)], axis=1)
    return wa.astype(BF16), wb.astype(BF16), wc.astype(BF16)


def kernel(x, c, rel_bias, hgrn_lb, ada_w, ada_b, norm_g, w_in, w_out, mla_q_norm, mla_kv_norm, w_uq, w_uk, w_uv, w_qidx, rwkv_mu, rwkv_w0, rwkv_w_up, rwkv_a0, rwkv_a_up, rwkv_g_up, rwkv_k_k, rwkv_k_a, rwkv_r_k, rwkv_lnx_w, rwkv_lnx_b, hgrn_onorm, w_ff1, w_ff2):
    bsz, l, d = x.shape
    depth = ada_w.shape[0]
    m = bsz * l
    mod = _ada_mod(c, ada_w, ada_b)
    mods = [[mod[i, :, None, j * d:(j + 1) * d] for j in range(6)] for i in range(depth)]
    bias = _bias_tiles(rel_bias)
    h = _norm_mod(x, norm_g[0, 0], mods[0][1], mods[0][0])
    for i in range(depth):
        sh_m, sc_m, g_m, sh_f, sc_f, g_f = mods[i]
        wa, wb, wc = _pack_w_in(w_in[i])
        h2d = h.reshape(m, d)
        pa = _matmul(h2d, wa, F32, 512, A_PAD, 4096, name="in_proj_a").reshape(bsz, l, -1)
        pb = _matmul(h2d, wb, BF16, 1024, 768, 4096, name="in_proj_b").reshape(bsz, l, -1)
        pc = _matmul(h2d, wc, BF16, 1024, 1024, 4096, name="in_proj_c").reshape(bsz, l, -1)
        wuq = w_uq[i].reshape(R_Q, H_A * DH_A).astype(BF16)
        wqi = jnp.transpose(w_qidx[i], (1, 2, 0)).astype(BF16)
        qlat, qidx, widx, kidx, ckv = _dsa_prep(pa, mla_q_norm[i], mla_kv_norm[i], wuq,
                                                w_uk[i].astype(BF16), wqi)
        y_a = _dsa_attn(qlat, qidx, widx, kidx, ckv, w_uv[i].astype(BF16), bias).transpose(0, 2, 1)
        mu = jnp.concatenate([rwkv_mu[i], jnp.zeros((LORA_G_PAD - LORA_G,), F32)]).reshape(1, B_PAD)
        gup = jnp.concatenate([rwkv_g_up[i], jnp.zeros((LORA_G_PAD - LORA_G, W_B), F32)]).astype(BF16)
        y_b = _rwkv_mixer(pb, mu, rwkv_w0[i], rwkv_w_up[i], rwkv_a0[i], rwkv_a_up[i], gup,
                          rwkv_k_k[i], rwkv_k_a[i], rwkv_r_k[i], rwkv_lnx_w[i], rwkv_lnx_b[i])
        y_c = _hgrn_mixer(pc, hgrn_lb, hgrn_onorm[i], i)
        ycat = jnp.concatenate([y_a, y_b, y_c], axis=-1).reshape(m, -1)
        y = _matmul(ycat, w_out[i].astype(BF16), F32, 1024, 1024, 4096, name="out_proj").reshape(bsz, l, d)
        x, h = _resid_norm(x, y, g_m, norm_g[i, 1], (norm_g[i, 2], sc_f, sh_f))
        u = _matmul(h.reshape(m, d), w_ff1[i].astype(BF16), BF16, 1024, 1024, 4096, relu2=True, name="ffn_up")
        y = _matmul(u, w_ff2[i].astype(BF16), F32, 1024, 1024, 4096, name="ffn_down").reshape(bsz, l, d)
        nxt = None if i == depth - 1 else (norm_g[i + 1, 0], mods[i + 1][1], mods[i + 1][0])
        x, h = _resid_norm(x, y, g_f, norm_g[i, 3], nxt)
    return x
```

```python
import functools
import math

import jax
import jax.numpy as jnp
from jax import lax
from jax.experimental import pallas as pl
from jax.experimental.pallas import tpu as pltpu

H_A, DH_A, R_Q, R_KV, H_I, D_I = 8, 128, 768, 256, 16, 64
TOPK_MAX, Q_BLOCK, N_BUCKETS, MAX_DIST = 256, 128, 32, 128
H_B, N_B, LORA_W, LORA_A, LORA_G = 24, 64, 128, 128, 480
W_B = H_B * N_B
H_C, DK_C, DV_C = 12, 128, 128
W_C = H_C * DK_C
EPS = 1e-6
GN_EPS = 64e-5

LANES = 128
KEY_CHUNK = 256
SUB = 8
RW_CHUNK = 64
SUBLANES = 8
I16_MIN, I16_MAX = -2 ** 15, 2 ** 15 - 1
LORA_G_PAD = 512
A_PAD = 1280
B_PAD = 3 * W_B + LORA_W + LORA_A + LORA_G_PAD
VMEM_LIMIT = 56 * 1024 * 1024
MM_TILES = {"in_proj_a": (512, A_PAD, 4096), "in_proj_b": (1024, 768, 4096), "in_proj_c": (1024, 1024, 4096),
            "out_proj": (1024, 1024, 4096), "ffn_up": (1024, 1024, 4096), "ffn_down": (1024, 1024, 4096)}

F32 = jnp.float32
BF16 = jnp.bfloat16
NT_DIMS = (((1,), (1,)), ((), ()))
TN_DIMS = (((0,), (0,)), ((), ()))
INT_MIN = -2 ** 31
LOG2E = math.log2(math.e)


def _cparams(*sem):
    return pltpu.CompilerParams(dimension_semantics=sem, vmem_limit_bytes=VMEM_LIMIT)


def _sigmoid(x):
    return 1.0 / (1.0 + jnp.exp(-x))


def _rms(x, eps=EPS):
    return x * lax.rsqrt(jnp.mean(x * x, axis=-1, keepdims=True) + eps)


def _mm_kernel(a_ref, b_ref, o_ref, acc_ref, *, nk, relu2):
    k = pl.program_id(2)

    @pl.when(k == 0)
    def _():
        acc_ref[...] = jnp.zeros_like(acc_ref)

    acc_ref[...] += jnp.dot(a_ref[...], b_ref[...], preferred_element_type=F32)

    @pl.when(k == nk - 1)
    def _():
        r = acc_ref[...]
        if relu2:
            r = jnp.square(jnp.maximum(r, 0.0))
        o_ref[...] = r.astype(o_ref.dtype)


def _mm_full_k_kernel(a_ref, b_ref, o_ref, *, relu2):
    r = jnp.dot(a_ref[...], b_ref[...], preferred_element_type=F32)
    if relu2:
        r = jnp.square(jnp.maximum(r, 0.0))
    o_ref[...] = r.astype(o_ref.dtype)


def _matmul(a, b, out_dtype, name, relu2=False):
    m, kdim = a.shape
    _, n = b.shape
    tm, tn, tk = (min(t, full) for t, full in zip(MM_TILES[name], (m, n, kdim)))
    assert m % tm == 0 and n % tn == 0 and kdim % tk == 0, (a.shape, b.shape, tm, tn, tk)
    nk = kdim // tk
    if nk == 1:
        return pl.pallas_call(
            functools.partial(_mm_full_k_kernel, relu2=relu2),
            grid=(m // tm, n // tn),
            in_specs=[pl.BlockSpec((tm, kdim), lambda i, j: (i, 0)),
                      pl.BlockSpec((kdim, tn), lambda i, j: (0, j))],
            out_specs=pl.BlockSpec((tm, tn), lambda i, j: (i, j)),
            out_shape=jax.ShapeDtypeStruct((m, n), out_dtype),
            compiler_params=_cparams("parallel", "parallel"),
            name=name,
        )(a, b)
    return pl.pallas_call(
        functools.partial(_mm_kernel, nk=nk, relu2=relu2),
        grid=(m // tm, n // tn, nk),
        in_specs=[pl.BlockSpec((tm, tk), lambda i, j, k: (i, k)),
                  pl.BlockSpec((tk, tn), lambda i, j, k: (k, j))],
        out_specs=pl.BlockSpec((tm, tn), lambda i, j, k: (i, j)),
        out_shape=jax.ShapeDtypeStruct((m, n), out_dtype),
        scratch_shapes=[pltpu.VMEM((tm, tn), F32)],
        compiler_params=_cparams("parallel", "parallel", "arbitrary"),
        name=name,
    )(a, b)


def _ada_kernel(c_ref, w_ref, b_ref, o_ref):
    k = pl.program_id(2)

    @pl.when(k == 0)
    def _():
        o_ref[0] = jnp.broadcast_to(b_ref[0], o_ref.shape[1:])

    c = c_ref[...]
    ca = (c * _sigmoid(c)).astype(BF16)
    o_ref[0] += jnp.dot(ca, w_ref[0].astype(BF16), preferred_element_type=F32)


def _ada_mod(c, ada_w, ada_b):
    nl, d, n = ada_w.shape
    bsz = c.shape[0]
    rows = SUBLANES
    cp = jnp.zeros((rows, d), F32).at[:bsz].set(c)
    tn, tk = min(2048, n), min(1024, d)
    out = pl.pallas_call(
        _ada_kernel,
        grid=(nl, n // tn, d // tk),
        in_specs=[pl.BlockSpec((rows, tk), lambda l, j, k: (0, k)),
                  pl.BlockSpec((1, tk, tn), lambda l, j, k: (l, k, j)),
                  pl.BlockSpec((1, 1, tn), lambda l, j, k: (l, 0, j))],
        out_specs=pl.BlockSpec((1, rows, tn), lambda l, j, k: (l, 0, j)),
        out_shape=jax.ShapeDtypeStruct((nl, rows, n), F32),
        compiler_params=_cparams("parallel", "parallel", "arbitrary"),
        name="ada_mod",
    )(cp, ada_w, ada_b.reshape(nl, 1, n))
    return out[:, :bsz]


def _norm_mod_kernel(x_ref, g_ref, sc_ref, sh_ref, h_ref):
    y = _rms(x_ref[0]) * g_ref[...]
    h_ref[0] = (y * (1.0 + sc_ref[0]) + sh_ref[0]).astype(h_ref.dtype)


def _norm_mod(x, g, sc, sh, tm=256):
    bsz, l, d = x.shape
    tm = min(tm, l)
    row = pl.BlockSpec((1, tm, d), lambda b, i: (b, i, 0))
    vec = pl.BlockSpec((1, d), lambda b, i: (0, 0))
    bvec = pl.BlockSpec((1, 1, d), lambda b, i: (b, 0, 0))
    return pl.pallas_call(
        _norm_mod_kernel,
        grid=(bsz, l // tm),
        in_specs=[row, vec, bvec, bvec],
        out_specs=row,
        out_shape=jax.ShapeDtypeStruct((bsz, l, d), BF16),
        compiler_params=_cparams("parallel", "parallel"),
        name="norm_mod",
    )(x, g.reshape(1, d), sc, sh)


def _resid_kernel(x_ref, y_ref, gate_ref, g1_ref, g2_ref, sc_ref, sh_ref, xo_ref, h_ref):
    xn = x_ref[0] + gate_ref[0] * (_rms(y_ref[0]) * g1_ref[...])
    xo_ref[0] = xn
    hn = _rms(xn) * g2_ref[...]
    h_ref[0] = (hn * (1.0 + sc_ref[0]) + sh_ref[0]).astype(h_ref.dtype)


def _resid_last_kernel(x_ref, y_ref, gate_ref, g1_ref, xo_ref):
    xo_ref[0] = x_ref[0] + gate_ref[0] * (_rms(y_ref[0]) * g1_ref[...])


def _resid_norm(x, y, gate, g1, nxt=None, tm=128):
    bsz, l, d = x.shape
    tm = min(tm, l)
    row = pl.BlockSpec((1, tm, d), lambda b, i: (b, i, 0))
    vec = pl.BlockSpec((1, d), lambda b, i: (0, 0))
    bvec = pl.BlockSpec((1, 1, d), lambda b, i: (b, 0, 0))
    if nxt is None:
        return pl.pallas_call(
            _resid_last_kernel,
            grid=(bsz, l // tm),
            in_specs=[row, row, bvec, vec],
            out_specs=row,
            out_shape=jax.ShapeDtypeStruct((bsz, l, d), F32),
            compiler_params=_cparams("parallel", "parallel"),
            name="resid_last",
        )(x, y, gate, g1.reshape(1, d)), None
    g2, sc, sh = nxt
    return pl.pallas_call(
        _resid_kernel,
        grid=(bsz, l // tm),
        in_specs=[row, row, bvec, vec, vec, bvec, bvec],
        out_specs=[row, row],
        out_shape=[jax.ShapeDtypeStruct((bsz, l, d), F32), jax.ShapeDtypeStruct((bsz, l, d), BF16)],
        compiler_params=_cparams("parallel", "parallel"),
        name="resid_norm",
    )(x, y, gate, g1.reshape(1, d), g2.reshape(1, d), sc, sh)


def _bias_kernel(rb_ref, o_ref):
    j = lax.broadcasted_iota(jnp.int32, (KEY_CHUNK, Q_BLOCK), 0)
    i = lax.broadcasted_iota(jnp.int32, (KEY_CHUNK, Q_BLOCK), 1)
    max_exact = N_BUCKETS // 2
    for v, off in enumerate((Q_BLOCK, 0, 2 * Q_BLOCK, None)):
        if off is None:
            dist = jnp.full((KEY_CHUNK, Q_BLOCK), 2 * MAX_DIST, jnp.int32)
        else:
            dist = jnp.maximum(i + off - j, 0)
        nf = jnp.maximum(dist, 1).astype(F32)
        large = max_exact + (jnp.log(nf / max_exact) / math.log(MAX_DIST / max_exact)
                             * (N_BUCKETS - max_exact)).astype(jnp.int32)
        large = jnp.minimum(large, N_BUCKETS - 1)
        bucket = jnp.where(dist < max_exact, dist, large)
        for h in range(H_A):
            def body(b, acc, h=h, bucket=bucket):
                return jnp.where(bucket == b, rb_ref[b * H_A + h] * LOG2E, acc)
            o_ref[v, h] = lax.fori_loop(0, N_BUCKETS, body, jnp.zeros((KEY_CHUNK, Q_BLOCK), F32))


def _bias_tiles(rel_bias):
    return pl.pallas_call(
        _bias_kernel,
        in_specs=[pl.BlockSpec(memory_space=pltpu.SMEM)],
        out_specs=pl.BlockSpec(memory_space=pltpu.VMEM),
        out_shape=jax.ShapeDtypeStruct((4, H_A, KEY_CHUNK, Q_BLOCK), F32),
        name="dsa_bias_tiles",
    )(rel_bias.reshape(-1))


def _dsa_prep_kernel(pa_ref, gq_ref, gkv_ref, wuq_ref, wuk_ref, wqi_ref,
                     qlat_ref, qidx_ref, widx_ref, kidx_ref, ckv_ref):
    pa = pa_ref[0]
    cq = pa[:, :R_Q]
    ckv = pa[:, R_Q:R_Q + R_KV]
    kid = pa[:, R_Q + R_KV:R_Q + R_KV + D_I]
    wid = pa[:, R_Q + R_KV + LANES:R_Q + R_KV + LANES + H_I]
    cqn = (_rms(cq) * gq_ref[...]).astype(BF16)
    ckv_ref[0] = (_rms(ckv) * gkv_ref[...]).astype(BF16)
    kidx_ref[0] = kid.astype(BF16)
    widx_ref[0] = wid * (H_I ** -0.5 * D_I ** -0.5)
    q = jnp.dot(cqn, wuq_ref[...], preferred_element_type=F32)
    for h in range(H_A):
        qh = q[:, h * DH_A:(h + 1) * DH_A].astype(BF16)
        ql = lax.dot_general(wuk_ref[h], qh, NT_DIMS, preferred_element_type=F32)
        qlat_ref[0, 0, :, h * Q_BLOCK:(h + 1) * Q_BLOCK] = (ql * (DH_A ** -0.5 * LOG2E)).astype(BF16)
    for h in range(H_I):
        qi = lax.dot_general(wqi_ref[h], cqn, NT_DIMS, preferred_element_type=F32)
        qidx_ref[0, 0, :, h * Q_BLOCK:(h + 1) * Q_BLOCK] = qi.astype(BF16)


def _dsa_prep(pa, gq, gkv, wuq, wuk, wqi):
    bsz, l, _ = pa.shape
    nb = l // Q_BLOCK
    full = lambda shp: pl.BlockSpec(shp, lambda b, i: (0,) * len(shp))
    return pl.pallas_call(
        _dsa_prep_kernel,
        grid=(bsz, nb),
        in_specs=[pl.BlockSpec((1, Q_BLOCK, A_PAD), lambda b, i: (b, i, 0)),
                  full((1, R_Q)), full((1, R_KV)), full((R_Q, H_A * DH_A)),
                  full((H_A, R_KV, DH_A)), full((H_I, D_I, R_Q))],
        out_specs=[pl.BlockSpec((1, 1, R_KV, H_A * Q_BLOCK), lambda b, i: (b, i, 0, 0)),
                   pl.BlockSpec((1, 1, D_I, H_I * Q_BLOCK), lambda b, i: (b, i, 0, 0)),
                   pl.BlockSpec((1, Q_BLOCK, H_I), lambda b, i: (b, i, 0)),
                   pl.BlockSpec((1, Q_BLOCK, D_I), lambda b, i: (b, i, 0)),
                   pl.BlockSpec((1, Q_BLOCK, R_KV), lambda b, i: (b, i, 0))],
        out_shape=[jax.ShapeDtypeStruct((bsz, nb, R_KV, H_A * Q_BLOCK), BF16),
                   jax.ShapeDtypeStruct((bsz, nb, D_I, H_I * Q_BLOCK), BF16),
                   jax.ShapeDtypeStruct((bsz, l, H_I), F32),
                   jax.ShapeDtypeStruct((bsz, l, D_I), BF16),
                   jax.ShapeDtypeStruct((bsz, l, R_KV), BF16)],
        compiler_params=_cparams("parallel", "parallel"),
        name="dsa_prep",
    )(pa, gq.reshape(1, R_Q), gkv.reshape(1, R_KV), wuq, wuk, wqi)


def _fold8(x, op, ways=1):
    n = x.shape[0] // SUBLANES
    ways = min(ways, n)
    part = lambda g: x[g * SUBLANES:(g + 1) * SUBLANES]
    acc = [part(g) for g in range(ways)]
    for g in range(ways, n):
        acc[g % ways] = op(acc[g % ways], part(g))
    while len(acc) > 1:
        acc = [op(acc[i], acc[i + 1]) for i in range(0, len(acc), 2)]
    return acc[0]


def _dsa_attn_kernel(qlat_ref, qidx_ref, widx_ref, kidx_ref, ckv_ref, ckvt_ref, wuvt_ref, bias_ref, o_ref,
                     key_ref, hi_ref, lo_ref, m_ref, ls_ref, acc_ref, sa_ref, sb_ref, lga_ref, lgb_ref,
                     pa_ref, pb_ref, aa_ref, ab_ref, *, topk):
    qi = pl.program_id(1)
    nch = (qi * Q_BLOCK) // KEY_CHUNK + 1
    krow = lax.broadcasted_iota(jnp.int32, (KEY_CHUNK, Q_BLOCK), 0)
    qlane = lax.broadcasted_iota(jnp.int32, (KEY_CHUNK, Q_BLOCK), 1)
    t_abs = qi * Q_BLOCK + qlane
    w = widx_ref[0, 0]

    npair = (nch + 1) // 2
    last_chunk = kidx_ref.shape[1] // KEY_CHUNK - 1

    def chunk_start(c):
        return pl.multiple_of(jnp.minimum(c, last_chunk) * KEY_CHUNK, KEY_CHUNK)

    def score_mm(c, s_ref):
        kc = kidx_ref[0, pl.ds(chunk_start(c), KEY_CHUNK), :]
        s_ref[...] = jnp.dot(kc, qidx_ref[0, 0], preferred_element_type=F32)

    def score_keys(c, s_ref):
        acc = jnp.zeros((KEY_CHUNK, Q_BLOCK), F32)
        for h in range(H_I):
            acc = acc + w[h:h + 1] * jnp.maximum(s_ref[:, h * Q_BLOCK:(h + 1) * Q_BLOCK], 0.0)
        acc = jnp.where(c * KEY_CHUNK + krow <= t_abs, acc, -jnp.inf)
        bits = lax.bitcast_convert_type(acc, jnp.int32)
        key = bits ^ ((bits >> 31) & 0x7FFFFFFF)
        key_ref[c] = key
        hi_ref[c] = (key >> 16).astype(jnp.int16)
        lo_ref[c] = ((key & 0xFFFF) + I16_MIN).astype(jnp.int16)

    def score_body(j, carry):
        c = 2 * j
        score_mm(c + 1, sb_ref)
        score_keys(c, sa_ref)
        score_mm(c + 2, sa_ref)
        score_keys(c + 1, sb_ref)
        return carry

    score_mm(0, sa_ref)
    lax.fori_loop(0, npair, score_body, 0)

    def count_ge16(ref, cand):
        c16 = cand.astype(jnp.int16)
        rows = 2 * SUBLANES

        def fold(x):
            r = x[0:rows]
            for g in range(1, x.shape[0] // rows):
                r = r + x[g * rows:(g + 1) * rows]
            return r

        def hits(c):
            return fold(jnp.where(ref[c] >= c16, jnp.int16(1), jnp.int16(0)))

        def quad(c, cnt):
            return cnt + ((hits(4 * c) + hits(4 * c + 1)) + (hits(4 * c + 2) + hits(4 * c + 3)))

        def tail(_, cnt):
            return cnt + (hits(2 * npair - 2) + hits(2 * npair - 1))

        cnt = lax.fori_loop(0, npair // 2, quad, jnp.zeros((rows, Q_BLOCK), jnp.int16))
        cnt = lax.fori_loop(0, npair % 2, tail, cnt)
        return jnp.sum(cnt.astype(F32), axis=0, keepdims=True)

    kf = float(topk)
    total = (npair * (2 * KEY_CHUNK)).astype(F32)
    few = (qi * Q_BLOCK + lax.broadcasted_iota(jnp.int32, (1, Q_BLOCK), 1)) < topk

    def unsettled(cnt_thr):
        return jnp.max(jnp.where(few | (cnt_thr == kf), 0.0, 1.0)) > 0.5

    def search16(ref, early_exit, cnt_min):
        cnt0 = count_ge16(ref, jnp.zeros((1, Q_BLOCK), jnp.int32))
        thr = jnp.where(cnt0 >= kf, 0, I16_MIN).astype(jnp.int32)
        cnt_thr = jnp.where(cnt0 >= kf, cnt0, cnt_min)

        def group_cond(st):
            return (st[0] < 4) & st[3]

        def group_body(st):
            g, thr, cnt_thr, _ = st
            for k in range(4):
                i = 4 * g + k
                bit = jnp.where(i < 15, jnp.left_shift(jnp.int32(1), jnp.maximum(14 - i, 0)), 0)
                cand = thr + bit
                cnt = count_ge16(ref, cand)
                take = cnt >= kf
                thr = jnp.where(take, cand, thr)
                cnt_thr = jnp.where(take, cnt, cnt_thr)
            return g + 1, thr, cnt_thr, (unsettled(cnt_thr) if early_exit else jnp.bool_(True))

        go = unsettled(cnt_thr) if early_exit else jnp.bool_(True)
        _, thr, cnt_thr, _ = lax.while_loop(group_cond, group_body, (jnp.int32(0), thr, cnt_thr, go))
        return thr, cnt_thr

    thr_hi, cnt_hi = search16(hi_ref, False, total)
    thi16 = thr_hi.astype(jnp.int16)

    def relabel(c, carry):
        for cc in (2 * c, 2 * c + 1):
            hi = hi_ref[cc]
            lo_ref[cc] = jnp.where(hi > thi16, jnp.int16(I16_MAX),
                                   jnp.where(hi == thi16, lo_ref[cc], jnp.int16(I16_MIN)))
        return carry

    lax.fori_loop(0, npair, relabel, 0)
    thr_lo, _ = search16(lo_ref, True, cnt_hi)
    thr = thr_hi * 2 ** 16 + (thr_lo - I16_MIN)

    odd = (qi % 2) == 1

    def logits_mm(c, lg_ref):
        kv = ckv_ref[0, pl.ds(chunk_start(c), KEY_CHUNK), :]
        lg_ref[...] = jnp.dot(kv, qlat_ref[0, 0], preferred_element_type=F32)

    def chunk_mask(c):
        sel = (key_ref[c] >= thr) & (c * KEY_CHUNK + krow <= t_abs)
        idx = jnp.where(c == nch - 1, jnp.where(odd, 0, 1),
                        jnp.where((c == nch - 2) & jnp.logical_not(odd), 2, 3))
        return sel, idx

    m_ref[...] = jnp.full(m_ref.shape, -1e29, F32)
    ls_ref[...] = jnp.zeros(ls_ref.shape, F32)
    acc_ref[...] = jnp.zeros(acc_ref.shape, F32)
    pb_ref[...] = jnp.zeros(pb_ref.shape, BF16)
    ab_ref[...] = jnp.ones(ab_ref.shape, F32)

    def chunk_probs(c, lg_ref, p_ref, a_ref):
        sel, idx = chunk_mask(c)
        for h in range(H_A):
            cs = slice(h * Q_BLOCK, (h + 1) * Q_BLOCK)
            lg = jnp.where(sel, lg_ref[:, cs] + bias_ref[idx, h], -1e30)
            cm = _fold8(lg, jnp.maximum)
            for shift in (4, 2, 1):
                cm = jnp.maximum(cm, pltpu.roll(cm, shift, 0))
            m_old = m_ref[:, cs]
            m_new = jnp.maximum(m_old, cm[0:1])
            alpha = jnp.exp2(m_old - m_new)
            p = jnp.exp2(lg - m_new)
            ls_ref[:, cs] = alpha * ls_ref[:, cs] + _fold8(p, jnp.add)
            m_ref[:, cs] = m_new
            a_ref[:, cs] = alpha
            p_ref[:, cs] = p.astype(BF16)

    def values_mm(c, p_ref, a_ref):
        acc_ref[...] = acc_ref[...] * a_ref[...] + jnp.dot(
            ckvt_ref[0, :, pl.ds(chunk_start(c), KEY_CHUNK)], p_ref[...], preferred_element_type=F32)

    def attn_body(j, carry):
        c = 2 * j
        logits_mm(c + 1, lgb_ref)
        chunk_probs(c, lga_ref, pa_ref, aa_ref)
        values_mm(jnp.maximum(c - 1, 0), pb_ref, ab_ref)
        logits_mm(c + 2, lga_ref)
        chunk_probs(c + 1, lgb_ref, pb_ref, ab_ref)
        values_mm(c, pa_ref, aa_ref)
        return carry

    logits_mm(0, lga_ref)
    lax.fori_loop(0, npair, attn_body, 0)
    values_mm(2 * npair - 1, pb_ref, ab_ref)
    lsum = jnp.sum(ls_ref[...], axis=0, keepdims=True)

    for h in range(H_A):
        cs = slice(h * Q_BLOCK, (h + 1) * Q_BLOCK)
        o_lat = (acc_ref[:, cs] / lsum[:, cs]).astype(BF16)
        out = jnp.dot(wuvt_ref[h], o_lat, preferred_element_type=F32)
        o_ref[0, h * DH_A:(h + 1) * DH_A, :] = out.astype(o_ref.dtype)


def _dsa_attn(qlat, qidx, widx, kidx, ckv, wuv, bias):
    bsz, l, _ = ckv.shape
    nb = l // Q_BLOCK
    topk = min(TOPK_MAX, l // 4)
    nchunks = 2 * ((l + 2 * KEY_CHUNK - 1) // (2 * KEY_CHUNK))
    hq = H_A * Q_BLOCK
    widx_t = widx.reshape(bsz, nb, Q_BLOCK, H_I).transpose(0, 1, 3, 2)
    ckv_t = ckv.transpose(0, 2, 1)
    wuv_t = wuv.transpose(0, 2, 1)
    return pl.pallas_call(
        functools.partial(_dsa_attn_kernel, topk=topk),
        grid=(bsz, nb),
        in_specs=[pl.BlockSpec((1, 1, R_KV, hq), lambda b, i: (b, i, 0, 0)),
                  pl.BlockSpec((1, 1, D_I, H_I * Q_BLOCK), lambda b, i: (b, i, 0, 0)),
                  pl.BlockSpec((1, 1, H_I, Q_BLOCK), lambda b, i: (b, i, 0, 0)),
                  pl.BlockSpec((1, l, D_I), lambda b, i: (b, 0, 0)),
                  pl.BlockSpec((1, l, R_KV), lambda b, i: (b, 0, 0)),
                  pl.BlockSpec((1, R_KV, l), lambda b, i: (b, 0, 0)),
                  pl.BlockSpec((H_A, DH_A, R_KV), lambda b, i: (0, 0, 0)),
                  pl.BlockSpec((4, H_A, KEY_CHUNK, Q_BLOCK), lambda b, i: (0, 0, 0, 0))],
        out_specs=pl.BlockSpec((1, H_A * DH_A, Q_BLOCK), lambda b, i: (b, 0, i)),
        out_shape=jax.ShapeDtypeStruct((bsz, H_A * DH_A, l), BF16),
        scratch_shapes=[pltpu.VMEM((nchunks, KEY_CHUNK, Q_BLOCK), jnp.int32),
                        pltpu.VMEM((nchunks, KEY_CHUNK, Q_BLOCK), jnp.int16),
                        pltpu.VMEM((nchunks, KEY_CHUNK, Q_BLOCK), jnp.int16),
                        pltpu.VMEM((1, hq), F32), pltpu.VMEM((8, hq), F32), pltpu.VMEM((R_KV, hq), F32),
                        pltpu.VMEM((KEY_CHUNK, H_I * Q_BLOCK), F32), pltpu.VMEM((KEY_CHUNK, H_I * Q_BLOCK), F32),
                        pltpu.VMEM((KEY_CHUNK, hq), F32), pltpu.VMEM((KEY_CHUNK, hq), F32),
                        pltpu.VMEM((KEY_CHUNK, hq), BF16), pltpu.VMEM((KEY_CHUNK, hq), BF16),
                        pltpu.VMEM((1, hq), F32), pltpu.VMEM((1, hq), F32)],
        compiler_params=_cparams("parallel", "arbitrary"),
        name="dsa_attn",
    )(qlat, qidx, widx_t, kidx, ckv, ckv_t, wuv_t, bias)


def _seg_sum(x, bd):
    hi = x.astype(BF16)
    lo = (x - hi.astype(F32)).astype(BF16)
    outs = []
    for p in range(x.shape[1] // LANES):
        sl = slice(p * LANES, (p + 1) * LANES)
        outs.append(jnp.dot(hi[:, sl], bd, preferred_element_type=F32)
                    + jnp.dot(lo[:, sl], bd, preferred_element_type=F32))
    return outs[0] if len(outs) == 1 else jnp.concatenate(outs, axis=1)


def _rwkv_pre_kernel(pb_ref, mu_ref, w0_ref, a0_ref, kk_ref, ka_ref, rk_ref, wup_ref, aup_ref, gup_ref,
                     bd_ref, tril_ref, ab_ref, bb_ref, kb_ref, rb_ref, v_ref, p_ref, g_ref, bonus_ref,
                     carry_ref):
    @pl.when(pl.program_id(1) == 0)
    def _():
        carry_ref[...] = jnp.zeros_like(carry_ref)

    x = pb_ref[0]
    tm = x.shape[0]
    rows = lax.broadcasted_iota(jnp.int32, x.shape, 0)
    prev = jnp.where(rows == 0, carry_ref[...], pltpu.roll(x, 1, 0))
    carry_ref[...] = x[tm - 1:tm]
    ps = x + mu_ref[...] * (prev - x)
    r = ps[:, :W_B]
    k = ps[:, W_B:2 * W_B]
    v = ps[:, 2 * W_B:3 * W_B]
    o = 3 * W_B
    wd = ps[:, o:o + LORA_W]
    ad = ps[:, o + LORA_W:o + LORA_W + LORA_A]
    gd = ps[:, o + LORA_W + LORA_A:]
    hp = lax.Precision.HIGHEST
    z = w0_ref[...] + jnp.dot(jnp.tanh(wd), wup_ref[...], precision=hp, preferred_element_type=F32)
    u = -z
    w = -(jnp.maximum(u, 0.0) + jnp.log(1.0 + jnp.exp(-jnp.abs(u)))) - 0.5
    logd = -jnp.exp(w)
    logp = jnp.dot(tril_ref[...], logd, precision=hp, preferred_element_type=F32)
    pinv = jnp.exp(-logp)
    a = _sigmoid(a0_ref[...] + jnp.dot(ad, aup_ref[...], precision=hp, preferred_element_type=F32))
    g_ref[0] = jnp.dot(_sigmoid(gd).astype(BF16), gup_ref[...], preferred_element_type=F32)
    bd = bd_ref[...]
    kk = k * kk_ref[...]
    kk = kk * lax.rsqrt(jnp.maximum(_seg_sum(kk * kk, bd), 1e-24))
    k2 = k * (1.0 + (a - 1.0) * ka_ref[...])
    p = jnp.exp(logp)
    ab_ref[0] = -kk * jnp.exp(logp - logd)
    bb_ref[0] = kk * a * pinv
    kb_ref[0] = k2 * pinv
    rb_ref[0] = r * p
    v_ref[0] = v
    p_ref[0] = p
    bonus_ref[0] = _seg_sum(r * k2 * rk_ref[...], bd) * v


def _rwkv_pre(pb, mu, w0, a0, k_k, k_a, r_k, w_up, a_up, g_up, bd, tm=128):
    bsz, l, _ = pb.shape
    tm = min(tm, l)
    assert tm % RW_CHUNK == 0
    tril = jnp.kron(jnp.eye(tm // RW_CHUNK, dtype=F32), jnp.tril(jnp.ones((RW_CHUNK, RW_CHUNK), F32)))
    full = lambda shp: pl.BlockSpec(shp, lambda b, i: (0,) * len(shp))
    row = pl.BlockSpec((1, tm, W_B), lambda b, i: (b, i, 0))
    vec = full((1, W_B))
    return pl.pallas_call(
        _rwkv_pre_kernel,
        grid=(bsz, l // tm),
        in_specs=[pl.BlockSpec((1, tm, B_PAD), lambda b, i: (b, i, 0)), full((1, B_PAD)),
                  vec, vec, vec, vec, vec,
                  full((LORA_W, W_B)), full((LORA_A, W_B)), full((LORA_G_PAD, W_B)), full((LANES, LANES)),
                  full((tm, tm))],
        out_specs=[row] * 8,
        out_shape=[jax.ShapeDtypeStruct((bsz, l, W_B), F32)] * 8,
        scratch_shapes=[pltpu.VMEM((1, B_PAD), F32)],
        compiler_params=_cparams("parallel", "arbitrary"),
        name="rwkv_pre",
    )(pb, mu, w0.reshape(1, W_B), a0.reshape(1, W_B), k_k.reshape(1, W_B), k_a.reshape(1, W_B),
      r_k.reshape(1, W_B), w_up, a_up, g_up, bd, tril)


def _mm(a, b, dims):
    return lax.dot_general(a.astype(BF16), b.astype(BF16), dims, preferred_element_type=F32)


def _rwkv_chunk_kernel(ab_ref, bb_ref, kb_ref, rb_ref, v_ref, pe_ref, y_ref, s_ref, *, npair):
    @pl.when(pl.program_id(1) == 0)
    def _():
        s_ref[...] = jnp.zeros_like(s_ref)

    c = RW_CHUNK
    ri = lax.broadcasted_iota(jnp.int32, (c, c), 0)
    ci = lax.broadcasted_iota(jnp.int32, (c, c), 1)
    strict, incl = ci < ri, ci <= ri
    eye = jnp.where(ri == ci, 1.0, 0.0)
    head0 = lax.broadcasted_iota(jnp.int32, (1, LANES), 1) < N_B
    bi = lax.broadcasted_iota(jnp.int32, (LANES, LANES), 0) < N_B
    bj = lax.broadcasted_iota(jnp.int32, (LANES, LANES), 1) < N_B
    blockdiag = bi == bj
    nt = (((2,), (2,)), ((0,), (0,)))
    nn = (((2,), (1,)), ((0,), (0,)))
    tn = (((1,), (1,)), ((0,), (0,)))
    tiles = lambda ref: jnp.stack([ref[0, :, p * LANES:(p + 1) * LANES] for p in range(npair)])
    ab, bb, kb, rb, v = tiles(ab_ref), tiles(bb_ref), tiles(kb_ref), tiles(rb_ref), tiles(v_ref)
    pe = jnp.stack([pe_ref[0, 0, :, p * LANES:(p + 1) * LANES] for p in range(npair)])
    s0 = s_ref[...]
    both = lambda x: jnp.concatenate([x, x], axis=0)
    pick = lambda x: jnp.where(head0, x[:npair], x[npair:])
    split = lambda x: jnp.concatenate([jnp.where(head0, x, 0.0), jnp.where(head0, 0.0, x)], axis=0)
    ar = jnp.concatenate([ab, rb], axis=1)
    ars = split(ar)
    gb = _mm(ars, both(bb), nt)
    gk = _mm(ars, both(kb), nt)
    m_ab = jnp.where(strict, gb[:, :c], 0.0)
    n_rb = jnp.where(incl, gb[:, c:], 0.0)
    m_ak = jnp.where(strict, gk[:, :c], 0.0)
    n_rk = jnp.where(incl, gk[:, c:], 0.0)
    tinv = eye + m_ab
    pw = m_ab
    for _ in range(5):
        pw = _mm(pw, pw, nn)
        tinv = tinv + _mm(tinv, pw, nn)
    x0 = _mm(ar, s0, nt)
    v2 = both(v)
    x = x0[:, :c] + pick(_mm(m_ak, v2, nn))
    u = pick(_mm(tinv, both(x), nn))
    y = x0[:, c:] + pick(_mm(n_rb, both(u), nn)) + pick(_mm(n_rk, v2, nn))
    for p in range(npair):
        y_ref[0, :, p * LANES:(p + 1) * LANES] = y[p]
    upd = _mm(jnp.concatenate([u, v], axis=1), jnp.concatenate([bb, kb], axis=1), tn)
    s_ref[...] = (s0 + jnp.where(blockdiag, upd, 0.0)) * pe


def _rwkv_chunks(ab, bb, kb, rb, v, pe):
    bsz, l, wb = ab.shape
    nt = l // RW_CHUNK
    npair = wb // LANES
    row = pl.BlockSpec((1, RW_CHUNK, wb), lambda b, i: (b, i, 0))
    return pl.pallas_call(
        functools.partial(_rwkv_chunk_kernel, npair=npair),
        grid=(bsz, nt),
        in_specs=[row] * 5 + [pl.BlockSpec((1, 1, 1, wb), lambda b, i: (b, i, 0, 0))],
        out_specs=row,
        out_shape=jax.ShapeDtypeStruct((bsz, l, wb), F32),
        scratch_shapes=[pltpu.VMEM((npair, LANES, LANES), F32)],
        compiler_params=_cparams("parallel", "arbitrary"),
        name="rwkv_chunks",
    )(ab, bb, kb, rb, v, pe)


def _rwkv_post_kernel(y_ref, bonus_ref, g_ref, lw_ref, lb_ref, bd_ref, o_ref):
    bd = bd_ref[...]
    y = y_ref[0]
    mean = _seg_sum(y, bd) * (1.0 / N_B)
    c = y - mean
    var = _seg_sum(c * c, bd) * (1.0 / N_B)
    yn = c * lax.rsqrt(var + GN_EPS) * lw_ref[...] + lb_ref[...]
    o_ref[0] = ((yn + bonus_ref[0]) * g_ref[0]).astype(o_ref.dtype)


def _rwkv_post(y, bonus, g, lnx_w, lnx_b, bd, tm=256):
    bsz, l, _ = y.shape
    tm = min(tm, l)
    row = pl.BlockSpec((1, tm, W_B), lambda b, i: (b, i, 0))
    vec = pl.BlockSpec((1, W_B), lambda b, i: (0, 0))
    return pl.pallas_call(
        _rwkv_post_kernel,
        grid=(bsz, l // tm),
        in_specs=[row, row, row, vec, vec, pl.BlockSpec((LANES, LANES), lambda b, i: (0, 0))],
        out_specs=row,
        out_shape=jax.ShapeDtypeStruct((bsz, l, W_B), BF16),
        compiler_params=_cparams("parallel", "parallel"),
        name="rwkv_post",
    )(y, bonus, g, lnx_w.reshape(1, W_B), lnx_b.reshape(1, W_B), bd)


def _rwkv_mixer(pb, mu, w0, w_up, a0, a_up, g_up, k_k, k_a, r_k, lnx_w, lnx_b):
    bsz, l, _ = pb.shape
    nt = l // RW_CHUNK
    eye2 = jnp.kron(jnp.eye(2, dtype=F32), jnp.ones((N_B, N_B), F32)).astype(BF16)
    ab, bb, kb, rb, v, p, g, bonus = _rwkv_pre(pb, mu, w0, a0, k_k, k_a, r_k, w_up, a_up, g_up, eye2)
    pe = p[:, RW_CHUNK - 1::RW_CHUNK].reshape(bsz, nt, 1, W_B)
    y = _rwkv_chunks(ab, bb, kb, rb, v, pe)
    return _rwkv_post(y, bonus, g, lnx_w, lnx_b, eye2)


def _hgrn_kernel(q_ref, f_ref, i_ref, g_ref, lb_ref, on_ref, tril_ref, o_ref, st_ref, *, layer, hb, tt):
    @pl.when(pl.program_id(2) == 0)
    def _():
        st_ref[...] = jnp.zeros_like(st_ref)

    x = lb_ref[...]
    e = jnp.exp(x - jnp.max(x, axis=0, keepdims=True))
    sm = e / jnp.sum(e, axis=0, keepdims=True)
    cs = sm[0:1]
    for i in range(1, layer + 1):
        cs = cs + sm[i:i + 1]
    lb_all = cs - sm[0:1]
    rowi = lax.broadcasted_iota(jnp.int32, (SUB, LANES), 0)
    lanei = lax.broadcasted_iota(jnp.int32, (SUB, LANES), 1)
    prep = []
    for hh in range(hb):
        sl = slice(hh * LANES, (hh + 1) * LANES)
        lb = lb_all[:, sl]
        qraw = q_ref[0][:, sl]
        ff = f_ref[0][:, sl]
        q = qraw * _sigmoid(qraw)
        logsig = jnp.minimum(ff, 0.0) - jnp.log(1.0 + jnp.exp(-jnp.abs(ff)))
        la = jnp.log(lb)
        lbv = jnp.log(1.0 - lb) + logsig
        mx = jnp.maximum(la, lbv)
        logf = mx + jnp.log(jnp.exp(la - mx) + jnp.exp(lbv - mx))
        k = (1.0 - lb) * _sigmoid(-ff)
        bcum = jnp.dot(tril_ref[...], logf * LOG2E, precision=lax.Precision.HIGHEST, preferred_element_type=F32)
        prep.append((q, k, i_ref[0][:, sl], bcum))
    sts = [st_ref[hh] for hh in range(hb)]
    outs = [[] for _ in range(hb)]
    for n in range(tt // SUB):
        rs = slice(n * SUB, (n + 1) * SUB)
        for hh in range(hb):
            q, k, v, bcum = prep[hh]
            st = sts[hh]
            bq, qn, kn, vn = bcum[rs], q[rs], k[rs], v[rs]
            bend = bq[SUB - 1:SUB]
            qd = (qn * jnp.exp2(bq)).astype(BF16)
            o = lax.dot_general(qd, st.astype(BF16), NT_DIMS, preferred_element_type=F32)
            a = jnp.zeros((SUB, LANES), F32)
            for s in range(SUB):
                wgt = kn[s:s + 1] * jnp.exp2(jnp.minimum(bq - bq[s:s + 1], 0.0))
                a = jnp.where(lanei == s, jnp.sum(qn * wgt, axis=1, keepdims=True), a)
            a = jnp.where(lanei <= rowi, a, 0.0)[:, :SUB]
            o = o + jnp.dot(a.astype(BF16), vn.astype(BF16), preferred_element_type=F32)
            kd = (kn * jnp.exp2(bend - bq)).astype(BF16)
            sts[hh] = st * jnp.exp2(bend) + lax.dot_general(vn.astype(BF16), kd, TN_DIMS,
                                                           preferred_element_type=F32)
            outs[hh].append(o)
    for hh in range(hb):
        sl = slice(hh * LANES, (hh + 1) * LANES)
        st_ref[hh] = sts[hh]
        gg = g_ref[0][:, sl]
        o = jnp.concatenate(outs[hh], axis=0)
        o = _rms(o) * on_ref[:, sl] * (gg * _sigmoid(gg))
        o_ref[0, :, sl] = o.astype(o_ref.dtype)


def _hgrn_mixer(pc, hgrn_lb, onorm, layer, hb=12, tt=128):
    bsz, l, _ = pc.shape
    depth = hgrn_lb.shape[0]
    tt = min(tt, l)
    ng = H_C // hb
    wblk = hb * LANES
    tril = jnp.kron(jnp.eye(tt // SUB, dtype=F32), jnp.tril(jnp.ones((SUB, SUB), F32)))
    col = lambda j: pl.BlockSpec((1, tt, wblk), lambda b, h, i, j=j: (b, i, j * ng + h))
    return pl.pallas_call(
        functools.partial(_hgrn_kernel, layer=layer, hb=hb, tt=tt),
        grid=(bsz, ng, l // tt),
        in_specs=[col(0), col(1), col(2), col(3),
                  pl.BlockSpec((depth, wblk), lambda b, h, i: (0, h)),
                  pl.BlockSpec((1, wblk), lambda b, h, i: (0, h)),
                  pl.BlockSpec((tt, tt), lambda b, h, i: (0, 0))],
        out_specs=pl.BlockSpec((1, tt, wblk), lambda b, h, i: (b, i, h)),
        out_shape=jax.ShapeDtypeStruct((bsz, l, W_C), BF16),
        scratch_shapes=[pltpu.VMEM((hb, DV_C, DK_C), F32)],
        compiler_params=_cparams("parallel", "parallel", "arbitrary"),
        name="hgrn2",
    )(pc, pc, pc, pc, hgrn_lb, onorm.reshape(1, W_C), tril)


def _pack_w_in(w):
    d = w.shape[0]
    a_cols = R_Q + R_KV + D_I + H_I
    b_cols = 3 * W_B + LORA_W + LORA_A + LORA_G
    wa, wb, wc = w[:, :a_cols], w[:, a_cols:a_cols + b_cols], w[:, a_cols + b_cols:]
    z = lambda n: jnp.zeros((d, n), w.dtype)
    o = R_Q + R_KV
    wa = jnp.concatenate([wa[:, :o + D_I], z(LANES - D_I), wa[:, o + D_I:], z(LANES - H_I)], axis=1)
    wb = jnp.concatenate([wb, z(LORA_G_PAD - LORA_G)], axis=1)
    return wa.astype(BF16), wb.astype(BF16), wc.astype(BF16)


def kernel(x, c, rel_bias, hgrn_lb, ada_w, ada_b, norm_g, w_in, w_out, mla_q_norm, mla_kv_norm, w_uq, w_uk, w_uv, w_qidx, rwkv_mu, rwkv_w0, rwkv_w_up, rwkv_a0, rwkv_a_up, rwkv_g_up, rwkv_k_k, rwkv_k_a, rwkv_r_k, rwkv_lnx_w, rwkv_lnx_b, hgrn_onorm, w_ff1, w_ff2):
    bsz, l, d = x.shape
    depth = ada_w.shape[0]
    m = bsz * l
    mod = _ada_mod(c, ada_w, ada_b)
    mods = [[mod[i, :, None, j * d:(j + 1) * d] for j in range(6)] for i in range(depth)]
    bias = _bias_tiles(rel_bias)
    h = _norm_mod(x, norm_g[0, 0], mods[0][1], mods[0][0])
    for i in range(depth):
        sh_m, sc_m, g_m, sh_f, sc_f, g_f = mods[i]
        wa, wb, wc = _pack_w_in(w_in[i])
        h2d = h.reshape(m, d)
        pa = _matmul(h2d, wa, F32, "in_proj_a").reshape(bsz, l, -1)
        pb = _matmul(h2d, wb, F32, "in_proj_b").reshape(bsz, l, -1)
        pc = _matmul(h2d, wc, F32, "in_proj_c").reshape(bsz, l, -1)
        wuq = w_uq[i].reshape(R_Q, H_A * DH_A).astype(BF16)
        wqi = jnp.transpose(w_qidx[i], (1, 2, 0)).astype(BF16)
        qlat, qidx, widx, kidx, ckv = _dsa_prep(pa, mla_q_norm[i], mla_kv_norm[i], wuq,
                                                w_uk[i].astype(BF16), wqi)
        y_a = _dsa_attn(qlat, qidx, widx, kidx, ckv, w_uv[i].astype(BF16), bias).transpose(0, 2, 1)
        mu = jnp.concatenate([rwkv_mu[i], jnp.zeros((LORA_G_PAD - LORA_G,), F32)]).reshape(1, B_PAD)
        gup = jnp.concatenate([rwkv_g_up[i], jnp.zeros((LORA_G_PAD - LORA_G, W_B), F32)]).astype(BF16)
        y_b = _rwkv_mixer(pb, mu, rwkv_w0[i], rwkv_w_up[i], rwkv_a0[i], rwkv_a_up[i], gup,
                          rwkv_k_k[i], rwkv_k_a[i], rwkv_r_k[i], rwkv_lnx_w[i], rwkv_lnx_b[i])
        y_c = _hgrn_mixer(pc, hgrn_lb, hgrn_onorm[i], i)
        ycat = jnp.concatenate([y_a, y_b, y_c], axis=-1).reshape(m, -1)
        y = _matmul(ycat, w_out[i].astype(BF16), F32, "out_proj").reshape(bsz, l, d)
        x, h = _resid_norm(x, y, g_m, norm_g[i, 1], (norm_g[i, 2], sc_f, sh_f))
        u = _matmul(h.reshape(m, d), w_ff1[i].astype(BF16), BF16, "ffn_up", relu2=True)
        y = _matmul(u, w_ff2[i].astype(BF16), F32, "ffn_down").reshape(bsz, l, d)
        nxt = None if i == depth - 1 else (norm_g[i + 1, 0], mods[i + 1][1], mods[i + 1][0])
        x, h = _resid_norm(x, y, g_f, norm_g[i, 3], nxt)
    return x
```

```python
import functools
import math

import jax
import jax.numpy as jnp
from jax import lax
from jax.experimental import pallas as pl
from jax.experimental.pallas import tpu as pltpu

H_A, DH_A, R_Q, R_KV, H_I, D_I = 8, 128, 768, 256, 16, 64
TOPK_MAX, Q_BLOCK, N_BUCKETS, MAX_DIST = 256, 128, 32, 128
H_B, N_B, LORA_W, LORA_A, LORA_G = 24, 64, 128, 128, 480
W_B = H_B * N_B
H_C, DK_C, DV_C = 12, 128, 128
W_C = H_C * DK_C
EPS = 1e-6
GN_EPS = 64e-5

LANES = 128
KEY_CHUNK = 256
SUB = 8
RW_CHUNK = 64
SUBLANES = 8
I16_MIN, I16_MAX = -2 ** 15, 2 ** 15 - 1
LORA_G_PAD = 512
A_PAD = 1280
B_PAD = 3 * W_B + LORA_W + LORA_A + LORA_G_PAD
VMEM_LIMIT = 56 * 1024 * 1024
MM_TILES = {"in_proj_a": (512, A_PAD, 4096), "in_proj_b": (1024, 768, 4096), "in_proj_c": (1024, 1024, 4096),
            "out_proj": (1024, 1024, 4096), "ffn_up": (1024, 1024, 4096), "ffn_down": (1024, 1024, 4096)}

F32 = jnp.float32
BF16 = jnp.bfloat16
NT_DIMS = (((1,), (1,)), ((), ()))
TN_DIMS = (((0,), (0,)), ((), ()))
INT_MIN, INT_MAX = -2 ** 31, 2 ** 31 - 1
LOG2E = math.log2(math.e)


def _cparams(*sem):
    return pltpu.CompilerParams(dimension_semantics=sem, vmem_limit_bytes=VMEM_LIMIT)


def _sigmoid(x):
    return 1.0 / (1.0 + jnp.exp(-x))


def _rms(x, eps=EPS):
    return x * lax.rsqrt(jnp.mean(x * x, axis=-1, keepdims=True) + eps)


def _mm_kernel(a_ref, b_ref, o_ref, acc_ref, *, nk, relu2):
    k = pl.program_id(2)

    @pl.when(k == 0)
    def _():
        acc_ref[...] = jnp.zeros_like(acc_ref)

    acc_ref[...] += jnp.dot(a_ref[...], b_ref[...], preferred_element_type=F32)

    @pl.when(k == nk - 1)
    def _():
        r = acc_ref[...]
        if relu2:
            r = jnp.square(jnp.maximum(r, 0.0))
        o_ref[...] = r.astype(o_ref.dtype)


def _mm_full_k_kernel(a_ref, b_ref, o_ref, *, relu2):
    r = jnp.dot(a_ref[...], b_ref[...], preferred_element_type=F32)
    if relu2:
        r = jnp.square(jnp.maximum(r, 0.0))
    o_ref[...] = r.astype(o_ref.dtype)


def _matmul(a, b, out_dtype, name, relu2=False):
    m, kdim = a.shape
    _, n = b.shape
    tm, tn, tk = (min(t, full) for t, full in zip(MM_TILES[name], (m, n, kdim)))
    assert m % tm == 0 and n % tn == 0 and kdim % tk == 0, (a.shape, b.shape, tm, tn, tk)
    nk = kdim // tk
    if nk == 1:
        return pl.pallas_call(
            functools.partial(_mm_full_k_kernel, relu2=relu2),
            grid=(m // tm, n // tn),
            in_specs=[pl.BlockSpec((tm, kdim), lambda i, j: (i, 0)),
                      pl.BlockSpec((kdim, tn), lambda i, j: (0, j))],
            out_specs=pl.BlockSpec((tm, tn), lambda i, j: (i, j)),
            out_shape=jax.ShapeDtypeStruct((m, n), out_dtype),
            compiler_params=_cparams("parallel", "parallel"),
            name=name,
        )(a, b)
    return pl.pallas_call(
        functools.partial(_mm_kernel, nk=nk, relu2=relu2),
        grid=(m // tm, n // tn, nk),
        in_specs=[pl.BlockSpec((tm, tk), lambda i, j, k: (i, k)),
                  pl.BlockSpec((tk, tn), lambda i, j, k: (k, j))],
        out_specs=pl.BlockSpec((tm, tn), lambda i, j, k: (i, j)),
        out_shape=jax.ShapeDtypeStruct((m, n), out_dtype),
        scratch_shapes=[pltpu.VMEM((tm, tn), F32)],
        compiler_params=_cparams("parallel", "parallel", "arbitrary"),
        name=name,
    )(a, b)


def _ada_kernel(c_ref, w_ref, b_ref, o_ref):
    k = pl.program_id(2)

    @pl.when(k == 0)
    def _():
        o_ref[0] = jnp.broadcast_to(b_ref[0], o_ref.shape[1:])

    c = c_ref[...]
    ca = (c * _sigmoid(c)).astype(BF16)
    o_ref[0] += jnp.dot(ca, w_ref[0].astype(BF16), preferred_element_type=F32)


def _ada_mod(c, ada_w, ada_b):
    nl, d, n = ada_w.shape
    bsz = c.shape[0]
    rows = SUBLANES
    cp = jnp.zeros((rows, d), F32).at[:bsz].set(c)
    tn, tk = min(2048, n), min(1024, d)
    out = pl.pallas_call(
        _ada_kernel,
        grid=(nl, n // tn, d // tk),
        in_specs=[pl.BlockSpec((rows, tk), lambda l, j, k: (0, k)),
                  pl.BlockSpec((1, tk, tn), lambda l, j, k: (l, k, j)),
                  pl.BlockSpec((1, 1, tn), lambda l, j, k: (l, 0, j))],
        out_specs=pl.BlockSpec((1, rows, tn), lambda l, j, k: (l, 0, j)),
        out_shape=jax.ShapeDtypeStruct((nl, rows, n), F32),
        compiler_params=_cparams("parallel", "parallel", "arbitrary"),
        name="ada_mod",
    )(cp, ada_w, ada_b.reshape(nl, 1, n))
    return out[:, :bsz]


def _norm_mod_kernel(x_ref, g_ref, sc_ref, sh_ref, h_ref):
    y = _rms(x_ref[0]) * g_ref[...]
    h_ref[0] = (y * (1.0 + sc_ref[0]) + sh_ref[0]).astype(h_ref.dtype)


def _norm_mod(x, g, sc, sh, tm=256):
    bsz, l, d = x.shape
    tm = min(tm, l)
    row = pl.BlockSpec((1, tm, d), lambda b, i: (b, i, 0))
    vec = pl.BlockSpec((1, d), lambda b, i: (0, 0))
    bvec = pl.BlockSpec((1, 1, d), lambda b, i: (b, 0, 0))
    return pl.pallas_call(
        _norm_mod_kernel,
        grid=(bsz, l // tm),
        in_specs=[row, vec, bvec, bvec],
        out_specs=row,
        out_shape=jax.ShapeDtypeStruct((bsz, l, d), BF16),
        compiler_params=_cparams("parallel", "parallel"),
        name="norm_mod",
    )(x, g.reshape(1, d), sc, sh)


def _resid_kernel(x_ref, y_ref, gate_ref, g1_ref, g2_ref, sc_ref, sh_ref, xo_ref, h_ref):
    xn = x_ref[0] + gate_ref[0] * (_rms(y_ref[0]) * g1_ref[...])
    xo_ref[0] = xn
    hn = _rms(xn) * g2_ref[...]
    h_ref[0] = (hn * (1.0 + sc_ref[0]) + sh_ref[0]).astype(h_ref.dtype)


def _resid_last_kernel(x_ref, y_ref, gate_ref, g1_ref, xo_ref):
    xo_ref[0] = x_ref[0] + gate_ref[0] * (_rms(y_ref[0]) * g1_ref[...])


def _resid_norm(x, y, gate, g1, nxt=None, tm=128):
    bsz, l, d = x.shape
    tm = min(tm, l)
    row = pl.BlockSpec((1, tm, d), lambda b, i: (b, i, 0))
    vec = pl.BlockSpec((1, d), lambda b, i: (0, 0))
    bvec = pl.BlockSpec((1, 1, d), lambda b, i: (b, 0, 0))
    if nxt is None:
        return pl.pallas_call(
            _resid_last_kernel,
            grid=(bsz, l // tm),
            in_specs=[row, row, bvec, vec],
            out_specs=row,
            out_shape=jax.ShapeDtypeStruct((bsz, l, d), F32),
            compiler_params=_cparams("parallel", "parallel"),
            name="resid_last",
        )(x, y, gate, g1.reshape(1, d)), None
    g2, sc, sh = nxt
    return pl.pallas_call(
        _resid_kernel,
        grid=(bsz, l // tm),
        in_specs=[row, row, bvec, vec, vec, bvec, bvec],
        out_specs=[row, row],
        out_shape=[jax.ShapeDtypeStruct((bsz, l, d), F32), jax.ShapeDtypeStruct((bsz, l, d), BF16)],
        compiler_params=_cparams("parallel", "parallel"),
        name="resid_norm",
    )(x, y, gate, g1.reshape(1, d), g2.reshape(1, d), sc, sh)


def _bias_kernel(rb_ref, o_ref):
    j = lax.broadcasted_iota(jnp.int32, (KEY_CHUNK, Q_BLOCK), 0)
    i = lax.broadcasted_iota(jnp.int32, (KEY_CHUNK, Q_BLOCK), 1)
    max_exact = N_BUCKETS // 2
    for v, off in enumerate((Q_BLOCK, 0, 2 * Q_BLOCK, None)):
        if off is None:
            dist = jnp.full((KEY_CHUNK, Q_BLOCK), 2 * MAX_DIST, jnp.int32)
        else:
            dist = jnp.maximum(i + off - j, 0)
        nf = jnp.maximum(dist, 1).astype(F32)
        large = max_exact + (jnp.log(nf / max_exact) / math.log(MAX_DIST / max_exact)
                             * (N_BUCKETS - max_exact)).astype(jnp.int32)
        large = jnp.minimum(large, N_BUCKETS - 1)
        bucket = jnp.where(dist < max_exact, dist, large)
        for h in range(H_A):
            def body(b, acc, h=h, bucket=bucket):
                return jnp.where(bucket == b, rb_ref[b * H_A + h] * LOG2E, acc)
            o_ref[v, h] = lax.fori_loop(0, N_BUCKETS, body, jnp.zeros((KEY_CHUNK, Q_BLOCK), F32))


def _bias_tiles(rel_bias):
    return pl.pallas_call(
        _bias_kernel,
        in_specs=[pl.BlockSpec(memory_space=pltpu.SMEM)],
        out_specs=pl.BlockSpec(memory_space=pltpu.VMEM),
        out_shape=jax.ShapeDtypeStruct((4, H_A, KEY_CHUNK, Q_BLOCK), F32),
        name="dsa_bias_tiles",
    )(rel_bias.reshape(-1))


def _dsa_prep_kernel(pa_ref, gq_ref, gkv_ref, wuq_ref, wuk_ref, wqi_ref,
                     qlat_ref, qidx_ref, widx_ref, kidx_ref, ckv_ref):
    pa = pa_ref[0]
    cq = pa[:, :R_Q]
    ckv = pa[:, R_Q:R_Q + R_KV]
    kid = pa[:, R_Q + R_KV:R_Q + R_KV + D_I]
    wid = pa[:, R_Q + R_KV + LANES:R_Q + R_KV + LANES + H_I]
    cqn = (_rms(cq) * gq_ref[...]).astype(BF16)
    ckv_ref[0] = (_rms(ckv) * gkv_ref[...]).astype(BF16)
    kidx_ref[0] = kid.astype(BF16)
    widx_ref[0] = wid * (H_I ** -0.5 * D_I ** -0.5)
    q = jnp.dot(cqn, wuq_ref[...], preferred_element_type=F32)
    for h in range(H_A):
        qh = q[:, h * DH_A:(h + 1) * DH_A].astype(BF16)
        ql = lax.dot_general(wuk_ref[h], qh, NT_DIMS, preferred_element_type=F32)
        qlat_ref[0, 0, :, h * Q_BLOCK:(h + 1) * Q_BLOCK] = (ql * (DH_A ** -0.5 * LOG2E)).astype(BF16)
    for h in range(H_I):
        qi = lax.dot_general(wqi_ref[h], cqn, NT_DIMS, preferred_element_type=F32)
        qidx_ref[0, 0, :, h * Q_BLOCK:(h + 1) * Q_BLOCK] = qi.astype(BF16)


def _dsa_prep(pa, gq, gkv, wuq, wuk, wqi):
    bsz, l, _ = pa.shape
    nb = l // Q_BLOCK
    full = lambda shp: pl.BlockSpec(shp, lambda b, i: (0,) * len(shp))
    return pl.pallas_call(
        _dsa_prep_kernel,
        grid=(bsz, nb),
        in_specs=[pl.BlockSpec((1, Q_BLOCK, A_PAD), lambda b, i: (b, i, 0)),
                  full((1, R_Q)), full((1, R_KV)), full((R_Q, H_A * DH_A)),
                  full((H_A, R_KV, DH_A)), full((H_I, D_I, R_Q))],
        out_specs=[pl.BlockSpec((1, 1, R_KV, H_A * Q_BLOCK), lambda b, i: (b, i, 0, 0)),
                   pl.BlockSpec((1, 1, D_I, H_I * Q_BLOCK), lambda b, i: (b, i, 0, 0)),
                   pl.BlockSpec((1, Q_BLOCK, H_I), lambda b, i: (b, i, 0)),
                   pl.BlockSpec((1, Q_BLOCK, D_I), lambda b, i: (b, i, 0)),
                   pl.BlockSpec((1, Q_BLOCK, R_KV), lambda b, i: (b, i, 0))],
        out_shape=[jax.ShapeDtypeStruct((bsz, nb, R_KV, H_A * Q_BLOCK), BF16),
                   jax.ShapeDtypeStruct((bsz, nb, D_I, H_I * Q_BLOCK), BF16),
                   jax.ShapeDtypeStruct((bsz, l, H_I), F32),
                   jax.ShapeDtypeStruct((bsz, l, D_I), BF16),
                   jax.ShapeDtypeStruct((bsz, l, R_KV), BF16)],
        compiler_params=_cparams("parallel", "parallel"),
        name="dsa_prep",
    )(pa, gq.reshape(1, R_Q), gkv.reshape(1, R_KV), wuq, wuk, wqi)


def _fold8(x, op, ways=1):
    n = x.shape[0] // SUBLANES
    ways = min(ways, n)
    part = lambda g: x[g * SUBLANES:(g + 1) * SUBLANES]
    acc = [part(g) for g in range(ways)]
    for g in range(ways, n):
        acc[g % ways] = op(acc[g % ways], part(g))
    while len(acc) > 1:
        acc = [op(acc[i], acc[i + 1]) for i in range(0, len(acc), 2)]
    return acc[0]


def _dsa_attn_kernel(qlat_ref, qidx_ref, widx_ref, kidx_ref, ckv_ref, ckvt_ref, wuvt_ref, bias_ref, o_ref,
                     key_ref, hi_ref, lo_ref, m_ref, ls_ref, acc_ref, sa_ref, sb_ref, lga_ref, lgb_ref,
                     pa_ref, pb_ref, aa_ref, ab_ref, *, topk):
    qi = pl.program_id(1)
    nch = (qi * Q_BLOCK) // KEY_CHUNK + 1
    krow = lax.broadcasted_iota(jnp.int32, (KEY_CHUNK, Q_BLOCK), 0)
    qlane = lax.broadcasted_iota(jnp.int32, (KEY_CHUNK, Q_BLOCK), 1)
    t_abs = qi * Q_BLOCK + qlane
    w = widx_ref[0, 0]

    npair = (nch + 1) // 2
    last_chunk = kidx_ref.shape[1] // KEY_CHUNK - 1

    def chunk_start(c):
        return pl.multiple_of(jnp.minimum(c, last_chunk) * KEY_CHUNK, KEY_CHUNK)

    def score_mm(c, s_ref):
        kc = kidx_ref[0, pl.ds(chunk_start(c), KEY_CHUNK), :]
        s_ref[...] = jnp.dot(kc, qidx_ref[0, 0], preferred_element_type=F32)

    def score_keys(c, s_ref):
        acc = jnp.zeros((KEY_CHUNK, Q_BLOCK), F32)
        for h in range(H_I):
            acc = acc + w[h:h + 1] * jnp.maximum(s_ref[:, h * Q_BLOCK:(h + 1) * Q_BLOCK], 0.0)
        acc = jnp.where(c * KEY_CHUNK + krow <= t_abs, acc, -jnp.inf)
        bits = lax.bitcast_convert_type(acc, jnp.int32)
        key = bits ^ ((bits >> 31) & 0x7FFFFFFF)
        key_ref[c] = key
        hi_ref[c] = (key >> 16).astype(jnp.int16)
        lo_ref[c] = ((key & 0xFFFF) + I16_MIN).astype(jnp.int16)

    def score_body(j, carry):
        c = 2 * j
        score_mm(c + 1, sb_ref)
        score_keys(c, sa_ref)
        score_mm(c + 2, sa_ref)
        score_keys(c + 1, sb_ref)
        return carry

    score_mm(0, sa_ref)
    lax.fori_loop(0, npair, score_body, 0)

    def count_ge16(ref, cand):
        c16 = cand.astype(jnp.int16)
        rows = 2 * SUBLANES

        def fold(x):
            r = x[0:rows]
            for g in range(1, x.shape[0] // rows):
                r = r + x[g * rows:(g + 1) * rows]
            return r

        def hits(c):
            return fold(jnp.where(ref[c] >= c16, jnp.int16(1), jnp.int16(0)))

        def quad(c, cnt):
            return cnt + ((hits(4 * c) + hits(4 * c + 1)) + (hits(4 * c + 2) + hits(4 * c + 3)))

        def tail(_, cnt):
            return cnt + (hits(2 * npair - 2) + hits(2 * npair - 1))

        cnt = lax.fori_loop(0, npair // 2, quad, jnp.zeros((rows, Q_BLOCK), jnp.int16))
        cnt = lax.fori_loop(0, npair % 2, tail, cnt)
        return jnp.sum(cnt.astype(F32), axis=0, keepdims=True)

    kf = float(topk)
    total = (npair * (2 * KEY_CHUNK)).astype(F32)
    few = (qi * Q_BLOCK + lax.broadcasted_iota(jnp.int32, (1, Q_BLOCK), 1)) < topk

    def unsettled(cnt_thr):
        return jnp.max(jnp.where(few | (cnt_thr == kf), 0.0, 1.0)) > 0.5

    def search16(ref, early_exit, cnt_min):
        cnt0 = count_ge16(ref, jnp.zeros((1, Q_BLOCK), jnp.int32))
        thr = jnp.where(cnt0 >= kf, 0, I16_MIN).astype(jnp.int32)
        cnt_thr = jnp.where(cnt0 >= kf, cnt0, cnt_min)

        def group_cond(st):
            return (st[0] < 4) & st[3]

        def group_body(st):
            g, thr, cnt_thr, _ = st
            for k in range(4):
                i = 4 * g + k
                bit = jnp.where(i < 15, jnp.left_shift(jnp.int32(1), jnp.maximum(14 - i, 0)), 0)
                cand = thr + bit
                cnt = count_ge16(ref, cand)
                take = (cnt >= kf) & (bit != 0)
                thr = jnp.where(take, cand, thr)
                cnt_thr = jnp.where(take, cnt, cnt_thr)
            return g + 1, thr, cnt_thr, (unsettled(cnt_thr) if early_exit else jnp.bool_(True))

        go = unsettled(cnt_thr) if early_exit else jnp.bool_(True)
        _, thr, cnt_thr, _ = lax.while_loop(group_cond, group_body, (jnp.int32(0), thr, cnt_thr, go))
        return thr, cnt_thr

    thr_hi, cnt_hi = search16(hi_ref, False, total)
    thi16 = thr_hi.astype(jnp.int16)

    def relabel(c, carry):
        for cc in (2 * c, 2 * c + 1):
            hi = hi_ref[cc]
            lo_ref[cc] = jnp.where(hi > thi16, jnp.int16(I16_MAX),
                                   jnp.where(hi == thi16, lo_ref[cc], jnp.int16(I16_MIN)))
        return carry

    lax.fori_loop(0, npair, relabel, 0)
    thr_lo, cnt_ge = search16(lo_ref, True, cnt_hi)
    thr = thr_hi * 2 ** 16 + (thr_lo - I16_MIN)

    tied = jnp.logical_not(few) & (cnt_ge > kf)
    idx_bits = max(1, (kidx_ref.shape[1] - 1).bit_length())

    def count32(pred):
        def body(c, cnt):
            return (cnt + _fold8(jnp.where(pred(2 * c), 1.0, 0.0), jnp.add, 4)
                    + _fold8(jnp.where(pred(2 * c + 1), 1.0, 0.0), jnp.add, 4))
        cnt = lax.fori_loop(0, npair, body, jnp.zeros((SUBLANES, Q_BLOCK), F32))
        return jnp.sum(cnt, axis=0, keepdims=True)

    def break_ties(_, carry):
        need = kf - count32(lambda c: key_ref[c] > thr)

        def tie_bit(i, cut):
            cand = cut + jnp.left_shift(jnp.int32(1), idx_bits - 1 - i)
            below = count32(lambda c: (key_ref[c] == thr) & (c * KEY_CHUNK + krow < cand))
            return jnp.where(below < need, cand, cut)

        cut = lax.fori_loop(0, idx_bits, tie_bit, jnp.zeros((1, Q_BLOCK), jnp.int32))
        cut = jnp.where(tied, cut, INT_MAX)

        def demote(c, carry):
            for cc in (2 * c, 2 * c + 1):
                key = key_ref[cc]
                key_ref[cc] = jnp.where((key == thr) & (cc * KEY_CHUNK + krow > cut), thr - 1, key)
            return carry

        return lax.fori_loop(0, npair, demote, carry)

    any_tied = jnp.max(jnp.where(tied, 1.0, 0.0)) > 0.5
    lax.fori_loop(0, jnp.where(any_tied, 1, 0), break_ties, 0)

    odd = (qi % 2) == 1

    def logits_mm(c, lg_ref):
        kv = ckv_ref[0, pl.ds(chunk_start(c), KEY_CHUNK), :]
        lg_ref[...] = jnp.dot(kv, qlat_ref[0, 0], preferred_element_type=F32)

    def chunk_mask(c):
        sel = (key_ref[c] >= thr) & (c * KEY_CHUNK + krow <= t_abs)
        idx = jnp.where(c == nch - 1, jnp.where(odd, 0, 1),
                        jnp.where((c == nch - 2) & jnp.logical_not(odd), 2, 3))
        return sel, idx

    m_ref[...] = jnp.full(m_ref.shape, -1e29, F32)
    ls_ref[...] = jnp.zeros(ls_ref.shape, F32)
    acc_ref[...] = jnp.zeros(acc_ref.shape, F32)
    pb_ref[...] = jnp.zeros(pb_ref.shape, BF16)
    ab_ref[...] = jnp.ones(ab_ref.shape, F32)

    def chunk_probs(c, lg_ref, p_ref, a_ref):
        sel, idx = chunk_mask(c)
        for h in range(H_A):
            cs = slice(h * Q_BLOCK, (h + 1) * Q_BLOCK)
            lg = jnp.where(sel, lg_ref[:, cs] + bias_ref[idx, h], -1e30)
            cm = _fold8(lg, jnp.maximum)
            for shift in (4, 2, 1):
                cm = jnp.maximum(cm, pltpu.roll(cm, shift, 0))
            m_old = m_ref[:, cs]
            m_new = jnp.maximum(m_old, cm[0:1])
            alpha = jnp.exp2(m_old - m_new)
            p = jnp.exp2(lg - m_new)
            ls_ref[:, cs] = alpha * ls_ref[:, cs] + _fold8(p, jnp.add)
            m_ref[:, cs] = m_new
            a_ref[:, cs] = alpha
            p_ref[:, cs] = p.astype(BF16)

    def values_mm(c, p_ref, a_ref):
        acc_ref[...] = acc_ref[...] * a_ref[...] + jnp.dot(
            ckvt_ref[0, :, pl.ds(chunk_start(c), KEY_CHUNK)], p_ref[...], preferred_element_type=F32)

    def attn_body(j, carry):
        c = 2 * j
        logits_mm(c + 1, lgb_ref)
        chunk_probs(c, lga_ref, pa_ref, aa_ref)
        values_mm(jnp.maximum(c - 1, 0), pb_ref, ab_ref)
        logits_mm(c + 2, lga_ref)
        chunk_probs(c + 1, lgb_ref, pb_ref, ab_ref)
        values_mm(c, pa_ref, aa_ref)
        return carry

    logits_mm(0, lga_ref)
    lax.fori_loop(0, npair, attn_body, 0)
    values_mm(2 * npair - 1, pb_ref, ab_ref)
    lsum = jnp.sum(ls_ref[...], axis=0, keepdims=True)

    for h in range(H_A):
        cs = slice(h * Q_BLOCK, (h + 1) * Q_BLOCK)
        o_lat = (acc_ref[:, cs] / lsum[:, cs]).astype(BF16)
        out = jnp.dot(wuvt_ref[h], o_lat, preferred_element_type=F32)
        o_ref[0, h * DH_A:(h + 1) * DH_A, :] = out.astype(o_ref.dtype)


def _dsa_attn(qlat, qidx, widx, kidx, ckv, wuv, bias):
    bsz, l, _ = ckv.shape
    nb = l // Q_BLOCK
    topk = min(TOPK_MAX, l // 4)
    nchunks = 2 * ((l + 2 * KEY_CHUNK - 1) // (2 * KEY_CHUNK))
    hq = H_A * Q_BLOCK
    widx_t = widx.reshape(bsz, nb, Q_BLOCK, H_I).transpose(0, 1, 3, 2)
    ckv_t = ckv.transpose(0, 2, 1)
    wuv_t = wuv.transpose(0, 2, 1)
    return pl.pallas_call(
        functools.partial(_dsa_attn_kernel, topk=topk),
        grid=(bsz, nb),
        in_specs=[pl.BlockSpec((1, 1, R_KV, hq), lambda b, i: (b, i, 0, 0)),
                  pl.BlockSpec((1, 1, D_I, H_I * Q_BLOCK), lambda b, i: (b, i, 0, 0)),
                  pl.BlockSpec((1, 1, H_I, Q_BLOCK), lambda b, i: (b, i, 0, 0)),
                  pl.BlockSpec((1, l, D_I), lambda b, i: (b, 0, 0)),
                  pl.BlockSpec((1, l, R_KV), lambda b, i: (b, 0, 0)),
                  pl.BlockSpec((1, R_KV, l), lambda b, i: (b, 0, 0)),
                  pl.BlockSpec((H_A, DH_A, R_KV), lambda b, i: (0, 0, 0)),
                  pl.BlockSpec((4, H_A, KEY_CHUNK, Q_BLOCK), lambda b, i: (0, 0, 0, 0))],
        out_specs=pl.BlockSpec((1, H_A * DH_A, Q_BLOCK), lambda b, i: (b, 0, i)),
        out_shape=jax.ShapeDtypeStruct((bsz, H_A * DH_A, l), BF16),
        scratch_shapes=[pltpu.VMEM((nchunks, KEY_CHUNK, Q_BLOCK), jnp.int32),
                        pltpu.VMEM((nchunks, KEY_CHUNK, Q_BLOCK), jnp.int16),
                        pltpu.VMEM((nchunks, KEY_CHUNK, Q_BLOCK), jnp.int16),
                        pltpu.VMEM((1, hq), F32), pltpu.VMEM((8, hq), F32), pltpu.VMEM((R_KV, hq), F32),
                        pltpu.VMEM((KEY_CHUNK, H_I * Q_BLOCK), F32), pltpu.VMEM((KEY_CHUNK, H_I * Q_BLOCK), F32),
                        pltpu.VMEM((KEY_CHUNK, hq), F32), pltpu.VMEM((KEY_CHUNK, hq), F32),
                        pltpu.VMEM((KEY_CHUNK, hq), BF16), pltpu.VMEM((KEY_CHUNK, hq), BF16),
                        pltpu.VMEM((1, hq), F32), pltpu.VMEM((1, hq), F32)],
        compiler_params=_cparams("parallel", "arbitrary"),
        name="dsa_attn",
    )(qlat, qidx, widx_t, kidx, ckv, ckv_t, wuv_t, bias)


def _seg_sum(x, bd):
    hi = x.astype(BF16)
    lo = (x - hi.astype(F32)).astype(BF16)
    outs = []
    for p in range(x.shape[1] // LANES):
        sl = slice(p * LANES, (p + 1) * LANES)
        outs.append(jnp.dot(hi[:, sl], bd, preferred_element_type=F32)
                    + jnp.dot(lo[:, sl], bd, preferred_element_type=F32))
    return outs[0] if len(outs) == 1 else jnp.concatenate(outs, axis=1)


def _hi_lo(w):
    hi = w.astype(BF16)
    return jnp.stack([hi, (w - hi.astype(F32)).astype(BF16)])


def _dot_split(x, w_hi, w_lo):
    xh = x.astype(BF16)
    xl = (x - xh.astype(F32)).astype(BF16)
    d = lambda a, b: jnp.dot(a, b, preferred_element_type=F32)
    return d(xh, w_hi) + (d(xh, w_lo) + d(xl, w_hi))


def _tril_cumsum(tril, x):
    x1 = x.astype(BF16)
    r1 = x - x1.astype(F32)
    x2 = r1.astype(BF16)
    x3 = (r1 - x2.astype(F32)).astype(BF16)
    d = lambda piece: jnp.dot(tril, piece, preferred_element_type=F32)
    return d(x1) + (d(x2) + d(x3))


def _rwkv_pre_kernel(pb_ref, mu_ref, w0_ref, a0_ref, kk_ref, ka_ref, rk_ref, wup_ref, aup_ref, gup_ref,
                     bd_ref, tril_ref, ab_ref, bb_ref, kb_ref, rb_ref, v_ref, p_ref, g_ref, bonus_ref,
                     carry_ref):
    @pl.when(pl.program_id(1) == 0)
    def _():
        carry_ref[...] = jnp.zeros_like(carry_ref)

    x = pb_ref[0]
    tm = x.shape[0]
    rows = lax.broadcasted_iota(jnp.int32, x.shape, 0)
    prev = jnp.where(rows == 0, carry_ref[...], pltpu.roll(x, 1, 0))
    carry_ref[...] = x[tm - 1:tm]
    ps = x + mu_ref[...] * (prev - x)
    r = ps[:, :W_B]
    k = ps[:, W_B:2 * W_B]
    v = ps[:, 2 * W_B:3 * W_B]
    o = 3 * W_B
    wd = ps[:, o:o + LORA_W]
    ad = ps[:, o + LORA_W:o + LORA_W + LORA_A]
    gd = ps[:, o + LORA_W + LORA_A:]
    z = w0_ref[...] + _dot_split(jnp.tanh(wd), wup_ref[0], wup_ref[1])
    u = -z
    w = -(jnp.maximum(u, 0.0) + jnp.log(1.0 + jnp.exp(-jnp.abs(u)))) - 0.5
    logd = -jnp.exp(w)
    logp = _tril_cumsum(tril_ref[...], logd)
    pinv = jnp.exp(-logp)
    a = _sigmoid(a0_ref[...] + _dot_split(ad, aup_ref[0], aup_ref[1]))
    g_ref[0] = jnp.dot(_sigmoid(gd).astype(BF16), gup_ref[...], preferred_element_type=F32)
    bd = bd_ref[...]
    kk = k * kk_ref[...]
    kk = kk * lax.rsqrt(jnp.maximum(_seg_sum(kk * kk, bd), 1e-24))
    k2 = k * (1.0 + (a - 1.0) * ka_ref[...])
    p = jnp.exp(logp)
    ab_ref[0] = -kk * jnp.exp(logp - logd)
    bb_ref[0] = kk * a * pinv
    kb_ref[0] = k2 * pinv
    rb_ref[0] = r * p
    v_ref[0] = v
    p_ref[0] = p
    bonus_ref[0] = _seg_sum(r * k2 * rk_ref[...], bd) * v


def _rwkv_pre(pb, mu, w0, a0, k_k, k_a, r_k, w_up, a_up, g_up, bd, tm=128):
    bsz, l, _ = pb.shape
    tm = min(tm, l)
    assert tm % RW_CHUNK == 0
    tril = jnp.kron(jnp.eye(tm // RW_CHUNK, dtype=F32), jnp.tril(jnp.ones((RW_CHUNK, RW_CHUNK), F32))).astype(BF16)
    full = lambda shp: pl.BlockSpec(shp, lambda b, i: (0,) * len(shp))
    row = pl.BlockSpec((1, tm, W_B), lambda b, i: (b, i, 0))
    vec = full((1, W_B))
    return pl.pallas_call(
        _rwkv_pre_kernel,
        grid=(bsz, l // tm),
        in_specs=[pl.BlockSpec((1, tm, B_PAD), lambda b, i: (b, i, 0)), full((1, B_PAD)),
                  vec, vec, vec, vec, vec,
                  full((2, LORA_W, W_B)), full((2, LORA_A, W_B)), full((LORA_G_PAD, W_B)), full((LANES, LANES)),
                  full((tm, tm))],
        out_specs=[row] * 8,
        out_shape=[jax.ShapeDtypeStruct((bsz, l, W_B), F32)] * 8,
        scratch_shapes=[pltpu.VMEM((1, B_PAD), F32)],
        compiler_params=_cparams("parallel", "arbitrary"),
        name="rwkv_pre",
    )(pb, mu, w0.reshape(1, W_B), a0.reshape(1, W_B), k_k.reshape(1, W_B), k_a.reshape(1, W_B),
      r_k.reshape(1, W_B), _hi_lo(w_up), _hi_lo(a_up), g_up, bd, tril)


def _mm(a, b, dims):
    return lax.dot_general(a.astype(BF16), b.astype(BF16), dims, preferred_element_type=F32)


def _rwkv_chunk_kernel(ab_ref, bb_ref, kb_ref, rb_ref, v_ref, pe_ref, y_ref, s_ref, *, npair):
    @pl.when(pl.program_id(1) == 0)
    def _():
        s_ref[...] = jnp.zeros_like(s_ref)

    c = RW_CHUNK
    ri = lax.broadcasted_iota(jnp.int32, (c, c), 0)
    ci = lax.broadcasted_iota(jnp.int32, (c, c), 1)
    strict, incl = ci < ri, ci <= ri
    eye = jnp.where(ri == ci, 1.0, 0.0)
    head0 = lax.broadcasted_iota(jnp.int32, (1, LANES), 1) < N_B
    bi = lax.broadcasted_iota(jnp.int32, (LANES, LANES), 0) < N_B
    bj = lax.broadcasted_iota(jnp.int32, (LANES, LANES), 1) < N_B
    blockdiag = bi == bj
    nt = (((2,), (2,)), ((0,), (0,)))
    nn = (((2,), (1,)), ((0,), (0,)))
    tn = (((1,), (1,)), ((0,), (0,)))
    tiles = lambda ref: jnp.stack([ref[0, :, p * LANES:(p + 1) * LANES] for p in range(npair)])
    ab, bb, kb, rb, v = tiles(ab_ref), tiles(bb_ref), tiles(kb_ref), tiles(rb_ref), tiles(v_ref)
    pe = jnp.stack([pe_ref[0, 0, :, p * LANES:(p + 1) * LANES] for p in range(npair)])
    s0 = s_ref[...]
    both = lambda x: jnp.concatenate([x, x], axis=0)
    pick = lambda x: jnp.where(head0, x[:npair], x[npair:])
    split = lambda x: jnp.concatenate([jnp.where(head0, x, 0.0), jnp.where(head0, 0.0, x)], axis=0)
    ar = jnp.concatenate([ab, rb], axis=1)
    ars = split(ar)
    gb = _mm(ars, both(bb), nt)
    gk = _mm(ars, both(kb), nt)
    m_ab = jnp.where(strict, gb[:, :c], 0.0)
    n_rb = jnp.where(incl, gb[:, c:], 0.0)
    m_ak = jnp.where(strict, gk[:, :c], 0.0)
    n_rk = jnp.where(incl, gk[:, c:], 0.0)
    tinv = eye + m_ab
    pw = m_ab
    for _ in range(5):
        pw = _mm(pw, pw, nn)
        tinv = tinv + _mm(tinv, pw, nn)
    x0 = _mm(ar, s0, nt)
    v2 = both(v)
    x = x0[:, :c] + pick(_mm(m_ak, v2, nn))
    u = pick(_mm(tinv, both(x), nn))
    y = x0[:, c:] + pick(_mm(n_rb, both(u), nn)) + pick(_mm(n_rk, v2, nn))
    for p in range(npair):
        y_ref[0, :, p * LANES:(p + 1) * LANES] = y[p]
    upd = _mm(jnp.concatenate([u, v], axis=1), jnp.concatenate([bb, kb], axis=1), tn)
    s_ref[...] = (s0 + jnp.where(blockdiag, upd, 0.0)) * pe


def _rwkv_chunks(ab, bb, kb, rb, v, pe):
    bsz, l, wb = ab.shape
    nt = l // RW_CHUNK
    npair = wb // LANES
    row = pl.BlockSpec((1, RW_CHUNK, wb), lambda b, i: (b, i, 0))
    return pl.pallas_call(
        functools.partial(_rwkv_chunk_kernel, npair=npair),
        grid=(bsz, nt),
        in_specs=[row] * 5 + [pl.BlockSpec((1, 1, 1, wb), lambda b, i: (b, i, 0, 0))],
        out_specs=row,
        out_shape=jax.ShapeDtypeStruct((bsz, l, wb), F32),
        scratch_shapes=[pltpu.VMEM((npair, LANES, LANES), F32)],
        compiler_params=_cparams("parallel", "arbitrary"),
        name="rwkv_chunks",
    )(ab, bb, kb, rb, v, pe)


def _rwkv_post_kernel(y_ref, bonus_ref, g_ref, lw_ref, lb_ref, bd_ref, o_ref):
    bd = bd_ref[...]
    y = y_ref[0]
    mean = _seg_sum(y, bd) * (1.0 / N_B)
    c = y - mean
    var = _seg_sum(c * c, bd) * (1.0 / N_B)
    yn = c * lax.rsqrt(var + GN_EPS) * lw_ref[...] + lb_ref[...]
    o_ref[0] = ((yn + bonus_ref[0]) * g_ref[0]).astype(o_ref.dtype)


def _rwkv_post(y, bonus, g, lnx_w, lnx_b, bd, tm=256):
    bsz, l, _ = y.shape
    tm = min(tm, l)
    row = pl.BlockSpec((1, tm, W_B), lambda b, i: (b, i, 0))
    vec = pl.BlockSpec((1, W_B), lambda b, i: (0, 0))
    return pl.pallas_call(
        _rwkv_post_kernel,
        grid=(bsz, l // tm),
        in_specs=[row, row, row, vec, vec, pl.BlockSpec((LANES, LANES), lambda b, i: (0, 0))],
        out_specs=row,
        out_shape=jax.ShapeDtypeStruct((bsz, l, W_B), BF16),
        compiler_params=_cparams("parallel", "parallel"),
        name="rwkv_post",
    )(y, bonus, g, lnx_w.reshape(1, W_B), lnx_b.reshape(1, W_B), bd)


def _rwkv_mixer(pb, mu, w0, w_up, a0, a_up, g_up, k_k, k_a, r_k, lnx_w, lnx_b):
    bsz, l, _ = pb.shape
    nt = l // RW_CHUNK
    eye2 = jnp.kron(jnp.eye(2, dtype=F32), jnp.ones((N_B, N_B), F32)).astype(BF16)
    ab, bb, kb, rb, v, p, g, bonus = _rwkv_pre(pb, mu, w0, a0, k_k, k_a, r_k, w_up, a_up, g_up, eye2)
    pe = p[:, RW_CHUNK - 1::RW_CHUNK].reshape(bsz, nt, 1, W_B)
    y = _rwkv_chunks(ab, bb, kb, rb, v, pe)
    return _rwkv_post(y, bonus, g, lnx_w, lnx_b, eye2)


def _hgrn_kernel(q_ref, f_ref, i_ref, g_ref, lb_ref, on_ref, tril_ref, o_ref, st_ref, *, layer, hb, tt):
    @pl.when(pl.program_id(2) == 0)
    def _():
        st_ref[...] = jnp.zeros_like(st_ref)

    x = lb_ref[...]
    e = jnp.exp(x - jnp.max(x, axis=0, keepdims=True))
    sm = e / jnp.sum(e, axis=0, keepdims=True)
    cs = sm[0:1]
    for i in range(1, layer + 1):
        cs = cs + sm[i:i + 1]
    lb_all = cs - sm[0:1]
    rowi = lax.broadcasted_iota(jnp.int32, (SUB, LANES), 0)
    lanei = lax.broadcasted_iota(jnp.int32, (SUB, LANES), 1)
    prep = []
    for hh in range(hb):
        sl = slice(hh * LANES, (hh + 1) * LANES)
        lb = lb_all[:, sl]
        qraw = q_ref[0][:, sl]
        ff = f_ref[0][:, sl]
        q = qraw * _sigmoid(qraw)
        logsig = jnp.minimum(ff, 0.0) - jnp.log(1.0 + jnp.exp(-jnp.abs(ff)))
        la = jnp.log(lb)
        lbv = jnp.log(1.0 - lb) + logsig
        mx = jnp.maximum(la, lbv)
        logf = mx + jnp.log(jnp.exp(la - mx) + jnp.exp(lbv - mx))
        k = (1.0 - lb) * _sigmoid(-ff)
        bcum = _tril_cumsum(tril_ref[...], logf * LOG2E)
        prep.append((q, k, i_ref[0][:, sl], bcum))
    sts = [st_ref[hh] for hh in range(hb)]
    outs = [[] for _ in range(hb)]
    for n in range(tt // SUB):
        rs = slice(n * SUB, (n + 1) * SUB)
        for hh in range(hb):
            q, k, v, bcum = prep[hh]
            st = sts[hh]
            bq, qn, kn, vn = bcum[rs], q[rs], k[rs], v[rs]
            bend = bq[SUB - 1:SUB]
            qd = (qn * jnp.exp2(bq)).astype(BF16)
            o = lax.dot_general(qd, st.astype(BF16), NT_DIMS, preferred_element_type=F32)
            a = jnp.zeros((SUB, LANES), F32)
            for s in range(SUB):
                wgt = kn[s:s + 1] * jnp.exp2(jnp.minimum(bq - bq[s:s + 1], 0.0))
                a = jnp.where(lanei == s, jnp.sum(qn * wgt, axis=1, keepdims=True), a)
            a = jnp.where(lanei <= rowi, a, 0.0)[:, :SUB]
            o = o + jnp.dot(a.astype(BF16), vn.astype(BF16), preferred_element_type=F32)
            kd = (kn * jnp.exp2(bend - bq)).astype(BF16)
            sts[hh] = st * jnp.exp2(bend) + lax.dot_general(vn.astype(BF16), kd, TN_DIMS,
                                                           preferred_element_type=F32)
            outs[hh].append(o)
    for hh in range(hb):
        sl = slice(hh * LANES, (hh + 1) * LANES)
        st_ref[hh] = sts[hh]
        gg = g_ref[0][:, sl]
        o = jnp.concatenate(outs[hh], axis=0)
        o = _rms(o) * on_ref[:, sl] * (gg * _sigmoid(gg))
        o_ref[0, :, sl] = o.astype(o_ref.dtype)


def _hgrn_mixer(pc, hgrn_lb, onorm, layer, hb=12, tt=128):
    bsz, l, _ = pc.shape
    depth = hgrn_lb.shape[0]
    tt = min(tt, l)
    ng = H_C // hb
    wblk = hb * LANES
    tril = jnp.kron(jnp.eye(tt // SUB, dtype=F32), jnp.tril(jnp.ones((SUB, SUB), F32))).astype(BF16)
    col = lambda j: pl.BlockSpec((1, tt, wblk), lambda b, h, i, j=j: (b, i, j * ng + h))
    return pl.pallas_call(
        functools.partial(_hgrn_kernel, layer=layer, hb=hb, tt=tt),
        grid=(bsz, ng, l // tt),
        in_specs=[col(0), col(1), col(2), col(3),
                  pl.BlockSpec((depth, wblk), lambda b, h, i: (0, h)),
                  pl.BlockSpec((1, wblk), lambda b, h, i: (0, h)),
                  pl.BlockSpec((tt, tt), lambda b, h, i: (0, 0))],
        out_specs=pl.BlockSpec((1, tt, wblk), lambda b, h, i: (b, i, h)),
        out_shape=jax.ShapeDtypeStruct((bsz, l, W_C), BF16),
        scratch_shapes=[pltpu.VMEM((hb, DV_C, DK_C), F32)],
        compiler_params=_cparams("parallel", "parallel", "arbitrary"),
        name="hgrn2",
    )(pc, pc, pc, pc, hgrn_lb, onorm.reshape(1, W_C), tril)


def _pack_w_in(w):
    d = w.shape[0]
    a_cols = R_Q + R_KV + D_I + H_I
    b_cols = 3 * W_B + LORA_W + LORA_A + LORA_G
    wa, wb, wc = w[:, :a_cols], w[:, a_cols:a_cols + b_cols], w[:, a_cols + b_cols:]
    z = lambda n: jnp.zeros((d, n), w.dtype)
    o = R_Q + R_KV
    wa = jnp.concatenate([wa[:, :o + D_I], z(LANES - D_I), wa[:, o + D_I:], z(LANES - H_I)], axis=1)
    wb = jnp.concatenate([wb, z(LORA_G_PAD - LORA_G)], axis=1)
    return wa.astype(BF16), wb.astype(BF16), wc.astype(BF16)


def kernel(x, c, rel_bias, hgrn_lb, ada_w, ada_b, norm_g, w_in, w_out, mla_q_norm, mla_kv_norm, w_uq, w_uk, w_uv, w_qidx, rwkv_mu, rwkv_w0, rwkv_w_up, rwkv_a0, rwkv_a_up, rwkv_g_up, rwkv_k_k, rwkv_k_a, rwkv_r_k, rwkv_lnx_w, rwkv_lnx_b, hgrn_onorm, w_ff1, w_ff2):
    bsz, l, d = x.shape
    depth = ada_w.shape[0]
    m = bsz * l
    mod = _ada_mod(c, ada_w, ada_b)
    mods = [[mod[i, :, None, j * d:(j + 1) * d] for j in range(6)] for i in range(depth)]
    bias = _bias_tiles(rel_bias)
    h = _norm_mod(x, norm_g[0, 0], mods[0][1], mods[0][0])
    for i in range(depth):
        sh_m, sc_m, g_m, sh_f, sc_f, g_f = mods[i]
        wa, wb, wc = _pack_w_in(w_in[i])
        h2d = h.reshape(m, d)
        pa = _matmul(h2d, wa, F32, "in_proj_a").reshape(bsz, l, -1)
        pb = _matmul(h2d, wb, F32, "in_proj_b").reshape(bsz, l, -1)
        pc = _matmul(h2d, wc, F32, "in_proj_c").reshape(bsz, l, -1)
        wuq = w_uq[i].reshape(R_Q, H_A * DH_A).astype(BF16)
        wqi = jnp.transpose(w_qidx[i], (1, 2, 0)).astype(BF16)
        qlat, qidx, widx, kidx, ckv = _dsa_prep(pa, mla_q_norm[i], mla_kv_norm[i], wuq,
                                                w_uk[i].astype(BF16), wqi)
        y_a = _dsa_attn(qlat, qidx, widx, kidx, ckv, w_uv[i].astype(BF16), bias).transpose(0, 2, 1)
        mu = jnp.concatenate([rwkv_mu[i], jnp.zeros((LORA_G_PAD - LORA_G,), F32)]).reshape(1, B_PAD)
        gup = jnp.concatenate([rwkv_g_up[i], jnp.zeros((LORA_G_PAD - LORA_G, W_B), F32)]).astype(BF16)
        y_b = _rwkv_mixer(pb, mu, rwkv_w0[i], rwkv_w_up[i], rwkv_a0[i], rwkv_a_up[i], gup,
                          rwkv_k_k[i], rwkv_k_a[i], rwkv_r_k[i], rwkv_lnx_w[i], rwkv_lnx_b[i])
        y_c = _hgrn_mixer(pc, hgrn_lb, hgrn_onorm[i], i)
        ycat = jnp.concatenate([y_a, y_b, y_c], axis=-1).reshape(m, -1)
        y = _matmul(ycat, w_out[i].astype(BF16), F32, "out_proj").reshape(bsz, l, d)
        x, h = _resid_norm(x, y, g_m, norm_g[i, 1], (norm_g[i, 2], sc_f, sh_f))
        u = _matmul(h.reshape(m, d), w_ff1[i].astype(BF16), BF16, "ffn_up", relu2=True)
        y = _matmul(u, w_ff2[i].astype(BF16), F32, "ffn_down").reshape(bsz, l, d)
        nxt = None if i == depth - 1 else (norm_g[i + 1, 0], mods[i + 1][1], mods[i + 1][0])
        x, h = _resid_norm(x, y, g_f, norm_g[i, 3], nxt)
    return x
```

```python
import functools
import math

import jax
import jax.numpy as jnp
from jax import lax
from jax.experimental import pallas as pl
from jax.experimental.pallas import tpu as pltpu

H_A, DH_A, R_Q, R_KV, H_I, D_I = 8, 128, 768, 256, 16, 64
TOPK_MAX, Q_BLOCK, N_BUCKETS, MAX_DIST = 256, 128, 32, 128
H_B, N_B, LORA_W, LORA_A, LORA_G = 24, 64, 128, 128, 480
W_B = H_B * N_B
H_C, DK_C, DV_C = 12, 128, 128
W_C = H_C * DK_C
EPS = 1e-6
GN_EPS = 64e-5

LANES = 128
KEY_CHUNK = 256
SUB = 8
RW_CHUNK = 64
SUBLANES = 8
I16_MIN, I16_MAX = -2 ** 15, 2 ** 15 - 1
LORA_G_PAD = 512
A_PAD = 1280
B_PAD = 3 * W_B + LORA_W + LORA_A + LORA_G_PAD
VMEM_LIMIT = 56 * 1024 * 1024
MM_TILES = {"in_proj_a": (512, A_PAD, 4096), "in_proj_b": (1024, 768, 4096), "in_proj_c": (1024, 1024, 4096),
            "out_proj": (1024, 1024, 4096), "ffn_up": (1024, 1024, 4096), "ffn_down": (1024, 1024, 4096)}

F32 = jnp.float32
BF16 = jnp.bfloat16
NT_DIMS = (((1,), (1,)), ((), ()))
TN_DIMS = (((0,), (0,)), ((), ()))
INT_MIN, INT_MAX = -2 ** 31, 2 ** 31 - 1
LOG2E = math.log2(math.e)


def _cparams(*sem):
    return pltpu.CompilerParams(dimension_semantics=sem, vmem_limit_bytes=VMEM_LIMIT)


def _sigmoid(x):
    return 1.0 / (1.0 + jnp.exp(-x))


def _rms(x, eps=EPS):
    return x * lax.rsqrt(jnp.mean(x * x, axis=-1, keepdims=True) + eps)


def _mm_kernel(a_ref, b_ref, o_ref, acc_ref, *, nk, relu2):
    k = pl.program_id(2)

    @pl.when(k == 0)
    def _():
        acc_ref[...] = jnp.zeros_like(acc_ref)

    acc_ref[...] += jnp.dot(a_ref[...], b_ref[...], preferred_element_type=F32)

    @pl.when(k == nk - 1)
    def _():
        r = acc_ref[...]
        if relu2:
            r = jnp.square(jnp.maximum(r, 0.0))
        o_ref[...] = r.astype(o_ref.dtype)


def _mm_full_k_kernel(a_ref, b_ref, o_ref, *, relu2):
    r = jnp.dot(a_ref[...], b_ref[...], preferred_element_type=F32)
    if relu2:
        r = jnp.square(jnp.maximum(r, 0.0))
    o_ref[...] = r.astype(o_ref.dtype)


def _matmul(a, b, out_dtype, name, relu2=False):
    m, kdim = a.shape
    _, n = b.shape
    tm, tn, tk = (min(t, full) for t, full in zip(MM_TILES[name], (m, n, kdim)))
    assert m % tm == 0 and n % tn == 0 and kdim % tk == 0, (a.shape, b.shape, tm, tn, tk)
    nk = kdim // tk
    if nk == 1:
        return pl.pallas_call(
            functools.partial(_mm_full_k_kernel, relu2=relu2),
            grid=(m // tm, n // tn),
            in_specs=[pl.BlockSpec((tm, kdim), lambda i, j: (i, 0)),
                      pl.BlockSpec((kdim, tn), lambda i, j: (0, j))],
            out_specs=pl.BlockSpec((tm, tn), lambda i, j: (i, j)),
            out_shape=jax.ShapeDtypeStruct((m, n), out_dtype),
            compiler_params=_cparams("parallel", "parallel"),
            name=name,
        )(a, b)
    return pl.pallas_call(
        functools.partial(_mm_kernel, nk=nk, relu2=relu2),
        grid=(m // tm, n // tn, nk),
        in_specs=[pl.BlockSpec((tm, tk), lambda i, j, k: (i, k)),
                  pl.BlockSpec((tk, tn), lambda i, j, k: (k, j))],
        out_specs=pl.BlockSpec((tm, tn), lambda i, j, k: (i, j)),
        out_shape=jax.ShapeDtypeStruct((m, n), out_dtype),
        scratch_shapes=[pltpu.VMEM((tm, tn), F32)],
        compiler_params=_cparams("parallel", "parallel", "arbitrary"),
        name=name,
    )(a, b)


def _ada_kernel(c_ref, w_ref, b_ref, o_ref):
    k = pl.program_id(2)

    @pl.when(k == 0)
    def _():
        o_ref[0] = jnp.broadcast_to(b_ref[0], o_ref.shape[1:])

    c = c_ref[...]
    ca = (c * _sigmoid(c)).astype(BF16)
    o_ref[0] += jnp.dot(ca, w_ref[0].astype(BF16), preferred_element_type=F32)


def _ada_mod(c, ada_w, ada_b):
    nl, d, n = ada_w.shape
    bsz = c.shape[0]
    rows = SUBLANES
    cp = jnp.zeros((rows, d), F32).at[:bsz].set(c)
    tn, tk = min(2048, n), min(1024, d)
    out = pl.pallas_call(
        _ada_kernel,
        grid=(nl, n // tn, d // tk),
        in_specs=[pl.BlockSpec((rows, tk), lambda l, j, k: (0, k)),
                  pl.BlockSpec((1, tk, tn), lambda l, j, k: (l, k, j)),
                  pl.BlockSpec((1, 1, tn), lambda l, j, k: (l, 0, j))],
        out_specs=pl.BlockSpec((1, rows, tn), lambda l, j, k: (l, 0, j)),
        out_shape=jax.ShapeDtypeStruct((nl, rows, n), F32),
        compiler_params=_cparams("parallel", "parallel", "arbitrary"),
        name="ada_mod",
    )(cp, ada_w, ada_b.reshape(nl, 1, n))
    return out[:, :bsz]


def _norm_mod_kernel(x_ref, g_ref, sc_ref, sh_ref, h_ref):
    y = _rms(x_ref[0]) * g_ref[...]
    h_ref[0] = (y * (1.0 + sc_ref[0]) + sh_ref[0]).astype(h_ref.dtype)


def _norm_mod(x, g, sc, sh, tm=256):
    bsz, l, d = x.shape
    tm = min(tm, l)
    row = pl.BlockSpec((1, tm, d), lambda b, i: (b, i, 0))
    vec = pl.BlockSpec((1, d), lambda b, i: (0, 0))
    bvec = pl.BlockSpec((1, 1, d), lambda b, i: (b, 0, 0))
    return pl.pallas_call(
        _norm_mod_kernel,
        grid=(bsz, l // tm),
        in_specs=[row, vec, bvec, bvec],
        out_specs=row,
        out_shape=jax.ShapeDtypeStruct((bsz, l, d), BF16),
        compiler_params=_cparams("parallel", "parallel"),
        name="norm_mod",
    )(x, g.reshape(1, d), sc, sh)


def _resid_kernel(x_ref, y_ref, gate_ref, g1_ref, g2_ref, sc_ref, sh_ref, xo_ref, h_ref):
    xn = x_ref[0] + gate_ref[0] * (_rms(y_ref[0]) * g1_ref[...])
    xo_ref[0] = xn
    hn = _rms(xn) * g2_ref[...]
    h_ref[0] = (hn * (1.0 + sc_ref[0]) + sh_ref[0]).astype(h_ref.dtype)


def _resid_last_kernel(x_ref, y_ref, gate_ref, g1_ref, xo_ref):
    xo_ref[0] = x_ref[0] + gate_ref[0] * (_rms(y_ref[0]) * g1_ref[...])


def _resid_norm(x, y, gate, g1, nxt=None, tm=128):
    bsz, l, d = x.shape
    tm = min(tm, l)
    row = pl.BlockSpec((1, tm, d), lambda b, i: (b, i, 0))
    vec = pl.BlockSpec((1, d), lambda b, i: (0, 0))
    bvec = pl.BlockSpec((1, 1, d), lambda b, i: (b, 0, 0))
    if nxt is None:
        return pl.pallas_call(
            _resid_last_kernel,
            grid=(bsz, l // tm),
            in_specs=[row, row, bvec, vec],
            out_specs=row,
            out_shape=jax.ShapeDtypeStruct((bsz, l, d), F32),
            compiler_params=_cparams("parallel", "parallel"),
            name="resid_last",
        )(x, y, gate, g1.reshape(1, d)), None
    g2, sc, sh = nxt
    return pl.pallas_call(
        _resid_kernel,
        grid=(bsz, l // tm),
        in_specs=[row, row, bvec, vec, vec, bvec, bvec],
        out_specs=[row, row],
        out_shape=[jax.ShapeDtypeStruct((bsz, l, d), F32), jax.ShapeDtypeStruct((bsz, l, d), BF16)],
        compiler_params=_cparams("parallel", "parallel"),
        name="resid_norm",
    )(x, y, gate, g1.reshape(1, d), g2.reshape(1, d), sc, sh)


def _bias_kernel(rb_ref, o_ref):
    j = lax.broadcasted_iota(jnp.int32, (KEY_CHUNK, Q_BLOCK), 0)
    i = lax.broadcasted_iota(jnp.int32, (KEY_CHUNK, Q_BLOCK), 1)
    max_exact = N_BUCKETS // 2
    for v, off in enumerate((Q_BLOCK, 0, 2 * Q_BLOCK, None)):
        if off is None:
            dist = jnp.full((KEY_CHUNK, Q_BLOCK), 2 * MAX_DIST, jnp.int32)
        else:
            dist = jnp.maximum(i + off - j, 0)
        nf = jnp.maximum(dist, 1).astype(F32)
        large = max_exact + (jnp.log(nf / max_exact) / math.log(MAX_DIST / max_exact)
                             * (N_BUCKETS - max_exact)).astype(jnp.int32)
        large = jnp.minimum(large, N_BUCKETS - 1)
        bucket = jnp.where(dist < max_exact, dist, large)
        for h in range(H_A):
            def body(b, acc, h=h, bucket=bucket):
                return jnp.where(bucket == b, rb_ref[b * H_A + h] * LOG2E, acc)
            o_ref[v, h] = lax.fori_loop(0, N_BUCKETS, body, jnp.zeros((KEY_CHUNK, Q_BLOCK), F32))


def _bias_tiles(rel_bias):
    return pl.pallas_call(
        _bias_kernel,
        in_specs=[pl.BlockSpec(memory_space=pltpu.SMEM)],
        out_specs=pl.BlockSpec(memory_space=pltpu.VMEM),
        out_shape=jax.ShapeDtypeStruct((4, H_A, KEY_CHUNK, Q_BLOCK), F32),
        name="dsa_bias_tiles",
    )(rel_bias.reshape(-1))


def _dsa_prep_kernel(pa_ref, gq_ref, gkv_ref, wuq_ref, wuk_ref, wqi_ref,
                     qlat_ref, qidx_ref, widx_ref, kidx_ref, ckv_ref):
    pa = pa_ref[0]
    cq = pa[:, :R_Q]
    ckv = pa[:, R_Q:R_Q + R_KV]
    kid = pa[:, R_Q + R_KV:R_Q + R_KV + D_I]
    wid = pa[:, R_Q + R_KV + LANES:R_Q + R_KV + LANES + H_I]
    cqn = (_rms(cq) * gq_ref[...]).astype(BF16)
    ckv_ref[0] = (_rms(ckv) * gkv_ref[...]).astype(BF16)
    kidx_ref[0] = kid.astype(BF16)
    widx_ref[0] = wid * (H_I ** -0.5 * D_I ** -0.5)
    q = jnp.dot(cqn, wuq_ref[...], preferred_element_type=F32)
    for h in range(H_A):
        qh = q[:, h * DH_A:(h + 1) * DH_A].astype(BF16)
        ql = lax.dot_general(wuk_ref[h], qh, NT_DIMS, preferred_element_type=F32)
        qlat_ref[0, 0, :, h * Q_BLOCK:(h + 1) * Q_BLOCK] = (ql * (DH_A ** -0.5 * LOG2E)).astype(BF16)
    for h in range(H_I):
        qi = lax.dot_general(wqi_ref[h], cqn, NT_DIMS, preferred_element_type=F32)
        qidx_ref[0, 0, :, h * Q_BLOCK:(h + 1) * Q_BLOCK] = qi.astype(BF16)


def _dsa_prep(pa, gq, gkv, wuq, wuk, wqi):
    bsz, l, _ = pa.shape
    nb = l // Q_BLOCK
    full = lambda shp: pl.BlockSpec(shp, lambda b, i: (0,) * len(shp))
    return pl.pallas_call(
        _dsa_prep_kernel,
        grid=(bsz, nb),
        in_specs=[pl.BlockSpec((1, Q_BLOCK, A_PAD), lambda b, i: (b, i, 0)),
                  full((1, R_Q)), full((1, R_KV)), full((R_Q, H_A * DH_A)),
                  full((H_A, R_KV, DH_A)), full((H_I, D_I, R_Q))],
        out_specs=[pl.BlockSpec((1, 1, R_KV, H_A * Q_BLOCK), lambda b, i: (b, i, 0, 0)),
                   pl.BlockSpec((1, 1, D_I, H_I * Q_BLOCK), lambda b, i: (b, i, 0, 0)),
                   pl.BlockSpec((1, Q_BLOCK, H_I), lambda b, i: (b, i, 0)),
                   pl.BlockSpec((1, Q_BLOCK, D_I), lambda b, i: (b, i, 0)),
                   pl.BlockSpec((1, Q_BLOCK, R_KV), lambda b, i: (b, i, 0))],
        out_shape=[jax.ShapeDtypeStruct((bsz, nb, R_KV, H_A * Q_BLOCK), BF16),
                   jax.ShapeDtypeStruct((bsz, nb, D_I, H_I * Q_BLOCK), BF16),
                   jax.ShapeDtypeStruct((bsz, l, H_I), F32),
                   jax.ShapeDtypeStruct((bsz, l, D_I), BF16),
                   jax.ShapeDtypeStruct((bsz, l, R_KV), BF16)],
        compiler_params=_cparams("parallel", "parallel"),
        name="dsa_prep",
    )(pa, gq.reshape(1, R_Q), gkv.reshape(1, R_KV), wuq, wuk, wqi)


def _fold8(x, op, ways=1):
    n = x.shape[0] // SUBLANES
    ways = min(ways, n)
    part = lambda g: x[g * SUBLANES:(g + 1) * SUBLANES]
    acc = [part(g) for g in range(ways)]
    for g in range(ways, n):
        acc[g % ways] = op(acc[g % ways], part(g))
    while len(acc) > 1:
        acc = [op(acc[i], acc[i + 1]) for i in range(0, len(acc), 2)]
    return acc[0]


def _dsa_attn_kernel(qlat_ref, qidx_ref, widx_ref, kidx_ref, ckv_ref, ckvt_ref, wuvt_ref, bias_ref, o_ref,
                     key_ref, hi_ref, lo_ref, m_ref, ls_ref, acc_ref, sa_ref, sb_ref, lga_ref, lgb_ref,
                     pa_ref, pb_ref, aa_ref, ab_ref, *, topk):
    qi = pl.program_id(1)
    nch = (qi * Q_BLOCK) // KEY_CHUNK + 1
    krow = lax.broadcasted_iota(jnp.int32, (KEY_CHUNK, Q_BLOCK), 0)
    qlane = lax.broadcasted_iota(jnp.int32, (KEY_CHUNK, Q_BLOCK), 1)
    t_abs = qi * Q_BLOCK + qlane
    w = widx_ref[0, 0]

    npair = (nch + 1) // 2
    last_chunk = kidx_ref.shape[1] // KEY_CHUNK - 1

    def chunk_start(c):
        return pl.multiple_of(jnp.minimum(c, last_chunk) * KEY_CHUNK, KEY_CHUNK)

    def score_mm(c, s_ref):
        kc = kidx_ref[0, pl.ds(chunk_start(c), KEY_CHUNK), :]
        s_ref[...] = jnp.dot(kc, qidx_ref[0, 0], preferred_element_type=F32)

    def score_keys(c, s_ref):
        acc = jnp.zeros((KEY_CHUNK, Q_BLOCK), F32)
        for h in range(H_I):
            acc = acc + w[h:h + 1] * jnp.maximum(s_ref[:, h * Q_BLOCK:(h + 1) * Q_BLOCK], 0.0)
        acc = jnp.where(c * KEY_CHUNK + krow <= t_abs, acc, -jnp.inf)
        bits = lax.bitcast_convert_type(acc, jnp.int32)
        key = bits ^ ((bits >> 31) & 0x7FFFFFFF)
        key_ref[c] = key
        hi_ref[c] = (key >> 16).astype(jnp.int16)
        lo_ref[c] = ((key & 0xFFFF) + I16_MIN).astype(jnp.int16)

    def score_body(j, carry):
        c = 2 * j
        score_mm(c + 1, sb_ref)
        score_keys(c, sa_ref)
        score_mm(c + 2, sa_ref)
        score_keys(c + 1, sb_ref)
        return carry

    score_mm(0, sa_ref)
    lax.fori_loop(0, npair, score_body, 0)

    def count_ge16(ref, cand):
        c16 = cand.astype(jnp.int16)
        rows = 2 * SUBLANES

        def fold(x):
            r = x[0:rows]
            for g in range(1, x.shape[0] // rows):
                r = r + x[g * rows:(g + 1) * rows]
            return r

        def hits(c):
            return fold(jnp.where(ref[c] >= c16, jnp.int16(1), jnp.int16(0)))

        def quad(c, cnt):
            return cnt + ((hits(4 * c) + hits(4 * c + 1)) + (hits(4 * c + 2) + hits(4 * c + 3)))

        def tail(_, cnt):
            return cnt + (hits(2 * npair - 2) + hits(2 * npair - 1))

        cnt = lax.fori_loop(0, npair // 2, quad, jnp.zeros((rows, Q_BLOCK), jnp.int16))
        cnt = lax.fori_loop(0, npair % 2, tail, cnt)
        return jnp.sum(cnt.astype(F32), axis=0, keepdims=True)

    kf = float(topk)
    total = (npair * (2 * KEY_CHUNK)).astype(F32)
    few = (qi * Q_BLOCK + lax.broadcasted_iota(jnp.int32, (1, Q_BLOCK), 1)) < topk

    def unsettled(cnt_thr):
        return jnp.max(jnp.where(few | (cnt_thr == kf), 0.0, 1.0)) > 0.5

    def search16(ref, early_exit, cnt_min):
        cnt0 = count_ge16(ref, jnp.zeros((1, Q_BLOCK), jnp.int32))
        thr = jnp.where(cnt0 >= kf, 0, I16_MIN).astype(jnp.int32)
        cnt_thr = jnp.where(cnt0 >= kf, cnt0, cnt_min)

        def group_cond(st):
            return (st[0] < 4) & st[3]

        def group_body(st):
            g, thr, cnt_thr, _ = st
            for k in range(4):
                i = 4 * g + k
                bit = jnp.where(i < 15, jnp.left_shift(jnp.int32(1), jnp.maximum(14 - i, 0)), 0)
                cand = thr + bit
                cnt = count_ge16(ref, cand)
                take = (cnt >= kf) & (bit != 0)
                thr = jnp.where(take, cand, thr)
                cnt_thr = jnp.where(take, cnt, cnt_thr)
            return g + 1, thr, cnt_thr, (unsettled(cnt_thr) if early_exit else jnp.bool_(True))

        go = unsettled(cnt_thr) if early_exit else jnp.bool_(True)
        _, thr, cnt_thr, _ = lax.while_loop(group_cond, group_body, (jnp.int32(0), thr, cnt_thr, go))
        return thr, cnt_thr

    thr_hi, cnt_hi = search16(hi_ref, False, total)
    thi16 = thr_hi.astype(jnp.int16)

    def relabel(c, carry):
        for cc in (2 * c, 2 * c + 1):
            hi = hi_ref[cc]
            lo_ref[cc] = jnp.where(hi > thi16, jnp.int16(I16_MAX),
                                   jnp.where(hi == thi16, lo_ref[cc], jnp.int16(I16_MIN)))
        return carry

    lax.fori_loop(0, npair, relabel, 0)
    thr_lo, cnt_ge = search16(lo_ref, True, cnt_hi)
    thr = thr_hi * 2 ** 16 + (thr_lo - I16_MIN)

    tied = jnp.logical_not(few) & (cnt_ge > kf)
    idx_bits = max(1, (kidx_ref.shape[1] - 1).bit_length())

    def count32(pred):
        def body(c, cnt):
            return (cnt + _fold8(jnp.where(pred(2 * c), 1.0, 0.0), jnp.add, 4)
                    + _fold8(jnp.where(pred(2 * c + 1), 1.0, 0.0), jnp.add, 4))
        cnt = lax.fori_loop(0, npair, body, jnp.zeros((SUBLANES, Q_BLOCK), F32))
        return jnp.sum(cnt, axis=0, keepdims=True)

    def break_ties(_, carry):
        need = kf - count32(lambda c: key_ref[c] > thr)

        def tie_bit(i, cut):
            cand = cut + jnp.left_shift(jnp.int32(1), idx_bits - 1 - i)
            below = count32(lambda c: (key_ref[c] == thr) & (c * KEY_CHUNK + krow < cand))
            return jnp.where(below < need, cand, cut)

        cut = lax.fori_loop(0, idx_bits, tie_bit, jnp.zeros((1, Q_BLOCK), jnp.int32))
        cut = jnp.where(tied, cut, INT_MAX)

        def demote(c, carry):
            for cc in (2 * c, 2 * c + 1):
                key = key_ref[cc]
                key_ref[cc] = jnp.where((key == thr) & (cc * KEY_CHUNK + krow > cut), thr - 1, key)
            return carry

        return lax.fori_loop(0, npair, demote, carry)

    any_tied = jnp.max(jnp.where(tied, 1.0, 0.0)) > 0.5
    lax.fori_loop(0, jnp.where(any_tied, 1, 0), break_ties, 0)

    odd = (qi % 2) == 1

    def logits_mm(c, lg_ref):
        kv = ckv_ref[0, pl.ds(chunk_start(c), KEY_CHUNK), :]
        lg_ref[...] = jnp.dot(kv, qlat_ref[0, 0], preferred_element_type=F32)

    def chunk_mask(c):
        sel = (key_ref[c] >= thr) & (c * KEY_CHUNK + krow <= t_abs)
        idx = jnp.where(c == nch - 1, jnp.where(odd, 0, 1),
                        jnp.where((c == nch - 2) & jnp.logical_not(odd), 2, 3))
        return sel, idx

    m_ref[...] = jnp.full(m_ref.shape, -1e29, F32)
    ls_ref[...] = jnp.zeros(ls_ref.shape, F32)
    acc_ref[...] = jnp.zeros(acc_ref.shape, F32)
    pb_ref[...] = jnp.zeros(pb_ref.shape, BF16)
    ab_ref[...] = jnp.ones(ab_ref.shape, F32)

    def chunk_probs(c, lg_ref, p_ref, a_ref):
        sel, idx = chunk_mask(c)
        for h in range(H_A):
            cs = slice(h * Q_BLOCK, (h + 1) * Q_BLOCK)
            lg = jnp.where(sel, lg_ref[:, cs] + bias_ref[idx, h], -1e30)
            cm = _fold8(lg, jnp.maximum)
            for shift in (4, 2, 1):
                cm = jnp.maximum(cm, pltpu.roll(cm, shift, 0))
            m_old = m_ref[:, cs]
            m_new = jnp.maximum(m_old, cm[0:1])
            alpha = jnp.exp2(m_old - m_new)
            p = jnp.exp2(lg - m_new)
            ls_ref[:, cs] = alpha * ls_ref[:, cs] + _fold8(p, jnp.add)
            m_ref[:, cs] = m_new
            a_ref[:, cs] = alpha
            p_ref[:, cs] = p.astype(BF16)

    def values_mm(c, p_ref, a_ref):
        acc_ref[...] = acc_ref[...] * a_ref[...] + jnp.dot(
            ckvt_ref[0, :, pl.ds(chunk_start(c), KEY_CHUNK)], p_ref[...], preferred_element_type=F32)

    def attn_body(j, carry):
        c = 2 * j
        logits_mm(c + 1, lgb_ref)
        chunk_probs(c, lga_ref, pa_ref, aa_ref)
        values_mm(jnp.maximum(c - 1, 0), pb_ref, ab_ref)
        logits_mm(c + 2, lga_ref)
        chunk_probs(c + 1, lgb_ref, pb_ref, ab_ref)
        values_mm(c, pa_ref, aa_ref)
        return carry

    logits_mm(0, lga_ref)
    lax.fori_loop(0, npair, attn_body, 0)
    values_mm(2 * npair - 1, pb_ref, ab_ref)
    inv_l = 1.0 / jnp.sum(ls_ref[...], axis=0, keepdims=True)

    for h in range(H_A):
        cs = slice(h * Q_BLOCK, (h + 1) * Q_BLOCK)
        out = jnp.dot(wuvt_ref[h], acc_ref[:, cs].astype(BF16), preferred_element_type=F32)
        o_ref[0, h * DH_A:(h + 1) * DH_A, :] = (out * inv_l[:, cs]).astype(o_ref.dtype)


def _dsa_attn(qlat, qidx, widx, kidx, ckv, wuv, bias):
    bsz, l, _ = ckv.shape
    nb = l // Q_BLOCK
    topk = min(TOPK_MAX, l // 4)
    nchunks = 2 * ((l + 2 * KEY_CHUNK - 1) // (2 * KEY_CHUNK))
    hq = H_A * Q_BLOCK
    widx_t = widx.reshape(bsz, nb, Q_BLOCK, H_I).transpose(0, 1, 3, 2)
    ckv_t = ckv.transpose(0, 2, 1)
    wuv_t = wuv.transpose(0, 2, 1)
    return pl.pallas_call(
        functools.partial(_dsa_attn_kernel, topk=topk),
        grid=(bsz, nb),
        in_specs=[pl.BlockSpec((1, 1, R_KV, hq), lambda b, i: (b, i, 0, 0)),
                  pl.BlockSpec((1, 1, D_I, H_I * Q_BLOCK), lambda b, i: (b, i, 0, 0)),
                  pl.BlockSpec((1, 1, H_I, Q_BLOCK), lambda b, i: (b, i, 0, 0)),
                  pl.BlockSpec((1, l, D_I), lambda b, i: (b, 0, 0)),
                  pl.BlockSpec((1, l, R_KV), lambda b, i: (b, 0, 0)),
                  pl.BlockSpec((1, R_KV, l), lambda b, i: (b, 0, 0)),
                  pl.BlockSpec((H_A, DH_A, R_KV), lambda b, i: (0, 0, 0)),
                  pl.BlockSpec((4, H_A, KEY_CHUNK, Q_BLOCK), lambda b, i: (0, 0, 0, 0))],
        out_specs=pl.BlockSpec((1, H_A * DH_A, Q_BLOCK), lambda b, i: (b, 0, i)),
        out_shape=jax.ShapeDtypeStruct((bsz, H_A * DH_A, l), BF16),
        scratch_shapes=[pltpu.VMEM((nchunks, KEY_CHUNK, Q_BLOCK), jnp.int32),
                        pltpu.VMEM((nchunks, KEY_CHUNK, Q_BLOCK), jnp.int16),
                        pltpu.VMEM((nchunks, KEY_CHUNK, Q_BLOCK), jnp.int16),
                        pltpu.VMEM((1, hq), F32), pltpu.VMEM((8, hq), F32), pltpu.VMEM((R_KV, hq), F32),
                        pltpu.VMEM((KEY_CHUNK, H_I * Q_BLOCK), F32), pltpu.VMEM((KEY_CHUNK, H_I * Q_BLOCK), F32),
                        pltpu.VMEM((KEY_CHUNK, hq), F32), pltpu.VMEM((KEY_CHUNK, hq), F32),
                        pltpu.VMEM((KEY_CHUNK, hq), BF16), pltpu.VMEM((KEY_CHUNK, hq), BF16),
                        pltpu.VMEM((1, hq), F32), pltpu.VMEM((1, hq), F32)],
        compiler_params=_cparams("parallel", "arbitrary"),
        name="dsa_attn",
    )(qlat, qidx, widx_t, kidx, ckv, ckv_t, wuv_t, bias)


def _seg_sum(x, bd):
    hi = x.astype(BF16)
    lo = (x - hi.astype(F32)).astype(BF16)
    outs = []
    for p in range(x.shape[1] // LANES):
        sl = slice(p * LANES, (p + 1) * LANES)
        outs.append(jnp.dot(hi[:, sl], bd, preferred_element_type=F32)
                    + jnp.dot(lo[:, sl], bd, preferred_element_type=F32))
    return outs[0] if len(outs) == 1 else jnp.concatenate(outs, axis=1)


def _hi_lo(w):
    hi = w.astype(BF16)
    return jnp.stack([hi, (w - hi.astype(F32)).astype(BF16)])


def _dot_split(x, w_hi, w_lo):
    xh = x.astype(BF16)
    xl = (x - xh.astype(F32)).astype(BF16)
    d = lambda a, b: jnp.dot(a, b, preferred_element_type=F32)
    return d(xh, w_hi) + (d(xh, w_lo) + d(xl, w_hi))


def _tril_cumsum(tril, x):
    x1 = x.astype(BF16)
    r1 = x - x1.astype(F32)
    x2 = r1.astype(BF16)
    x3 = (r1 - x2.astype(F32)).astype(BF16)
    d = lambda piece: jnp.dot(tril, piece, preferred_element_type=F32)
    return d(x1) + (d(x2) + d(x3))


def _rwkv_pre_kernel(pb_ref, mu_ref, w0_ref, a0_ref, kk_ref, ka_ref, rk_ref, wup_ref, aup_ref, gup_ref,
                     bd_ref, tril_ref, ab_ref, bb_ref, kb_ref, rb_ref, v_ref, p_ref, g_ref, bonus_ref,
                     carry_ref):
    @pl.when(pl.program_id(1) == 0)
    def _():
        carry_ref[...] = jnp.zeros_like(carry_ref)

    x = pb_ref[0]
    tm = x.shape[0]
    rows = lax.broadcasted_iota(jnp.int32, x.shape, 0)
    prev = jnp.where(rows == 0, carry_ref[...], pltpu.roll(x, 1, 0))
    carry_ref[...] = x[tm - 1:tm]
    ps = x + mu_ref[...] * (prev - x)
    r = ps[:, :W_B]
    k = ps[:, W_B:2 * W_B]
    v = ps[:, 2 * W_B:3 * W_B]
    o = 3 * W_B
    wd = ps[:, o:o + LORA_W]
    ad = ps[:, o + LORA_W:o + LORA_W + LORA_A]
    gd = ps[:, o + LORA_W + LORA_A:]
    z = w0_ref[...] + _dot_split(jnp.tanh(wd), wup_ref[0], wup_ref[1])
    u = -z
    w = -(jnp.maximum(u, 0.0) + jnp.log(1.0 + jnp.exp(-jnp.abs(u)))) - 0.5
    logd = -jnp.exp(w)
    logp = _tril_cumsum(tril_ref[...], logd)
    pinv = jnp.exp(-logp)
    a = _sigmoid(a0_ref[...] + _dot_split(ad, aup_ref[0], aup_ref[1]))
    g_ref[0] = jnp.dot(_sigmoid(gd).astype(BF16), gup_ref[...], preferred_element_type=F32)
    bd = bd_ref[...]
    kk = k * kk_ref[...]
    kk = kk * lax.rsqrt(jnp.maximum(_seg_sum(kk * kk, bd), 1e-24))
    k2 = k * (1.0 + (a - 1.0) * ka_ref[...])
    p = jnp.exp(logp)
    ab_ref[0] = -kk * jnp.exp(logp - logd)
    bb_ref[0] = kk * a * pinv
    kb_ref[0] = k2 * pinv
    rb_ref[0] = r * p
    v_ref[0] = v
    p_ref[0] = p
    bonus_ref[0] = _seg_sum(r * k2 * rk_ref[...], bd) * v


def _rwkv_pre(pb, mu, w0, a0, k_k, k_a, r_k, w_up, a_up, g_up, bd, tm=128):
    bsz, l, _ = pb.shape
    tm = min(tm, l)
    assert tm % RW_CHUNK == 0
    tril = jnp.kron(jnp.eye(tm // RW_CHUNK, dtype=F32), jnp.tril(jnp.ones((RW_CHUNK, RW_CHUNK), F32))).astype(BF16)
    full = lambda shp: pl.BlockSpec(shp, lambda b, i: (0,) * len(shp))
    row = pl.BlockSpec((1, tm, W_B), lambda b, i: (b, i, 0))
    vec = full((1, W_B))
    return pl.pallas_call(
        _rwkv_pre_kernel,
        grid=(bsz, l // tm),
        in_specs=[pl.BlockSpec((1, tm, B_PAD), lambda b, i: (b, i, 0)), full((1, B_PAD)),
                  vec, vec, vec, vec, vec,
                  full((2, LORA_W, W_B)), full((2, LORA_A, W_B)), full((LORA_G_PAD, W_B)), full((LANES, LANES)),
                  full((tm, tm))],
        out_specs=[row] * 8,
        out_shape=[jax.ShapeDtypeStruct((bsz, l, W_B), F32)] * 8,
        scratch_shapes=[pltpu.VMEM((1, B_PAD), F32)],
        compiler_params=_cparams("parallel", "arbitrary"),
        name="rwkv_pre",
    )(pb, mu, w0.reshape(1, W_B), a0.reshape(1, W_B), k_k.reshape(1, W_B), k_a.reshape(1, W_B),
      r_k.reshape(1, W_B), _hi_lo(w_up), _hi_lo(a_up), g_up, bd, tril)


def _mm(a, b, dims):
    return lax.dot_general(a.astype(BF16), b.astype(BF16), dims, preferred_element_type=F32)


def _rwkv_chunk_kernel(ab_ref, bb_ref, kb_ref, rb_ref, v_ref, pe_ref, y_ref, s_ref, *, npair):
    @pl.when(pl.program_id(1) == 0)
    def _():
        s_ref[...] = jnp.zeros_like(s_ref)

    c = RW_CHUNK
    ri = lax.broadcasted_iota(jnp.int32, (c, c), 0)
    ci = lax.broadcasted_iota(jnp.int32, (c, c), 1)
    strict, incl = ci < ri, ci <= ri
    eye = jnp.where(ri == ci, 1.0, 0.0)
    head0 = lax.broadcasted_iota(jnp.int32, (1, LANES), 1) < N_B
    bi = lax.broadcasted_iota(jnp.int32, (LANES, LANES), 0) < N_B
    bj = lax.broadcasted_iota(jnp.int32, (LANES, LANES), 1) < N_B
    blockdiag = bi == bj
    nt = (((2,), (2,)), ((0,), (0,)))
    nn = (((2,), (1,)), ((0,), (0,)))
    tn = (((1,), (1,)), ((0,), (0,)))
    tiles = lambda ref: jnp.stack([ref[0, :, p * LANES:(p + 1) * LANES] for p in range(npair)])
    ab, bb, kb, rb, v = tiles(ab_ref), tiles(bb_ref), tiles(kb_ref), tiles(rb_ref), tiles(v_ref)
    pe = jnp.stack([pe_ref[0, 0, :, p * LANES:(p + 1) * LANES] for p in range(npair)])
    s0 = s_ref[...]
    both = lambda x: jnp.concatenate([x, x], axis=0)
    pick = lambda x: jnp.where(head0, x[:npair], x[npair:])
    split = lambda x: jnp.concatenate([jnp.where(head0, x, 0.0), jnp.where(head0, 0.0, x)], axis=0)
    ar = jnp.concatenate([ab, rb], axis=1)
    ars = split(ar)
    gb = _mm(ars, both(bb), nt)
    gk = _mm(ars, both(kb), nt)
    m_ab = jnp.where(strict, gb[:, :c], 0.0)
    n_rb = jnp.where(incl, gb[:, c:], 0.0)
    m_ak = jnp.where(strict, gk[:, :c], 0.0)
    n_rk = jnp.where(incl, gk[:, c:], 0.0)
    tinv = eye + m_ab
    pw = m_ab
    for _ in range(5):
        pw = _mm(pw, pw, nn)
        tinv = tinv + _mm(tinv, pw, nn)
    x0 = _mm(ar, s0, nt)
    v2 = both(v)
    x = x0[:, :c] + pick(_mm(m_ak, v2, nn))
    u = pick(_mm(tinv, both(x), nn))
    y = x0[:, c:] + pick(_mm(n_rb, both(u), nn)) + pick(_mm(n_rk, v2, nn))
    for p in range(npair):
        y_ref[0, :, p * LANES:(p + 1) * LANES] = y[p]
    upd = _mm(jnp.concatenate([u, v], axis=1), jnp.concatenate([bb, kb], axis=1), tn)
    s_ref[...] = (s0 + jnp.where(blockdiag, upd, 0.0)) * pe


def _rwkv_chunks(ab, bb, kb, rb, v, pe):
    bsz, l, wb = ab.shape
    nt = l // RW_CHUNK
    npair = wb // LANES
    row = pl.BlockSpec((1, RW_CHUNK, wb), lambda b, i: (b, i, 0))
    return pl.pallas_call(
        functools.partial(_rwkv_chunk_kernel, npair=npair),
        grid=(bsz, nt),
        in_specs=[row] * 5 + [pl.BlockSpec((1, 1, 1, wb), lambda b, i: (b, i, 0, 0))],
        out_specs=row,
        out_shape=jax.ShapeDtypeStruct((bsz, l, wb), F32),
        scratch_shapes=[pltpu.VMEM((npair, LANES, LANES), F32)],
        compiler_params=_cparams("parallel", "arbitrary"),
        name="rwkv_chunks",
    )(ab, bb, kb, rb, v, pe)


def _rwkv_post_kernel(y_ref, bonus_ref, g_ref, lw_ref, lb_ref, bd_ref, o_ref):
    bd = bd_ref[...]
    y = y_ref[0]
    mean = _seg_sum(y, bd) * (1.0 / N_B)
    c = y - mean
    var = _seg_sum(c * c, bd) * (1.0 / N_B)
    yn = c * lax.rsqrt(var + GN_EPS) * lw_ref[...] + lb_ref[...]
    o_ref[0] = ((yn + bonus_ref[0]) * g_ref[0]).astype(o_ref.dtype)


def _rwkv_post(y, bonus, g, lnx_w, lnx_b, bd, tm=256):
    bsz, l, _ = y.shape
    tm = min(tm, l)
    row = pl.BlockSpec((1, tm, W_B), lambda b, i: (b, i, 0))
    vec = pl.BlockSpec((1, W_B), lambda b, i: (0, 0))
    return pl.pallas_call(
        _rwkv_post_kernel,
        grid=(bsz, l // tm),
        in_specs=[row, row, row, vec, vec, pl.BlockSpec((LANES, LANES), lambda b, i: (0, 0))],
        out_specs=row,
        out_shape=jax.ShapeDtypeStruct((bsz, l, W_B), BF16),
        compiler_params=_cparams("parallel", "parallel"),
        name="rwkv_post",
    )(y, bonus, g, lnx_w.reshape(1, W_B), lnx_b.reshape(1, W_B), bd)


def _rwkv_mixer(pb, mu, w0, w_up, a0, a_up, g_up, k_k, k_a, r_k, lnx_w, lnx_b):
    bsz, l, _ = pb.shape
    nt = l // RW_CHUNK
    eye2 = jnp.kron(jnp.eye(2, dtype=F32), jnp.ones((N_B, N_B), F32)).astype(BF16)
    ab, bb, kb, rb, v, p, g, bonus = _rwkv_pre(pb, mu, w0, a0, k_k, k_a, r_k, w_up, a_up, g_up, eye2)
    pe = p[:, RW_CHUNK - 1::RW_CHUNK].reshape(bsz, nt, 1, W_B)
    y = _rwkv_chunks(ab, bb, kb, rb, v, pe)
    return _rwkv_post(y, bonus, g, lnx_w, lnx_b, eye2)


def _hgrn_kernel(q_ref, f_ref, i_ref, g_ref, lb_ref, on_ref, tril_ref, o_ref, st_ref, *, layer, hb, tt):
    @pl.when(pl.program_id(2) == 0)
    def _():
        st_ref[...] = jnp.zeros_like(st_ref)

    x = lb_ref[...]
    e = jnp.exp(x - jnp.max(x, axis=0, keepdims=True))
    sm = e / jnp.sum(e, axis=0, keepdims=True)
    cs = sm[0:1]
    for i in range(1, layer + 1):
        cs = cs + sm[i:i + 1]
    lb_all = cs - sm[0:1]
    rowi = lax.broadcasted_iota(jnp.int32, (SUB, LANES), 0)
    lanei = lax.broadcasted_iota(jnp.int32, (SUB, LANES), 1)
    prep = []
    for hh in range(hb):
        sl = slice(hh * LANES, (hh + 1) * LANES)
        lb = lb_all[:, sl]
        qraw = q_ref[0][:, sl]
        ff = f_ref[0][:, sl]
        q = qraw * _sigmoid(qraw)
        logsig = jnp.minimum(ff, 0.0) - jnp.log(1.0 + jnp.exp(-jnp.abs(ff)))
        la = jnp.log(lb)
        lbv = jnp.log(1.0 - lb) + logsig
        mx = jnp.maximum(la, lbv)
        logf = mx + jnp.log(jnp.exp(la - mx) + jnp.exp(lbv - mx))
        k = (1.0 - lb) * _sigmoid(-ff)
        bcum = _tril_cumsum(tril_ref[...], logf * LOG2E)
        prep.append((q, k, i_ref[0][:, sl], bcum))
    sts = [st_ref[hh] for hh in range(hb)]
    outs = [[] for _ in range(hb)]
    for n in range(tt // SUB):
        rs = slice(n * SUB, (n + 1) * SUB)
        for hh in range(hb):
            q, k, v, bcum = prep[hh]
            st = sts[hh]
            bq, qn, kn, vn = bcum[rs], q[rs], k[rs], v[rs]
            bend = bq[SUB - 1:SUB]
            qd = (qn * jnp.exp2(bq)).astype(BF16)
            o = lax.dot_general(qd, st.astype(BF16), NT_DIMS, preferred_element_type=F32)
            a = jnp.zeros((SUB, LANES), F32)
            for s in range(SUB):
                wgt = kn[s:s + 1] * jnp.exp2(jnp.minimum(bq - bq[s:s + 1], 0.0))
                a = jnp.where(lanei == s, jnp.sum(qn * wgt, axis=1, keepdims=True), a)
            a = jnp.where(lanei <= rowi, a, 0.0)[:, :SUB]
            o = o + jnp.dot(a.astype(BF16), vn.astype(BF16), preferred_element_type=F32)
            kd = (kn * jnp.exp2(bend - bq)).astype(BF16)
            sts[hh] = st * jnp.exp2(bend) + lax.dot_general(vn.astype(BF16), kd, TN_DIMS,
                                                           preferred_element_type=F32)
            outs[hh].append(o)
    for hh in range(hb):
        sl = slice(hh * LANES, (hh + 1) * LANES)
        st_ref[hh] = sts[hh]
        gg = g_ref[0][:, sl]
        o = jnp.concatenate(outs[hh], axis=0)
        o = _rms(o) * on_ref[:, sl] * (gg * _sigmoid(gg))
        o_ref[0, :, sl] = o.astype(o_ref.dtype)


def _hgrn_mixer(pc, hgrn_lb, onorm, layer, hb=12, tt=128):
    bsz, l, _ = pc.shape
    depth = hgrn_lb.shape[0]
    tt = min(tt, l)
    ng = H_C // hb
    wblk = hb * LANES
    tril = jnp.kron(jnp.eye(tt // SUB, dtype=F32), jnp.tril(jnp.ones((SUB, SUB), F32))).astype(BF16)
    col = lambda j: pl.BlockSpec((1, tt, wblk), lambda b, h, i, j=j: (b, i, j * ng + h))
    return pl.pallas_call(
        functools.partial(_hgrn_kernel, layer=layer, hb=hb, tt=tt),
        grid=(bsz, ng, l // tt),
        in_specs=[col(0), col(1), col(2), col(3),
                  pl.BlockSpec((depth, wblk), lambda b, h, i: (0, h)),
                  pl.BlockSpec((1, wblk), lambda b, h, i: (0, h)),
                  pl.BlockSpec((tt, tt), lambda b, h, i: (0, 0))],
        out_specs=pl.BlockSpec((1, tt, wblk), lambda b, h, i: (b, i, h)),
        out_shape=jax.ShapeDtypeStruct((bsz, l, W_C), BF16),
        scratch_shapes=[pltpu.VMEM((hb, DV_C, DK_C), F32)],
        compiler_params=_cparams("parallel", "parallel", "arbitrary"),
        name="hgrn2",
    )(pc, pc, pc, pc, hgrn_lb, onorm.reshape(1, W_C), tril)


def _pack_w_in(w):
    d = w.shape[0]
    a_cols = R_Q + R_KV + D_I + H_I
    b_cols = 3 * W_B + LORA_W + LORA_A + LORA_G
    wa, wb, wc = w[:, :a_cols], w[:, a_cols:a_cols + b_cols], w[:, a_cols + b_cols:]
    z = lambda n: jnp.zeros((d, n), w.dtype)
    o = R_Q + R_KV
    wa = jnp.concatenate([wa[:, :o + D_I], z(LANES - D_I), wa[:, o + D_I:], z(LANES - H_I)], axis=1)
    wb = jnp.concatenate([wb, z(LORA_G_PAD - LORA_G)], axis=1)
    return wa.astype(BF16), wb.astype(BF16), wc.astype(BF16)


def kernel(x, c, rel_bias, hgrn_lb, ada_w, ada_b, norm_g, w_in, w_out, mla_q_norm, mla_kv_norm, w_uq, w_uk, w_uv, w_qidx, rwkv_mu, rwkv_w0, rwkv_w_up, rwkv_a0, rwkv_a_up, rwkv_g_up, rwkv_k_k, rwkv_k_a, rwkv_r_k, rwkv_lnx_w, rwkv_lnx_b, hgrn_onorm, w_ff1, w_ff2):
    bsz, l, d = x.shape
    depth = ada_w.shape[0]
    m = bsz * l
    mod = _ada_mod(c, ada_w, ada_b)
    mods = [[mod[i, :, None, j * d:(j + 1) * d] for j in range(6)] for i in range(depth)]
    bias = _bias_tiles(rel_bias)
    h = _norm_mod(x, norm_g[0, 0], mods[0][1], mods[0][0])
    for i in range(depth):
        sh_m, sc_m, g_m, sh_f, sc_f, g_f = mods[i]
        wa, wb, wc = _pack_w_in(w_in[i])
        h2d = h.reshape(m, d)
        pa = _matmul(h2d, wa, F32, "in_proj_a").reshape(bsz, l, -1)
        pb = _matmul(h2d, wb, F32, "in_proj_b").reshape(bsz, l, -1)
        pc = _matmul(h2d, wc, F32, "in_proj_c").reshape(bsz, l, -1)
        wuq = w_uq[i].reshape(R_Q, H_A * DH_A).astype(BF16)
        wqi = jnp.transpose(w_qidx[i], (1, 2, 0)).astype(BF16)
        qlat, qidx, widx, kidx, ckv = _dsa_prep(pa, mla_q_norm[i], mla_kv_norm[i], wuq,
                                                w_uk[i].astype(BF16), wqi)
        y_a = _dsa_attn(qlat, qidx, widx, kidx, ckv, w_uv[i].astype(BF16), bias).transpose(0, 2, 1)
        mu = jnp.concatenate([rwkv_mu[i], jnp.zeros((LORA_G_PAD - LORA_G,), F32)]).reshape(1, B_PAD)
        gup = jnp.concatenate([rwkv_g_up[i], jnp.zeros((LORA_G_PAD - LORA_G, W_B), F32)]).astype(BF16)
        y_b = _rwkv_mixer(pb, mu, rwkv_w0[i], rwkv_w_up[i], rwkv_a0[i], rwkv_a_up[i], gup,
                          rwkv_k_k[i], rwkv_k_a[i], rwkv_r_k[i], rwkv_lnx_w[i], rwkv_lnx_b[i])
        y_c = _hgrn_mixer(pc, hgrn_lb, hgrn_onorm[i], i)
        ycat = jnp.concatenate([y_a, y_b, y_c], axis=-1).reshape(m, -1)
        y = _matmul(ycat, w_out[i].astype(BF16), F32, "out_proj").reshape(bsz, l, d)
        x, h = _resid_norm(x, y, g_m, norm_g[i, 1], (norm_g[i, 2], sc_f, sh_f))
        u = _matmul(h.reshape(m, d), w_ff1[i].astype(BF16), BF16, "ffn_up", relu2=True)
        y = _matmul(u, w_ff2[i].astype(BF16), F32, "ffn_down").reshape(bsz, l, d)
        nxt = None if i == depth - 1 else (norm_g[i + 1, 0], mods[i + 1][1], mods[i + 1][0])
        x, h = _resid_norm(x, y, g_f, norm_g[i, 3], nxt)
    return x
```
